```python
import math
import jax, jax.numpy as jnp
from jax import lax
import numpy as np

D_MODEL = 1024
BATCH = 16
SEQ = 2048
DEPTH = 1
DEC_BATCH = 32
DEC_SEQ = 16
PAST_LEN = 1024

CHUNK = 64
A_HEADS = 16
A_HEAD_DIM = D_MODEL // A_HEADS
A_LEFT_CHUNKS = 8
A_WINDOW = A_LEFT_CHUNKS * CHUNK
A_BAND = A_WINDOW + CHUNK
A_REL_CLIP = 128
B_HEADS = 8
B_HEAD_DIM = D_MODEL // B_HEADS
B_CONV = 4
MEM_TOKENS = 256
C_HEADS = 4
C_HEAD_DIM = D_MODEL // C_HEADS
D_FF = 4 * D_MODEL

A_WIDTH = A_HEADS * A_HEAD_DIM
B_WIDTH = B_HEADS * B_HEAD_DIM
C_WIDTH = C_HEADS * C_HEAD_DIM
N_BRANCH = 3
IN_COLS = 3 * A_WIDTH + 3 * B_WIDTH + C_WIDTH + N_BRANCH * D_MODEL + 2 * B_HEADS
DN_ALPHA = (2.0 * DEPTH) ** 0.25
DN_BETA = (8.0 * DEPTH) ** -0.25
LN_EPS = 1e-5
RMS_EPS = 1e-6
L2_EPS = 1e-6
NEG_INF = -1e30

kernel_name = 'hybrid_stream_chunkband_gdn_mem'


def _layer_norm(x, g, b):
    xf = x.astype(jnp.float32)
    mu = jnp.mean(xf, axis=-1, keepdims=True)
    var = jnp.mean(jnp.square(xf - mu), axis=-1, keepdims=True)
    return ((xf - mu) * lax.rsqrt(var + LN_EPS) * g.astype(jnp.float32) + b.astype(jnp.float32)).astype(x.dtype)


def _l2norm(x):
    return x * lax.rsqrt(jnp.sum(jnp.square(x), axis=-1, keepdims=True) + L2_EPS)


def _split_in(z):
    sizes = (A_WIDTH, A_WIDTH, A_WIDTH, 3 * B_WIDTH, C_WIDTH, N_BRANCH * D_MODEL, B_HEADS, B_HEADS)
    offs = tuple(int(o) for o in np.cumsum(sizes)[:-1])
    return jnp.split(z, offs, axis=-1)


def _band_attention(q, k, v, q_pos, k_pos, k_valid, rel_bias):
    s = jnp.einsum('bqhd,bkhd->bhqk', q, k).astype(jnp.float32) * (A_HEAD_DIM ** -0.5)
    rel = jnp.clip(q_pos[:, None] - k_pos[None, :], -A_REL_CLIP, A_REL_CLIP) + A_REL_CLIP
    s = s + rel_bias[:, rel].astype(jnp.float32)[None]
    s = jnp.where(k_valid[None, None, None, :], s, NEG_INF)
    p = jax.nn.softmax(s, axis=-1).astype(v.dtype)
    return jnp.einsum('bhqk,bkhd->bqhd', p, v)


def _chunk_band_prompt(q, k, v, rel_bias):
    bsz, t = q.shape[0], q.shape[1]
    nc = t // CHUNK
    pad = ((0, 0), (A_WINDOW, 0), (0, 0), (0, 0))
    kp = jnp.pad(k, pad)
    vp = jnp.pad(v, pad)
    qc = jnp.moveaxis(q.reshape(bsz, nc, CHUNK, A_HEADS, A_HEAD_DIM), 1, 0)

    def one_chunk(args):
        c, qb = args
        start = c * CHUNK
        kb = lax.dynamic_slice_in_dim(kp, start, A_BAND, axis=1)
        vb = lax.dynamic_slice_in_dim(vp, start, A_BAND, axis=1)
        q_pos = start + jnp.arange(CHUNK)
        k_pos = start - A_WINDOW + jnp.arange(A_BAND)
        return _band_attention(qb, kb, vb, q_pos, k_pos, k_pos >= 0, rel_bias)

    out = lax.map(one_chunk, (jnp.arange(nc), qc))
    return jnp.moveaxis(out, 0, 1).reshape(bsz, t, A_WIDTH)


def _causal_conv(x, buf, w):
    t = x.shape[1]
    xp = jnp.concatenate([buf.astype(x.dtype), x], axis=1)
    y = xp[:, 0:t] * w[0]
    for i in range(1, B_CONV):
        y = y + xp[:, i:i + t] * w[i]
    return jax.nn.silu(y), xp[:, xp.shape[1] - (B_CONV - 1):]


def _gated_delta(q, k, v, g, beta, state):
    bsz, t, h, dk = k.shape
    dv = v.shape[-1]
    c = min(CHUNK, t)
    nc = t // c

    def blk(a):
        return jnp.swapaxes(a.reshape((bsz, nc, c) + a.shape[2:]), 2, 3)

    qc, kc, vc, gc, bc = blk(q), blk(k), blk(v), blk(g), blk(beta)
    G = jnp.cumsum(gc, axis=-1)
    diff = G[..., :, None] - G[..., None, :]
    tri_strict = jnp.tril(jnp.ones((c, c), dtype=bool), -1)
    tri_incl = jnp.tril(jnp.ones((c, c), dtype=bool))
    dec_strict = jnp.exp(jnp.where(tri_strict, diff, -jnp.inf))
    dec_incl = jnp.exp(jnp.where(tri_incl, diff, -jnp.inf))
    a_kk = bc[..., :, None] * jnp.einsum('bnhtd,bnhsd->bnhts', kc, kc) * dec_strict
    lower = jnp.eye(c, dtype=q.dtype) + a_kk
    rhs = jnp.concatenate([bc[..., None] * vc, (bc * jnp.exp(G))[..., None] * kc], axis=-1)
    sol = lax.linalg.triangular_solve(lower, rhs, left_side=True, lower=True, unit_diagonal=True)
    u_blk, w_blk = sol[..., :dv], sol[..., dv:]
    a_qk = jnp.einsum('bnhtd,bnhsd->bnhts', qc, kc) * dec_incl
    q_dec = qc * jnp.exp(G)[..., None]
    k_dec = kc * jnp.exp(G[..., -1:] - G)[..., None]
    g_end = jnp.exp(G[..., -1])

    def step(s, inp):
        u_n, w_n, qd_n, aqk_n, kd_n, ge_n = inp
        wv = u_n - jnp.einsum('bhtk,bhkv->bhtv', w_n, s)
        o = jnp.einsum('bhtk,bhkv->bhtv', qd_n, s) + jnp.einsum('bhts,bhsv->bhtv', aqk_n, wv)
        s = ge_n[..., None, None] * s + jnp.einsum('bhsk,bhsv->bhkv', kd_n, wv)
        return s, o

    xs = tuple(jnp.moveaxis(a, 1, 0) for a in (u_blk, w_blk, q_dec, a_qk, k_dec, g_end))
    s_final, o = lax.scan(step, state, xs)
    o = jnp.transpose(o, (1, 0, 3, 2, 4)).reshape(bsz, t, h, dv)
    return o, s_final


def _mixer_b(qkv_b, a_b, b_b, conv_buf, state, conv_w, a_log, dt_bias, norm_g):
    bsz, t, _ = qkv_b.shape
    y, new_buf = _causal_conv(qkv_b, conv_buf, conv_w)
    y = y.astype(jnp.float32).reshape(bsz, t, 3, B_HEADS, B_HEAD_DIM)
    q = _l2norm(y[:, :, 0]) * (B_HEAD_DIM ** -0.5)
    k = _l2norm(y[:, :, 1])
    v = y[:, :, 2]
    g = -jnp.exp(a_log.astype(jnp.float32)) * jax.nn.softplus(a_b.astype(jnp.float32) + dt_bias.astype(jnp.float32))
    beta = jax.nn.sigmoid(b_b.astype(jnp.float32))
    o, new_state = _gated_delta(q, k, v, g, beta, state.astype(jnp.float32))
    o = o * lax.rsqrt(jnp.mean(jnp.square(o), axis=-1, keepdims=True) + RMS_EPS) * norm_g.astype(jnp.float32)
    return o.reshape(bsz, t, B_WIDTH).astype(qkv_b.dtype), new_buf, new_state.astype(qkv_b.dtype)


def _mem_kv(mem, w_kv):
    bsz, m, _ = mem.shape
    kv = (mem @ w_kv).reshape(bsz, m, 2, C_HEADS, C_HEAD_DIM)
    return kv[:, :, 0], kv[:, :, 1]


def _mem_attention(q, mk, mv):
    s = jnp.einsum('bqhd,bmhd->bhqm', q, mk).astype(jnp.float32) * (C_HEAD_DIM ** -0.5)
    p = jax.nn.softmax(s, axis=-1).astype(mv.dtype)
    o = jnp.einsum('bhqm,bmhd->bqhd', p, mv)
    return o.reshape(q.shape[0], q.shape[1], C_WIDTH)


def _merge_and_ffn(x, o_a, o_b, o_c, gate_logits, w_out, ln1_g, ln1_b, w_ff1, b_ff1, w_ff2, b_ff2, ln2_g, ln2_b):
    bsz, t, _ = x.shape
    gates = jax.nn.sigmoid(gate_logits.reshape(bsz, t, N_BRANCH, D_MODEL))
    merged = jnp.sum(gates * jnp.stack([o_a, o_b, o_c], axis=2), axis=2)
    h = _layer_norm(DN_ALPHA * x + merged @ w_out, ln1_g, ln1_b)
    f = jnp.square(jax.nn.relu(h @ w_ff1 + b_ff1)) @ w_ff2 + b_ff2
    return _layer_norm(DN_ALPHA * h + f, ln2_g, ln2_b)


def setup_inputs(seed: int = 0) -> dict:
    key = jax.random.key(seed)
    ks = jax.random.split(key, 26)
    f32 = jnp.float32
    a_cache = min(A_WINDOW, PAST_LEN)

    def nrm(k, shape, scale=1.0):
        return jax.random.normal(k, shape, f32) * scale

    dt = jnp.exp(jax.random.uniform(ks[12], (DEPTH, B_HEADS), f32, math.log(1e-3), math.log(1e-1)))
    return {
        'x_prompt': nrm(ks[0], (BATCH, SEQ, D_MODEL)),
        'x_sample': nrm(ks[1], (DEC_BATCH, DEC_SEQ, D_MODEL)),
        'cache_a_k': nrm(ks[2], (DEPTH, DEC_BATCH, a_cache, A_HEADS, A_HEAD_DIM)),
        'cache_a_v': nrm(ks[3], (DEPTH, DEC_BATCH, a_cache, A_HEADS, A_HEAD_DIM)),
        'state_b_conv': nrm(ks[4], (DEPTH, DEC_BATCH, B_CONV - 1, 3 * B_WIDTH)),
        'state_b_ssm': nrm(ks[5], (DEPTH, DEC_BATCH, B_HEADS, B_HEAD_DIM, B_HEAD_DIM), 0.1),
        'cache_mem_k': nrm(ks[6], (DEPTH, DEC_BATCH, MEM_TOKENS, C_HEADS, C_HEAD_DIM)),
        'cache_mem_v': nrm(ks[7], (DEPTH, DEC_BATCH, MEM_TOKENS, C_HEADS, C_HEAD_DIM)),
        'mem_prompt': nrm(ks[8], (BATCH, MEM_TOKENS, D_MODEL)),
        'w_in': nrm(ks[9], (DEPTH, D_MODEL, IN_COLS), D_MODEL ** -0.5),
        'w_b_conv': nrm(ks[10], (DEPTH, B_CONV, 3 * B_WIDTH), B_CONV ** -0.5),
        'b_a_log': jnp.log(jax.random.uniform(ks[11], (DEPTH, B_HEADS), f32, 1.0, 16.0)),
        'b_dt_bias': dt + jnp.log(-jnp.expm1(-dt)),
        'b_norm_g': 1.0 + nrm(ks[13], (DEPTH, B_HEAD_DIM), 0.05),
        'a_rel_bias': nrm(ks[14], (DEPTH, A_HEADS, 2 * A_REL_CLIP + 1), 0.5),
        'w_mem_kv': nrm(ks[15], (DEPTH, D_MODEL, 2 * C_WIDTH), D_MODEL ** -0.5),
        'w_out': nrm(ks[16], (DEPTH, D_MODEL, D_MODEL), D_MODEL ** -0.5 * DN_BETA),
        'ln1_g': 1.0 + nrm(ks[17], (DEPTH, D_MODEL), 0.05),
        'ln1_b': nrm(ks[18], (DEPTH, D_MODEL), 0.05),
        'w_ff1': nrm(ks[19], (DEPTH, D_MODEL, D_FF), D_MODEL ** -0.5),
        'b_ff1': nrm(ks[20], (DEPTH, D_FF), 0.05),
        'w_ff2': nrm(ks[21], (DEPTH, D_FF, D_MODEL), D_FF ** -0.5 * DN_BETA),
        'b_ff2': nrm(ks[22], (DEPTH, D_MODEL), 0.05),
        'ln2_g': 1.0 + nrm(ks[23], (DEPTH, D_MODEL), 0.05),
        'ln2_b': nrm(ks[24], (DEPTH, D_MODEL), 0.05),
    }


def reference(x_prompt, x_sample, cache_a_k, cache_a_v, state_b_conv, state_b_ssm,
              cache_mem_k, cache_mem_v, mem_prompt, w_in, w_b_conv, b_a_log, b_dt_bias,
              b_norm_g, a_rel_bias, w_mem_kv, w_out, ln1_g, ln1_b, w_ff1, b_ff1,
              w_ff2, b_ff2, ln2_g, ln2_b):
    bp, tp, _ = x_prompt.shape
    bs, tn, _ = x_sample.shape
    keep_p = min(A_WINDOW, tp)
    lc = cache_a_k.shape[2]
    xp, xs = x_prompt, x_sample
    a_k_p, a_v_p, conv_p, ssm_p, mk_p, mv_p = [], [], [], [], [], []
    a_k_s, a_v_s, conv_s, ssm_s = [], [], [], []
    for l in range(DEPTH):
        ffn_w = (w_out[l], ln1_g[l], ln1_b[l], w_ff1[l], b_ff1[l], w_ff2[l], b_ff2[l], ln2_g[l], ln2_b[l])
        q_a, k_a, v_a, qkv_b, q_c, g_logit, a_b, b_b = _split_in(xp @ w_in[l])
        q_a, k_a, v_a = (t.reshape(bp, tp, A_HEADS, A_HEAD_DIM) for t in (q_a, k_a, v_a))
        o_a = _chunk_band_prompt(q_a, k_a, v_a, a_rel_bias[l])
        conv0 = jnp.zeros((bp, B_CONV - 1, 3 * B_WIDTH), xp.dtype)
        ssm0 = jnp.zeros((bp, B_HEADS, B_HEAD_DIM, B_HEAD_DIM), jnp.float32)
        o_b, cbuf, ssm = _mixer_b(qkv_b, a_b, b_b, conv0, ssm0, w_b_conv[l], b_a_log[l], b_dt_bias[l], b_norm_g[l])
        mk, mv = _mem_kv(mem_prompt, w_mem_kv[l])
        o_c = _mem_attention(q_c.reshape(bp, tp, C_HEADS, C_HEAD_DIM), mk, mv)
        xp = _merge_and_ffn(xp, o_a, o_b, o_c, g_logit, *ffn_w)
        a_k_p.append(k_a[:, tp - keep_p:])
        a_v_p.append(v_a[:, tp - keep_p:])
        conv_p.append(cbuf)
        ssm_p.append(ssm)
        mk_p.append(mk)
        mv_p.append(mv)
        q_a, k_a, v_a, qkv_b, q_c, g_logit, a_b, b_b = _split_in(xs @ w_in[l])
        q_a, k_a, v_a = (t.reshape(bs, tn, A_HEADS, A_HEAD_DIM) for t in (q_a, k_a, v_a))
        k_all = jnp.concatenate([cache_a_k[l], k_a], axis=1)
        v_all = jnp.concatenate([cache_a_v[l], v_a], axis=1)
        q_pos = PAST_LEN + jnp.arange(tn)
        k_pos = PAST_LEN - lc + jnp.arange(lc + tn)
        o_a = _band_attention(q_a, k_all, v_all, q_pos, k_pos, k_pos >= 0, a_rel_bias[l]).reshape(bs, tn, A_WIDTH)
        o_b, cbuf, ssm = _mixer_b(qkv_b, a_b, b_b, state_b_conv[l], state_b_ssm[l], w_b_conv[l], b_a_log[l], b_dt_bias[l], b_norm_g[l])
        o_c = _mem_attention(q_c.reshape(bs, tn, C_HEADS, C_HEAD_DIM), cache_mem_k[l], cache_mem_v[l])
        xs = _merge_and_ffn(xs, o_a, o_b, o_c, g_logit, *ffn_w)
        a_k_s.append(k_all[:, tn:])
        a_v_s.append(v_all[:, tn:])
        conv_s.append(cbuf)
        ssm_s.append(ssm)
    return (xp, xs,
            jnp.stack(a_k_p), jnp.stack(a_v_p), jnp.stack(conv_p), jnp.stack(ssm_p),
            jnp.stack(mk_p), jnp.stack(mv_p),
            jnp.stack(a_k_s), jnp.stack(a_v_s), jnp.stack(conv_s), jnp.stack(ssm_s))
```

```python
import functools

import jax
import jax.numpy as jnp
from jax import lax
from jax.experimental import pallas as pl
from jax.experimental.pallas import tpu as pltpu

F32 = jnp.float32
BF16 = jnp.bfloat16

CHUNK = 64
A_HEADS = 16
A_HEAD_DIM = 64
A_GROUP = 4
A_LEFT_CHUNKS = 8
A_REL_CLIP = 128
B_HEADS = 8
B_HEAD_DIM = 128
B_CONV = 4
C_HEADS = 4
C_HEAD_DIM = 256
N_BRANCH = 3
LN_EPS = 1e-5
RMS_EPS = 1e-6
L2_EPS = 1e-6
NEG_INF = -1e30

LANES = 128
SUBLANES = 8
VMEM_LIMIT_BYTES = 56 * 1024 * 1024


def _params(*sem):
    return pltpu.CompilerParams(dimension_semantics=sem, vmem_limit_bytes=VMEM_LIMIT_BYTES)


def _dot(a, b):
    return jnp.dot(a, b, preferred_element_type=F32)


def _dot_nt(a, b):
    return lax.dot_general(a, b, (((1,), (1,)), ((), ())), preferred_element_type=F32)


def _dot_tn(a, b):
    return lax.dot_general(a, b, (((0,), (0,)), ((), ())), preferred_element_type=F32)


def _sigmoid(x):
    return 1.0 / (1.0 + jnp.exp(-x))


def _inproj_kernel(x_ref, w_ref, wab_ref, z_ref, ab_ref, xb_ref):
    @pl.when(pl.program_id(1) == 0)
    def _():
        xb = x_ref[...].astype(BF16)
        xb_ref[...] = xb
        ab_ref[...] = _dot(xb, wab_ref[...])

    z_ref[...] = _dot(xb_ref[...], w_ref[...]).astype(z_ref.dtype)


def _inproj(x2, wz, wab, out_dtype, bm, bn):
    m, d = x2.shape
    n = wz.shape[1]
    return pl.pallas_call(
        _inproj_kernel,
        grid=(m // bm, n // bn),
        in_specs=[
            pl.BlockSpec((bm, d), lambda i, j: (i, 0)),
            pl.BlockSpec((d, bn), lambda i, j: (0, j)),
            pl.BlockSpec((d, LANES), lambda i, j: (0, 0)),
        ],
        out_specs=[
            pl.BlockSpec((bm, bn), lambda i, j: (i, j)),
            pl.BlockSpec((bm, LANES), lambda i, j: (i, 0)),
        ],
        out_shape=[
            jax.ShapeDtypeStruct((m, n), out_dtype),
            jax.ShapeDtypeStruct((m, LANES), F32),
        ],
        scratch_shapes=[pltpu.VMEM((bm, d), BF16)],
        compiler_params=_params("arbitrary", "arbitrary"),
        name="inproj",
    )(x2, wz, wab)


def _rows_matmul_kernel(x_ref, w_ref, o_ref):
    o_ref[...] = _dot(x_ref[...].astype(BF16), w_ref[...])


def _rows_matmul(x3, w, t_blk, t_idx, n_blk, n_idx0, n_cnt):
    bsz, _, d = x3.shape
    return pl.pallas_call(
        _rows_matmul_kernel,
        grid=(n_cnt, bsz),
        in_specs=[
            pl.BlockSpec((None, t_blk, d), lambda j, b: (b, t_idx, 0)),
            pl.BlockSpec((d, n_blk), lambda j, b: (0, n_idx0 + j)),
        ],
        out_specs=pl.BlockSpec((None, None, t_blk, n_blk), lambda j, b: (j, b, 0, 0)),
        out_shape=jax.ShapeDtypeStruct((n_cnt, bsz, t_blk, n_blk), F32),
        compiler_params=_params("arbitrary", "arbitrary"),
        name="rows_matmul",
    )(x3, w)


def _bias_table_kernel(rb_ref, o_ref, *, window):
    h = pl.program_id(0)
    tq, band = o_ref.shape
    i = lax.broadcasted_iota(jnp.int32, (tq, band), 0)
    j = lax.broadcasted_iota(jnp.int32, (tq, band), 1)
    idx = jnp.clip(i - j + window, -A_REL_CLIP, A_REL_CLIP) + A_REL_CLIP
    lowest = max(-A_REL_CLIP, window - (band - 1)) + A_REL_CLIP

    def body(r, acc):
        return jnp.where(idx == r, rb_ref[h, r], acc)

    o_ref[...] = lax.fori_loop(lowest, 2 * A_REL_CLIP + 1, body, jnp.zeros((tq, band), F32))


def _bias_table(rel_bias, tq, band):
    heads = rel_bias.shape[0]
    return pl.pallas_call(
        functools.partial(_bias_table_kernel, window=band - tq),
        grid=(heads,),
        in_specs=[pl.BlockSpec(memory_space=pltpu.SMEM)],
        out_specs=pl.BlockSpec((None, tq, band), lambda h: (h, 0, 0)),
        out_shape=jax.ShapeDtypeStruct((heads, tq, band), F32),
        compiler_params=_params("arbitrary"),
        name="bias_table",
    )(rel_bias)


def _attn_groups(q_ref, k_get, v_get, bias_ref, o_ref, valid_from):
    tq = q_ref.shape[0]
    gw = A_GROUP * A_HEAD_DIM
    lane_head = jnp.right_shift(lax.broadcasted_iota(jnp.int32, (tq, gw), 1), A_HEAD_DIM.bit_length() - 1)
    for g in range(A_HEADS // A_GROUP):
        qg = q_ref[:, g * gw:(g + 1) * gw].astype(BF16)
        zero = jnp.zeros_like(qg)
        qm = jnp.concatenate([jnp.where(lane_head == h, qg, zero) for h in range(A_GROUP)], axis=0)
        s = _dot_nt(qm, k_get(g)) * (A_HEAD_DIM ** -0.5) + bias_ref[g]
        if valid_from is not None:
            col = lax.broadcasted_iota(jnp.int32, s.shape, 1)
            s = jnp.where(col >= valid_from, s, NEG_INF)
        m = jnp.max(s, axis=-1, keepdims=True)
        p = jnp.exp(s - m)
        l = jnp.sum(p, axis=-1, keepdims=True)
        o = _dot(p.astype(BF16), v_get(g)) * (1.0 / l)
        out = jnp.zeros((tq, gw), F32)
        for h in range(A_GROUP):
            out = jnp.where(lane_head == h, o[h * tq:(h + 1) * tq], out)
        o_ref[:, g * gw:(g + 1) * gw] = out.astype(o_ref.dtype)


def _attn_prompt_kernel(q_ref, k_ref, v_ref, bias_ref, o_ref, kpad_ref, vpad_ref, *, window):
    c = pl.program_id(1)
    tq = q_ref.shape[0]
    band = window + tq
    gw = A_GROUP * A_HEAD_DIM

    @pl.when(c == 0)
    def _():
        zeros = jnp.zeros((window, kpad_ref.shape[1]), BF16)
        kpad_ref[0:window, :] = zeros
        vpad_ref[0:window, :] = zeros
        kpad_ref[window:, :] = k_ref[...]
        vpad_ref[window:, :] = v_ref[...]

    start = pl.multiple_of(c * tq, tq)
    _attn_groups(
        q_ref,
        lambda g: kpad_ref[pl.ds(start, band), g * gw:(g + 1) * gw],
        lambda g: vpad_ref[pl.ds(start, band), g * gw:(g + 1) * gw],
        bias_ref, o_ref, window - c * tq)


def _attn_prompt(z3, bias_g, tq, window):
    bsz, t, _ = z3.shape
    width = A_HEADS * A_HEAD_DIM
    return pl.pallas_call(
        functools.partial(_attn_prompt_kernel, window=window),
        grid=(bsz, t // tq),
        in_specs=[
            pl.BlockSpec((None, tq, width), lambda b, c: (b, c, 0)),
            pl.BlockSpec((None, t, width), lambda b, c: (b, 0, 1)),
            pl.BlockSpec((None, t, width), lambda b, c: (b, 0, 2)),
            pl.BlockSpec(bias_g.shape, lambda b, c: (0, 0, 0)),
        ],
        out_specs=pl.BlockSpec((None, tq, width), lambda b, c: (b, c, 0)),
        out_shape=jax.ShapeDtypeStruct((bsz, t, width), BF16),
        scratch_shapes=[pltpu.VMEM((t + window, width), BF16), pltpu.VMEM((t + window, width), BF16)],
        compiler_params=_params("arbitrary", "arbitrary"),
        name="attn_prompt",
    )(z3, z3, z3, bias_g)


def _attn_sample_kernel(q_ref, k_ref, v_ref, bias_ref, o_ref):
    gw = A_GROUP * A_HEAD_DIM
    _attn_groups(
        q_ref,
        lambda g: k_ref[:, g * gw:(g + 1) * gw].astype(BF16),
        lambda g: v_ref[:, g * gw:(g + 1) * gw].astype(BF16),
        bias_ref, o_ref, None)


def _attn_sample(z3, k_all, v_all, bias_g):
    bsz, tq, _ = z3.shape
    band = k_all.shape[1]
    width = A_HEADS * A_HEAD_DIM
    return pl.pallas_call(
        _attn_sample_kernel,
        grid=(bsz,),
        in_specs=[
            pl.BlockSpec((None, tq, width), lambda b: (b, 0, 0)),
            pl.BlockSpec((None, band, width), lambda b: (b, 0, 0)),
            pl.BlockSpec((None, band, width), lambda b: (b, 0, 0)),
            pl.BlockSpec(bias_g.shape, lambda b: (0, 0, 0)),
        ],
        out_specs=pl.BlockSpec((None, tq, width), lambda b: (b, 0, 0)),
        out_shape=jax.ShapeDtypeStruct((bsz, tq, width), BF16),
        compiler_params=_params("arbitrary"),
        name="attn_sample",
    )(z3, k_all, v_all, bias_g)


def _split3(x):
    x1 = x.astype(BF16)
    r1 = x - x1.astype(F32)
    x2 = r1.astype(BF16)
    x3 = (r1 - x2.astype(F32)).astype(BF16)
    return x1, x2, x3


def _gdn_kernel(x_ref, ab_ref, abt_ref, conv0_ref, s0_ref, cw_ref, pcol_ref, prow_ref, ng_ref,
                o_ref, s_ref, xbuf_ref):
    c = pl.program_id(1)
    ch = x_ref.shape[0]
    hd = B_HEAD_DIM
    width = B_HEADS * hd
    keep = B_CONV - 1
    top = SUBLANES

    @pl.when(c == 0)
    def _():
        xbuf_ref[top - keep:top, :] = conv0_ref[...].astype(F32)
        s_ref[...] = s0_ref[...].astype(F32)

    xbuf_ref[top:top + ch, :] = x_ref[...].astype(F32)

    ab = ab_ref[...]
    g_col = -jnp.exp(pcol_ref[0:1, :]) * jax.nn.softplus(ab + pcol_ref[1:2, :])
    beta_col = _sigmoid(ab)
    abt = abt_ref[...]
    g_row = -jnp.exp(prow_ref[0, :, 0:ch]) * jax.nn.softplus(abt + prow_ref[1, :, 0:ch])

    t_i = lax.broadcasted_iota(jnp.int32, (ch, ch), 0)
    s_i = lax.broadcasted_iota(jnp.int32, (ch, ch), 1)
    tri_incl = (t_i >= s_i)
    tri_strict = (t_i > s_i)
    lower01 = jnp.where(tri_incl, 1.0, 0.0).astype(BF16)
    upper01 = jnp.where(t_i <= s_i, 1.0, 0.0).astype(BF16)
    gcum_col = sum(_dot(lower01, p) for p in _split3(g_col))
    gcum_row = sum(_dot(p, upper01) for p in _split3(g_row))

    lane = lax.broadcasted_iota(jnp.int32, (ch, LANES), 1)
    lane_row = lax.broadcasted_iota(jnp.int32, (1, ch), 1)

    def pick(x, idx):
        return jnp.sum(jnp.where(lane == idx, x, 0.0), axis=-1, keepdims=True)

    def conv_silu(col0):
        acc = xbuf_ref[top:top + ch, col0:col0 + hd] * cw_ref[keep:keep + 1, col0:col0 + hd]
        for i in range(keep):
            acc = acc + (xbuf_ref[top - keep + i:top - keep + i + ch, col0:col0 + hd]
                         * cw_ref[i:i + 1, col0:col0 + hd])
        return acc * _sigmoid(acc)

    def l2norm(x):
        return x * lax.rsqrt(jnp.sum(x * x, axis=-1, keepdims=True) + L2_EPS)

    levels = max(1, (ch - 1).bit_length())
    for h in range(B_HEADS):
        q = l2norm(conv_silu(h * hd)) * (hd ** -0.5)
        k = l2norm(conv_silu(width + h * hd))
        v = conv_silu(2 * width + h * hd)

        gc = pick(gcum_col, h)
        bc = pick(beta_col, B_HEADS + h)
        gr = gcum_row[h:h + 1, :]
        glast = jnp.sum(jnp.where(lane_row == ch - 1, gr, 0.0), axis=-1, keepdims=True)
        dec_incl = jnp.exp(jnp.where(tri_incl, gc - gr, NEG_INF))
        dec_strict = jnp.where(tri_strict, dec_incl, 0.0)
        eg = jnp.exp(gc)

        kb = k.astype(BF16)
        qk_kk = _dot_nt(jnp.concatenate([q, k], axis=0).astype(BF16), kb)
        a_qk = qk_kk[0:ch] * dec_incl
        a_kk = bc * qk_kk[ch:2 * ch] * dec_strict

        x_sol = jnp.concatenate([bc * v, (bc * eg) * k], axis=1)
        npow = -a_kk
        for lvl in range(levels):
            npow_b = npow.astype(BF16)
            x_sol = x_sol + _dot(npow_b, x_sol.astype(BF16))
            if lvl + 1 < levels:
                npow = _dot(npow_b, npow_b)
        u_blk = x_sol[:, 0:hd]
        w_blk = x_sol[:, hd:2 * hd]

        state = s_ref[h]
        wq = _dot(jnp.concatenate([w_blk, q * eg], axis=0).astype(BF16), state.astype(BF16))
        wv = u_blk - wq[0:ch]
        wv_b = wv.astype(BF16)
        o = wq[ch:2 * ch] + _dot(a_qk.astype(BF16), wv_b)
        k_dec = k * jnp.exp(glast - gc)
        s_ref[h] = jnp.exp(glast) * state + _dot_tn(k_dec.astype(BF16), wv_b)

        o = o * lax.rsqrt(jnp.mean(o * o, axis=-1, keepdims=True) + RMS_EPS) * ng_ref[...]
        o_ref[:, h * hd:(h + 1) * hd] = o.astype(o_ref.dtype)

    xbuf_ref[top - keep:top, :] = xbuf_ref[top + ch - keep:top + ch, :]


def _gdn(z3, ab3, abt4, conv0, s0, conv_w, pcol, prow, norm_g, ch):
    bsz, t, _ = z3.shape
    width3 = 3 * B_HEADS * B_HEAD_DIM
    width = B_HEADS * B_HEAD_DIM
    return pl.pallas_call(
        _gdn_kernel,
        grid=(bsz, t // ch),
        in_specs=[
            pl.BlockSpec((None, ch, width3), lambda b, c: (b, c, 1)),
            pl.BlockSpec((None, ch, LANES), lambda b, c: (b, c, 0)),
            pl.BlockSpec((None, None, 2 * B_HEADS, ch), lambda b, c: (b, c, 0, 0)),
            pl.BlockSpec((None, B_CONV - 1, width3), lambda b, c: (b, 0, 0)),
            pl.BlockSpec((None, B_HEADS, B_HEAD_DIM, B_HEAD_DIM), lambda b, c: (b, 0, 0, 0)),
            pl.BlockSpec(conv_w.shape, lambda b, c: (0, 0)),
            pl.BlockSpec(pcol.shape, lambda b, c: (0, 0)),
            pl.BlockSpec(prow.shape, lambda b, c: (0, 0, 0)),
            pl.BlockSpec(norm_g.shape, lambda b, c: (0, 0)),
        ],
        out_specs=[
            pl.BlockSpec((None, ch, width), lambda b, c: (b, c, 0)),
            pl.BlockSpec((None, B_HEADS, B_HEAD_DIM, B_HEAD_DIM), lambda b, c: (b, 0, 0, 0)),
        ],
        out_shape=[
            jax.ShapeDtypeStruct((bsz, t, width), BF16),
            jax.ShapeDtypeStruct((bsz, B_HEADS, B_HEAD_DIM, B_HEAD_DIM), F32),
        ],
        scratch_shapes=[pltpu.VMEM((SUBLANES + ch, width3), F32)],
        compiler_params=_params("arbitrary", "arbitrary"),
        name="gdn",
    )(z3, ab3, abt4, conv0, s0, conv_w, pcol, prow, norm_g)


def _memattn_kernel(q_ref, mk_ref, mv_ref, o_ref, kb_ref, vb_ref):
    @pl.when(pl.program_id(1) == 0)
    def _():
        kb_ref[...] = mk_ref[...].astype(BF16)
        vb_ref[...] = mv_ref[...].astype(BF16)

    dh = C_HEAD_DIM
    for h in range(C_HEADS):
        q = q_ref[:, h * dh:(h + 1) * dh].astype(BF16)
        s = _dot_nt(q, kb_ref[:, h * dh:(h + 1) * dh]) * (dh ** -0.5)
        m = jnp.max(s, axis=-1, keepdims=True)
        p = jnp.exp(s - m)
        l = jnp.sum(p, axis=-1, keepdims=True)
        o = _dot(p.astype(BF16), vb_ref[:, h * dh:(h + 1) * dh]) * (1.0 / l)
        o_ref[:, h * dh:(h + 1) * dh] = o.astype(o_ref.dtype)


def _memattn(z3, q_blk, mk, mv, tq):
    bsz, t, _ = z3.shape
    mem = mk.shape[1]
    width = C_HEADS * C_HEAD_DIM
    return pl.pallas_call(
        _memattn_kernel,
        grid=(bsz, t // tq),
        in_specs=[
            pl.BlockSpec((None, tq, width), lambda b, i: (b, i, q_blk)),
            pl.BlockSpec((None, mem, width), lambda b, i: (b, 0, 0)),
            pl.BlockSpec((None, mem, width), lambda b, i: (b, 0, 0)),
        ],
        out_specs=pl.BlockSpec((None, tq, width), lambda b, i: (b, i, 0)),
        out_shape=jax.ShapeDtypeStruct((bsz, t, width), BF16),
        scratch_shapes=[pltpu.VMEM((mem, width), BF16), pltpu.VMEM((mem, width), BF16)],
        compiler_params=_params("arbitrary", "arbitrary"),
        name="memattn",
    )(z3, mk, mv)


def _layer_norm(x, g, b):
    mu = jnp.mean(x, axis=-1, keepdims=True)
    xc = x - mu
    var = jnp.mean(xc * xc, axis=-1, keepdims=True)
    return xc * lax.rsqrt(var + LN_EPS) * g + b


def _ffn_kernel(x_ref, ga_ref, gb_ref, gc_ref, oa_ref, ob_ref, oc_ref, wo_ref, w1_ref, w2_ref,
                vec_ref, b1_ref, y_ref, *, alpha, ff_blk):
    merged = (_sigmoid(ga_ref[...].astype(F32)) * oa_ref[...].astype(F32)
              + _sigmoid(gb_ref[...].astype(F32)) * ob_ref[...].astype(F32)
              + _sigmoid(gc_ref[...].astype(F32)) * oc_ref[...].astype(F32))
    proj = _dot(merged.astype(BF16), wo_ref[...])
    h = _layer_norm(alpha * x_ref[...] + proj, vec_ref[0:1, :], vec_ref[1:2, :])
    hb = h.astype(BF16)
    acc = jnp.zeros(h.shape, F32)
    for k0 in range(0, w1_ref.shape[1], ff_blk):
        f = _dot(hb, w1_ref[:, k0:k0 + ff_blk]) + b1_ref[:, k0:k0 + ff_blk]
        f = jnp.square(jnp.maximum(f, 0.0))
        acc = acc + _dot(f.astype(BF16), w2_ref[k0:k0 + ff_blk, :])
    y_ref[...] = _layer_norm(alpha * h + acc + vec_ref[2:3, :], vec_ref[3:4, :], vec_ref[4:5, :])


def _ffn(x2, z2, gate_blk0, oa, ob, oc, wo, w1, w2, vec, b1, alpha, bm):
    m, d = x2.shape
    dff = w1.shape[1]
    const = dict(pipeline_mode=pl.Buffered(1))
    row = lambda i: (i, 0)
    return pl.pallas_call(
        functools.partial(_ffn_kernel, alpha=alpha, ff_blk=1024),
        grid=(m // bm,),
        in_specs=[
            pl.BlockSpec((bm, d), row),
            pl.BlockSpec((bm, d), lambda i: (i, gate_blk0)),
            pl.BlockSpec((bm, d), lambda i: (i, gate_blk0 + 1)),
            pl.BlockSpec((bm, d), lambda i: (i, gate_blk0 + 2)),
            pl.BlockSpec((bm, d), row),
            pl.BlockSpec((bm, d), row),
            pl.BlockSpec((bm, d), row),
            pl.BlockSpec((d, d), lambda i: (0, 0), **const),
            pl.BlockSpec((d, dff), lambda i: (0, 0), **const),
            pl.BlockSpec((dff, d), lambda i: (0, 0), **const),
            pl.BlockSpec(vec.shape, lambda i: (0, 0), **const),
            pl.BlockSpec(b1.shape, lambda i: (0, 0), **const),
        ],
        out_specs=pl.BlockSpec((bm, d), row),
        out_shape=jax.ShapeDtypeStruct((m, d), F32),
        compiler_params=_params("arbitrary"),
        name="merge_ffn",
    )(x2, z2, z2, z2, oa, ob, oc, wo, w1, w2, vec, b1)


def _group_layer(x3, z_dtype, attn_fn, conv0, ssm0, mem_k, mem_v, lw):
    bsz, t, d = x3.shape
    m = bsz * t
    ch = min(CHUNK, t)
    x2 = x3.reshape(m, d)
    z2, ab = _inproj(x2, lw["wz"], lw["wab"], z_dtype, min(1024, m), 1024)
    z3 = z2.reshape(bsz, t, -1)
    o_a = attn_fn(z3)
    ab3 = ab.reshape(bsz, t, LANES)
    abt4 = ab[:, 0:2 * B_HEADS].reshape(bsz, t // ch, ch, 2 * B_HEADS).transpose(0, 1, 3, 2)
    o_b, ssm = _gdn(z3, ab3, abt4, conv0, ssm0, lw["conv_w"], lw["pcol"], lw["prow"], lw["norm_g"], ch)
    o_c = _memattn(z3, 6, mem_k, mem_v, min(512, t))
    y = _ffn(x2, z2, 7, o_a.reshape(m, d), o_b.reshape(m, d), o_c.reshape(m, d),
             lw["wo"], lw["w1"], lw["w2"], lw["vec"], lw["b1"], lw["alpha"], min(512, m))
    return y.reshape(bsz, t, d), z3, ssm


def kernel(x_prompt, x_sample, cache_a_k, cache_a_v, state_b_conv, state_b_ssm, cache_mem_k, cache_mem_v, mem_prompt, w_in, w_b_conv, b_a_log, b_dt_bias, b_norm_g, a_rel_bias, w_mem_kv, w_out, ln1_g, ln1_b, w_ff1, b_ff1, w_ff2, b_ff2, ln2_g, ln2_b):
    depth = w_in.shape[0]
    bp, tp, d = x_prompt.shape
    bs, tn, _ = x_sample.shape
    window = A_LEFT_CHUNKS * CHUNK
    keep_p = min(window, tp)
    lc = cache_a_k.shape[2]
    n_main = w_in.shape[2] - 2 * B_HEADS
    a_width = A_HEADS * A_HEAD_DIM
    b_width3 = 3 * B_HEADS * B_HEAD_DIM
    alpha = (2.0 * depth) ** 0.25
    xp, xs = x_prompt, x_sample
    outs = [[] for _ in range(10)]
    for l in range(depth):
        w_in_b = w_in[l].astype(BF16)
        vec = jnp.zeros((SUBLANES, d), F32)
        vec = vec.at[0].set(ln1_g[l]).at[1].set(ln1_b[l]).at[2].set(b_ff2[l]).at[3].set(ln2_g[l]).at[4].set(ln2_b[l])
        pcol = jnp.zeros((SUBLANES, LANES), F32)
        pcol = pcol.at[0, 0:B_HEADS].set(b_a_log[l]).at[1, 0:B_HEADS].set(b_dt_bias[l])
        lw = dict(
            wz=w_in_b[:, 0:n_main],
            wab=jnp.pad(w_in_b[:, n_main:], ((0, 0), (0, LANES - 2 * B_HEADS))),
            conv_w=w_b_conv[l],
            pcol=pcol,
            prow=jnp.stack([jnp.broadcast_to(jnp.pad(b_a_log[l], (0, B_HEADS))[:, None], (2 * B_HEADS, LANES)),
                            jnp.broadcast_to(jnp.pad(b_dt_bias[l], (0, B_HEADS))[:, None], (2 * B_HEADS, LANES))]),
            norm_g=b_norm_g[l].reshape(1, B_HEAD_DIM),
            wo=w_out[l].astype(BF16), w1=w_ff1[l].astype(BF16), w2=w_ff2[l].astype(BF16),
            vec=vec, b1=b_ff1[l].reshape(1, -1), alpha=alpha,
        )
        bias_tab = _bias_table(a_rel_bias[l], CHUNK, window + CHUNK)

        bias_p = bias_tab.reshape(A_HEADS // A_GROUP, A_GROUP * CHUNK, window + CHUNK)
        w_mem_b = w_mem_kv[l].astype(BF16)
        mkv = _rows_matmul(mem_prompt, w_mem_b, mem_prompt.shape[1], 0, C_HEADS * C_HEAD_DIM, 0, 2)
        conv0 = jnp.zeros((bp, B_CONV - 1, b_width3), F32)
        ssm0 = jnp.zeros((bp, B_HEADS, B_HEAD_DIM, B_HEAD_DIM), F32)
        x_in = xp
        xp, _, ssm_p = _group_layer(
            x_in, BF16, lambda z3: _attn_prompt(z3, bias_p, CHUNK, window), conv0, ssm0, mkv[0], mkv[1], lw)
        kv_tail = _rows_matmul(x_in, w_in_b, keep_p, tp // keep_p - 1, a_width, 1, 2)
        conv_tail = _rows_matmul(x_in, w_in_b, SUBLANES, tp // SUBLANES - 1, b_width3, 1, 1)
        outs[0].append(kv_tail[0].reshape(bp, keep_p, A_HEADS, A_HEAD_DIM))
        outs[1].append(kv_tail[1].reshape(bp, keep_p, A_HEADS, A_HEAD_DIM))
        outs[2].append(conv_tail[0][:, SUBLANES - (B_CONV - 1):])
        outs[3].append(ssm_p)
        outs[4].append(mkv[0].reshape(bp, -1, C_HEADS, C_HEAD_DIM))
        outs[5].append(mkv[1].reshape(bp, -1, C_HEADS, C_HEAD_DIM))

        bias_s = bias_tab[:, 0:tn, 0:lc + tn].reshape(A_HEADS // A_GROUP, A_GROUP * tn, lc + tn)
        ck = cache_a_k[l].reshape(bs, lc, a_width)
        cv = cache_a_v[l].reshape(bs, lc, a_width)
        kv_all = {}

        def attn_sample(z3):
            kv_all["k"] = jnp.concatenate([ck, z3[:, :, a_width:2 * a_width]], axis=1)
            kv_all["v"] = jnp.concatenate([cv, z3[:, :, 2 * a_width:3 * a_width]], axis=1)
            return _attn_sample(z3, kv_all["k"], kv_all["v"], bias_s)

        xs, zs3, ssm_s = _group_layer(
            xs, F32, attn_sample, state_b_conv[l], state_b_ssm[l],
            cache_mem_k[l].reshape(bs, -1, C_HEADS * C_HEAD_DIM),
            cache_mem_v[l].reshape(bs, -1, C_HEADS * C_HEAD_DIM), lw)
        outs[6].append(kv_all["k"][:, tn:].reshape(bs, lc, A_HEADS, A_HEAD_DIM))
        outs[7].append(kv_all["v"][:, tn:].reshape(bs, lc, A_HEADS, A_HEAD_DIM))
        qkvb_s = zs3[:, :, 3 * a_width:3 * a_width + b_width3]
        outs[8].append(jnp.concatenate([state_b_conv[l], qkvb_s], axis=1)[:, tn:])
        outs[9].append(ssm_s)
    return (xp, xs) + tuple(jnp.stack(o) for o in outs)
```

```python
import functools

import jax
import jax.numpy as jnp
from jax import lax
from jax.experimental import pallas as pl
from jax.experimental.pallas import tpu as pltpu

F32 = jnp.float32
BF16 = jnp.bfloat16

CHUNK = 64
A_HEADS = 16
A_HEAD_DIM = 64
A_GROUP = 4
A_LEFT_CHUNKS = 8
A_REL_CLIP = 128
B_HEADS = 8
B_HEAD_DIM = 128
B_CONV = 4
C_HEADS = 4
C_HEAD_DIM = 256
N_BRANCH = 3
LN_EPS = 1e-5
RMS_EPS = 1e-6
L2_EPS = 1e-6
NEG_INF = -1e30

LANES = 128
SUBLANES = 8
VMEM_LIMIT_BYTES = 56 * 1024 * 1024


def _params(*sem):
    return pltpu.CompilerParams(dimension_semantics=sem, vmem_limit_bytes=VMEM_LIMIT_BYTES)


def _dot(a, b):
    return jnp.dot(a, b, preferred_element_type=F32)


def _dot_nt(a, b):
    return lax.dot_general(a, b, (((1,), (1,)), ((), ())), preferred_element_type=F32)


def _dot_tn(a, b):
    return lax.dot_general(a, b, (((0,), (0,)), ((), ())), preferred_element_type=F32)


def _sigmoid(x):
    return 1.0 / (1.0 + jnp.exp(-x))


def _inproj_kernel(x_ref, w_ref, wab_ref, z_ref, ab_ref, xb_ref):
    @pl.when(pl.program_id(1) == 0)
    def _():
        xb = x_ref[...].astype(BF16)
        xb_ref[...] = xb
        ab_ref[...] = _dot(xb, wab_ref[...])

    z_ref[...] = _dot(xb_ref[...], w_ref[...]).astype(z_ref.dtype)


def _inproj(x2, wz, wab, out_dtype, bm, bn):
    m, d = x2.shape
    n = wz.shape[1]
    return pl.pallas_call(
        _inproj_kernel,
        grid=(m // bm, n // bn),
        in_specs=[
            pl.BlockSpec((bm, d), lambda i, j: (i, 0)),
            pl.BlockSpec((d, bn), lambda i, j: (0, j)),
            pl.BlockSpec((d, LANES), lambda i, j: (0, 0)),
        ],
        out_specs=[
            pl.BlockSpec((bm, bn), lambda i, j: (i, j)),
            pl.BlockSpec((bm, LANES), lambda i, j: (i, 0)),
        ],
        out_shape=[
            jax.ShapeDtypeStruct((m, n), out_dtype),
            jax.ShapeDtypeStruct((m, LANES), F32),
        ],
        scratch_shapes=[pltpu.VMEM((bm, d), BF16)],
        compiler_params=_params("arbitrary", "arbitrary"),
        name="inproj",
    )(x2, wz, wab)


def _rows_matmul_kernel(x_ref, w_ref, o_ref):
    o_ref[...] = _dot(x_ref[...].astype(BF16), w_ref[...])


def _rows_matmul(x3, w, t_blk, t_idx, n_blk, n_idx0, n_cnt):
    bsz, _, d = x3.shape
    return pl.pallas_call(
        _rows_matmul_kernel,
        grid=(n_cnt, bsz),
        in_specs=[
            pl.BlockSpec((None, t_blk, d), lambda j, b: (b, t_idx, 0)),
            pl.BlockSpec((d, n_blk), lambda j, b: (0, n_idx0 + j)),
        ],
        out_specs=pl.BlockSpec((None, None, t_blk, n_blk), lambda j, b: (j, b, 0, 0)),
        out_shape=jax.ShapeDtypeStruct((n_cnt, bsz, t_blk, n_blk), F32),
        compiler_params=_params("arbitrary", "arbitrary"),
        name="rows_matmul",
    )(x3, w)


def _bias_table_kernel(rb_ref, o_ref, *, window):
    h = pl.program_id(0)
    tq, band = o_ref.shape
    i = lax.broadcasted_iota(jnp.int32, (tq, band), 0)
    j = lax.broadcasted_iota(jnp.int32, (tq, band), 1)
    idx = jnp.clip(i - j + window, -A_REL_CLIP, A_REL_CLIP) + A_REL_CLIP
    lowest = max(-A_REL_CLIP, window - (band - 1)) + A_REL_CLIP

    def body(r, acc):
        return jnp.where(idx == r, rb_ref[h, r], acc)

    o_ref[...] = lax.fori_loop(lowest, 2 * A_REL_CLIP + 1, body, jnp.zeros((tq, band), F32))


def _bias_table(rel_bias, tq, band):
    heads = rel_bias.shape[0]
    return pl.pallas_call(
        functools.partial(_bias_table_kernel, window=band - tq),
        grid=(heads,),
        in_specs=[pl.BlockSpec(memory_space=pltpu.SMEM)],
        out_specs=pl.BlockSpec((None, tq, band), lambda h: (h, 0, 0)),
        out_shape=jax.ShapeDtypeStruct((heads, tq, band), F32),
        compiler_params=_params("arbitrary"),
        name="bias_table",
    )(rel_bias)


def _attn_groups(q_ref, k_get, v_get, bias_ref, o_ref, valid_from):
    tq = q_ref.shape[0]
    gw = A_GROUP * A_HEAD_DIM
    lane_head = jnp.right_shift(lax.broadcasted_iota(jnp.int32, (tq, gw), 1), A_HEAD_DIM.bit_length() - 1)
    for g in range(A_HEADS // A_GROUP):
        qg = q_ref[:, g * gw:(g + 1) * gw].astype(BF16)
        zero = jnp.zeros_like(qg)
        qm = jnp.concatenate([jnp.where(lane_head == h, qg, zero) for h in range(A_GROUP)], axis=0)
        s = _dot_nt(qm, k_get(g)) * (A_HEAD_DIM ** -0.5) + bias_ref[g]
        if valid_from is not None:
            col = lax.broadcasted_iota(jnp.int32, s.shape, 1)
            s = jnp.where(col >= valid_from, s, NEG_INF)
        m = jnp.max(s, axis=-1, keepdims=True)
        p = jnp.exp(s - m)
        l = jnp.sum(p, axis=-1, keepdims=True)
        o = _dot(p.astype(BF16), v_get(g)) * (1.0 / l)
        out = jnp.zeros((tq, gw), F32)
        for h in range(A_GROUP):
            out = jnp.where(lane_head == h, o[h * tq:(h + 1) * tq], out)
        o_ref[:, g * gw:(g + 1) * gw] = out.astype(o_ref.dtype)


def _attn_prompt_kernel(q_ref, k_ref, v_ref, bias_ref, o_ref, kpad_ref, vpad_ref, *, window):
    c = pl.program_id(1)
    tq = q_ref.shape[0]
    band = window + tq
    gw = A_GROUP * A_HEAD_DIM

    @pl.when(c == 0)
    def _():
        zeros = jnp.zeros((window, kpad_ref.shape[1]), BF16)
        kpad_ref[0:window, :] = zeros
        vpad_ref[0:window, :] = zeros
        kpad_ref[window:, :] = k_ref[...]
        vpad_ref[window:, :] = v_ref[...]

    start = pl.multiple_of(c * tq, tq)
    _attn_groups(
        q_ref,
        lambda g: kpad_ref[pl.ds(start, band), g * gw:(g + 1) * gw],
        lambda g: vpad_ref[pl.ds(start, band), g * gw:(g + 1) * gw],
        bias_ref, o_ref, window - c * tq)


def _attn_prompt(z3, bias_g, tq, window):
    bsz, t, _ = z3.shape
    width = A_HEADS * A_HEAD_DIM
    return pl.pallas_call(
        functools.partial(_attn_prompt_kernel, window=window),
        grid=(bsz, t // tq),
        in_specs=[
            pl.BlockSpec((None, tq, width), lambda b, c: (b, c, 0)),
            pl.BlockSpec((None, t, width), lambda b, c: (b, 0, 1)),
            pl.BlockSpec((None, t, width), lambda b, c: (b, 0, 2)),
            pl.BlockSpec(bias_g.shape, lambda b, c: (0, 0, 0)),
        ],
        out_specs=pl.BlockSpec((None, tq, width), lambda b, c: (b, c, 0)),
        out_shape=jax.ShapeDtypeStruct((bsz, t, width), BF16),
        scratch_shapes=[pltpu.VMEM((t + window, width), BF16), pltpu.VMEM((t + window, width), BF16)],
        compiler_params=_params("arbitrary", "arbitrary"),
        name="attn_prompt",
    )(z3, z3, z3, bias_g)


def _attn_sample_kernel(q_ref, k_ref, v_ref, bias_ref, o_ref):
    gw = A_GROUP * A_HEAD_DIM
    _attn_groups(
        q_ref,
        lambda g: k_ref[:, g * gw:(g + 1) * gw].astype(BF16),
        lambda g: v_ref[:, g * gw:(g + 1) * gw].astype(BF16),
        bias_ref, o_ref, None)


def _attn_sample(z3, k_all, v_all, bias_g):
    bsz, tq, _ = z3.shape
    band = k_all.shape[1]
    width = A_HEADS * A_HEAD_DIM
    return pl.pallas_call(
        _attn_sample_kernel,
        grid=(bsz,),
        in_specs=[
            pl.BlockSpec((None, tq, width), lambda b: (b, 0, 0)),
            pl.BlockSpec((None, band, width), lambda b: (b, 0, 0)),
            pl.BlockSpec((None, band, width), lambda b: (b, 0, 0)),
            pl.BlockSpec(bias_g.shape, lambda b: (0, 0, 0)),
        ],
        out_specs=pl.BlockSpec((None, tq, width), lambda b: (b, 0, 0)),
        out_shape=jax.ShapeDtypeStruct((bsz, tq, width), BF16),
        compiler_params=_params("arbitrary"),
        name="attn_sample",
    )(z3, k_all, v_all, bias_g)


def _split3(x):
    x1 = x.astype(BF16)
    r1 = x - x1.astype(F32)
    x2 = r1.astype(BF16)
    x3 = (r1 - x2.astype(F32)).astype(BF16)
    return x1, x2, x3


def _gdn_kernel(x_ref, ab_ref, abt_ref, conv0_ref, s0_ref, cw_ref, pcol_ref, prow_ref, ng_ref,
                o_ref, s_ref, xbuf_ref):
    c = pl.program_id(1)
    nb, ch, _ = x_ref.shape
    hd = B_HEAD_DIM
    width = B_HEADS * hd
    keep = B_CONV - 1
    top = SUBLANES
    chains = [(bi, h) for bi in range(nb) for h in range(B_HEADS)]
    every = range(len(chains))

    @pl.when(c == 0)
    def _():
        xbuf_ref[:, top - keep:top, :] = conv0_ref[...].astype(F32)
        s_ref[...] = s0_ref[...].astype(F32)

    xbuf_ref[:, top:top + ch, :] = x_ref[...].astype(F32)

    t_i = lax.broadcasted_iota(jnp.int32, (ch, ch), 0)
    s_i = lax.broadcasted_iota(jnp.int32, (ch, ch), 1)
    tri_incl = (t_i >= s_i)
    tri_strict = (t_i > s_i)
    lower01 = jnp.where(tri_incl, 1.0, 0.0).astype(BF16)
    upper01 = jnp.where(t_i <= s_i, 1.0, 0.0).astype(BF16)
    eye = jnp.where(t_i == s_i, 1.0, 0.0)
    lane = lax.broadcasted_iota(jnp.int32, (ch, LANES), 1)
    lane_row = lax.broadcasted_iota(jnp.int32, (1, ch), 1)
    t_half = lax.broadcasted_iota(jnp.int32, (ch, 2 * ch), 1) >= ch

    gcum_col, beta_col, gcum_row = [], [], []
    for bi in range(nb):
        ab = ab_ref[bi]
        g_col = -jnp.exp(pcol_ref[0:1, :]) * jax.nn.softplus(ab + pcol_ref[1:2, :])
        abt = abt_ref[bi]
        g_row = -jnp.exp(prow_ref[0, :, 0:ch]) * jax.nn.softplus(abt + prow_ref[1, :, 0:ch])
        beta_col.append(_sigmoid(ab))
        gcum_col.append(sum(_dot(lower01, p) for p in _split3(g_col)))
        gcum_row.append(sum(_dot(p, upper01) for p in _split3(g_row)))

    def pick(x, idx):
        return jnp.sum(jnp.where(lane == idx, x, 0.0), axis=-1, keepdims=True)

    def conv_silu(bi, col0):
        acc = xbuf_ref[bi, top:top + ch, col0:col0 + hd] * cw_ref[keep:keep + 1, col0:col0 + hd]
        for i in range(keep):
            acc = acc + (xbuf_ref[bi, top - keep + i:top - keep + i + ch, col0:col0 + hd]
                         * cw_ref[i:i + 1, col0:col0 + hd])
        return acc * _sigmoid(acc)

    def l2norm(x):
        return x * lax.rsqrt(jnp.sum(x * x, axis=-1, keepdims=True) + L2_EPS)

    q = [l2norm(conv_silu(bi, h * hd)) * (hd ** -0.5) for bi, h in chains]
    k = [l2norm(conv_silu(bi, width + h * hd)) for bi, h in chains]
    v = [conv_silu(bi, 2 * width + h * hd) for bi, h in chains]
    gc = [pick(gcum_col[bi], h) for bi, h in chains]
    bc = [pick(beta_col[bi], B_HEADS + h) for bi, h in chains]
    gr = [gcum_row[bi][h:h + 1, :] for bi, h in chains]
    glast = [jnp.sum(jnp.where(lane_row == ch - 1, gr[i], 0.0), axis=-1, keepdims=True) for i in every]
    dec_incl = [jnp.exp(jnp.where(tri_incl, gc[i] - gr[i], NEG_INF)) for i in every]
    eg = [jnp.exp(gc[i]) for i in every]

    qk_kk = [_dot_nt(jnp.concatenate([q[i], k[i]], axis=0).astype(BF16), k[i].astype(BF16)) for i in every]
    a_qk = [qk_kk[i][0:ch] * dec_incl[i] for i in every]

    tp = [jnp.concatenate([-(bc[i] * qk_kk[i][ch:2 * ch]) * jnp.where(tri_strict, dec_incl[i], 0.0), eye], axis=1)
          for i in every]
    zpad = jnp.zeros((ch, 2 * ch), BF16)
    for _ in range(max(1, (ch - 1).bit_length())):
        tp_b = [tp[i].astype(BF16) for i in every]
        tp = [_dot(tp_b[i], jnp.concatenate([tp_b[i], zpad], axis=0)) + jnp.where(t_half, tp[i], 0.0)
              for i in every]
    rhs = [jnp.concatenate([bc[i] * v[i], (bc[i] * eg[i]) * k[i]], axis=1).astype(BF16) for i in every]
    zrhs = jnp.zeros((ch, 2 * hd), BF16)
    x_sol = [_dot(tp[i].astype(BF16), jnp.concatenate([zrhs, rhs[i]], axis=0)) for i in every]

    state = [s_ref[bi, h] for bi, h in chains]
    wq = [_dot(jnp.concatenate([x_sol[i][:, hd:2 * hd], q[i] * eg[i]], axis=0).astype(BF16),
               state[i].astype(BF16)) for i in every]
    wv_b = [(x_sol[i][:, 0:hd] - wq[i][0:ch]).astype(BF16) for i in every]
    o = [wq[i][ch:2 * ch] + _dot(a_qk[i].astype(BF16), wv_b[i]) for i in every]
    s_new = [jnp.exp(glast[i]) * state[i]
             + _dot_tn((k[i] * jnp.exp(glast[i] - gc[i])).astype(BF16), wv_b[i]) for i in every]
    for i, (bi, h) in enumerate(chains):
        s_ref[bi, h] = s_new[i]
        on = o[i] * lax.rsqrt(jnp.mean(o[i] * o[i], axis=-1, keepdims=True) + RMS_EPS) * ng_ref[...]
        o_ref[bi, :, h * hd:(h + 1) * hd] = on.astype(o_ref.dtype)

    xbuf_ref[:, top - keep:top, :] = xbuf_ref[:, top + ch - keep:top + ch, :]


GDN_BATCHES = 2


def _gdn(z3, ab3, abt4, conv0, s0, conv_w, pcol, prow, norm_g, ch):
    bsz, t, _ = z3.shape
    nb = GDN_BATCHES
    width3 = 3 * B_HEADS * B_HEAD_DIM
    width = B_HEADS * B_HEAD_DIM
    return pl.pallas_call(
        _gdn_kernel,
        grid=(bsz // nb, t // ch),
        in_specs=[
            pl.BlockSpec((nb, ch, width3), lambda b, c: (b, c, 1)),
            pl.BlockSpec((nb, ch, LANES), lambda b, c: (b, c, 0)),
            pl.BlockSpec((nb, None, 2 * B_HEADS, ch), lambda b, c: (b, c, 0, 0)),
            pl.BlockSpec((nb, B_CONV - 1, width3), lambda b, c: (b, 0, 0)),
            pl.BlockSpec((nb, B_HEADS, B_HEAD_DIM, B_HEAD_DIM), lambda b, c: (b, 0, 0, 0)),
            pl.BlockSpec(conv_w.shape, lambda b, c: (0, 0)),
            pl.BlockSpec(pcol.shape, lambda b, c: (0, 0)),
            pl.BlockSpec(prow.shape, lambda b, c: (0, 0, 0)),
            pl.BlockSpec(norm_g.shape, lambda b, c: (0, 0)),
        ],
        out_specs=[
            pl.BlockSpec((nb, ch, width), lambda b, c: (b, c, 0)),
            pl.BlockSpec((nb, B_HEADS, B_HEAD_DIM, B_HEAD_DIM), lambda b, c: (b, 0, 0, 0)),
        ],
        out_shape=[
            jax.ShapeDtypeStruct((bsz, t, width), BF16),
            jax.ShapeDtypeStruct((bsz, B_HEADS, B_HEAD_DIM, B_HEAD_DIM), F32),
        ],
        scratch_shapes=[pltpu.VMEM((nb, SUBLANES + ch, width3), F32)],
        compiler_params=_params("arbitrary", "arbitrary"),
        name="gdn",
    )(z3, ab3, abt4, conv0, s0, conv_w, pcol, prow, norm_g)


def _memattn_kernel(q_ref, mk_ref, mv_ref, o_ref, kb_ref, vb_ref):
    @pl.when(pl.program_id(1) == 0)
    def _():
        kb_ref[...] = mk_ref[...].astype(BF16)
        vb_ref[...] = mv_ref[...].astype(BF16)

    dh = C_HEAD_DIM
    for h in range(C_HEADS):
        q = q_ref[:, h * dh:(h + 1) * dh].astype(BF16)
        s = _dot_nt(q, kb_ref[:, h * dh:(h + 1) * dh]) * (dh ** -0.5)
        m = jnp.max(s, axis=-1, keepdims=True)
        p = jnp.exp(s - m)
        l = jnp.sum(p, axis=-1, keepdims=True)
        o = _dot(p.astype(BF16), vb_ref[:, h * dh:(h + 1) * dh]) * (1.0 / l)
        o_ref[:, h * dh:(h + 1) * dh] = o.astype(o_ref.dtype)


def _memattn(z3, q_blk, mk, mv, tq):
    bsz, t, _ = z3.shape
    mem = mk.shape[1]
    width = C_HEADS * C_HEAD_DIM
    return pl.pallas_call(
        _memattn_kernel,
        grid=(bsz, t // tq),
        in_specs=[
            pl.BlockSpec((None, tq, width), lambda b, i: (b, i, q_blk)),
            pl.BlockSpec((None, mem, width), lambda b, i: (b, 0, 0)),
            pl.BlockSpec((None, mem, width), lambda b, i: (b, 0, 0)),
        ],
        out_specs=pl.BlockSpec((None, tq, width), lambda b, i: (b, i, 0)),
        out_shape=jax.ShapeDtypeStruct((bsz, t, width), BF16),
        scratch_shapes=[pltpu.VMEM((mem, width), BF16), pltpu.VMEM((mem, width), BF16)],
        compiler_params=_params("arbitrary", "arbitrary"),
        name="memattn",
    )(z3, mk, mv)


def _layer_norm(x, g, b):
    mu = jnp.mean(x, axis=-1, keepdims=True)
    xc = x - mu
    var = jnp.mean(xc * xc, axis=-1, keepdims=True)
    return xc * lax.rsqrt(var + LN_EPS) * g + b


def _ffn_kernel(x_ref, ga_ref, gb_ref, gc_ref, oa_ref, ob_ref, oc_ref, wo_ref, w1_ref, w2_ref,
                vec_ref, b1_ref, y_ref, *, alpha, ff_blk):
    merged = (_sigmoid(ga_ref[...].astype(F32)) * oa_ref[...].astype(F32)
              + _sigmoid(gb_ref[...].astype(F32)) * ob_ref[...].astype(F32)
              + _sigmoid(gc_ref[...].astype(F32)) * oc_ref[...].astype(F32))
    proj = _dot(merged.astype(BF16), wo_ref[...])
    h = _layer_norm(alpha * x_ref[...] + proj, vec_ref[0:1, :], vec_ref[1:2, :])
    hb = h.astype(BF16)
    acc = jnp.zeros(h.shape, F32)
    for k0 in range(0, w1_ref.shape[1], ff_blk):
        f = _dot(hb, w1_ref[:, k0:k0 + ff_blk]) + b1_ref[:, k0:k0 + ff_blk]
        f = jnp.square(jnp.maximum(f, 0.0))
        acc = acc + _dot(f.astype(BF16), w2_ref[k0:k0 + ff_blk, :])
    y_ref[...] = _layer_norm(alpha * h + acc + vec_ref[2:3, :], vec_ref[3:4, :], vec_ref[4:5, :])


def _ffn(x2, z2, gate_blk0, oa, ob, oc, wo, w1, w2, vec, b1, alpha, bm):
    m, d = x2.shape
    dff = w1.shape[1]
    const = dict(pipeline_mode=pl.Buffered(1))
    row = lambda i: (i, 0)
    return pl.pallas_call(
        functools.partial(_ffn_kernel, alpha=alpha, ff_blk=1024),
        grid=(m // bm,),
        in_specs=[
            pl.BlockSpec((bm, d), row),
            pl.BlockSpec((bm, d), lambda i: (i, gate_blk0)),
            pl.BlockSpec((bm, d), lambda i: (i, gate_blk0 + 1)),
            pl.BlockSpec((bm, d), lambda i: (i, gate_blk0 + 2)),
            pl.BlockSpec((bm, d), row),
            pl.BlockSpec((bm, d), row),
            pl.BlockSpec((bm, d), row),
            pl.BlockSpec((d, d), lambda i: (0, 0), **const),
            pl.BlockSpec((d, dff), lambda i: (0, 0), **const),
            pl.BlockSpec((dff, d), lambda i: (0, 0), **const),
            pl.BlockSpec(vec.shape, lambda i: (0, 0), **const),
            pl.BlockSpec(b1.shape, lambda i: (0, 0), **const),
        ],
        out_specs=pl.BlockSpec((bm, d), row),
        out_shape=jax.ShapeDtypeStruct((m, d), F32),
        compiler_params=_params("arbitrary"),
        name="merge_ffn",
    )(x2, z2, z2, z2, oa, ob, oc, wo, w1, w2, vec, b1)


def _group_layer(x3, z_dtype, attn_fn, conv0, ssm0, mem_k, mem_v, lw):
    bsz, t, d = x3.shape
    m = bsz * t
    ch = min(CHUNK, t)
    x2 = x3.reshape(m, d)
    z2, ab = _inproj(x2, lw["wz"], lw["wab"], z_dtype, min(1024, m), 1024)
    z3 = z2.reshape(bsz, t, -1)
    o_a = attn_fn(z3)
    ab3 = ab.reshape(bsz, t, LANES)
    abt4 = ab[:, 0:2 * B_HEADS].reshape(bsz, t // ch, ch, 2 * B_HEADS).transpose(0, 1, 3, 2)
    o_b, ssm = _gdn(z3, ab3, abt4, conv0, ssm0, lw["conv_w"], lw["pcol"], lw["prow"], lw["norm_g"], ch)
    o_c = _memattn(z3, 6, mem_k, mem_v, min(512, t))
    y = _ffn(x2, z2, 7, o_a.reshape(m, d), o_b.reshape(m, d), o_c.reshape(m, d),
             lw["wo"], lw["w1"], lw["w2"], lw["vec"], lw["b1"], lw["alpha"], min(512, m))
    return y.reshape(bsz, t, d), z3, ssm


def kernel(x_prompt, x_sample, cache_a_k, cache_a_v, state_b_conv, state_b_ssm, cache_mem_k, cache_mem_v, mem_prompt, w_in, w_b_conv, b_a_log, b_dt_bias, b_norm_g, a_rel_bias, w_mem_kv, w_out, ln1_g, ln1_b, w_ff1, b_ff1, w_ff2, b_ff2, ln2_g, ln2_b):
    depth = w_in.shape[0]
    bp, tp, d = x_prompt.shape
    bs, tn, _ = x_sample.shape
    window = A_LEFT_CHUNKS * CHUNK
    keep_p = min(window, tp)
    lc = cache_a_k.shape[2]
    n_main = w_in.shape[2] - 2 * B_HEADS
    a_width = A_HEADS * A_HEAD_DIM
    b_width3 = 3 * B_HEADS * B_HEAD_DIM
    alpha = (2.0 * depth) ** 0.25
    xp, xs = x_prompt, x_sample
    outs = [[] for _ in range(10)]
    for l in range(depth):
        w_in_b = w_in[l].astype(BF16)
        vec = jnp.zeros((SUBLANES, d), F32)
        vec = vec.at[0].set(ln1_g[l]).at[1].set(ln1_b[l]).at[2].set(b_ff2[l]).at[3].set(ln2_g[l]).at[4].set(ln2_b[l])
        pcol = jnp.zeros((SUBLANES, LANES), F32)
        pcol = pcol.at[0, 0:B_HEADS].set(b_a_log[l]).at[1, 0:B_HEADS].set(b_dt_bias[l])
        lw = dict(
            wz=w_in_b[:, 0:n_main],
            wab=jnp.pad(w_in_b[:, n_main:], ((0, 0), (0, LANES - 2 * B_HEADS))),
            conv_w=w_b_conv[l],
            pcol=pcol,
            prow=jnp.stack([jnp.broadcast_to(jnp.pad(b_a_log[l], (0, B_HEADS))[:, None], (2 * B_HEADS, LANES)),
                            jnp.broadcast_to(jnp.pad(b_dt_bias[l], (0, B_HEADS))[:, None], (2 * B_HEADS, LANES))]),
            norm_g=b_norm_g[l].reshape(1, B_HEAD_DIM),
            wo=w_out[l].astype(BF16), w1=w_ff1[l].astype(BF16), w2=w_ff2[l].astype(BF16),
            vec=vec, b1=b_ff1[l].reshape(1, -1), alpha=alpha,
        )
        bias_tab = _bias_table(a_rel_bias[l], CHUNK, window + CHUNK)

        bias_p = bias_tab.reshape(A_HEADS // A_GROUP, A_GROUP * CHUNK, window + CHUNK)
        w_mem_b = w_mem_kv[l].astype(BF16)
        mkv = _rows_matmul(mem_prompt, w_mem_b, mem_prompt.shape[1], 0, C_HEADS * C_HEAD_DIM, 0, 2)
        conv0 = jnp.zeros((bp, B_CONV - 1, b_width3), F32)
        ssm0 = jnp.zeros((bp, B_HEADS, B_HEAD_DIM, B_HEAD_DIM), F32)
        x_in = xp
        xp, _, ssm_p = _group_layer(
            x_in, BF16, lambda z3: _attn_prompt(z3, bias_p, CHUNK, window), conv0, ssm0, mkv[0], mkv[1], lw)
        kv_tail = _rows_matmul(x_in, w_in_b, keep_p, tp // keep_p - 1, a_width, 1, 2)
        conv_tail = _rows_matmul(x_in, w_in_b, SUBLANES, tp // SUBLANES - 1, b_width3, 1, 1)
        outs[0].append(kv_tail[0].reshape(bp, keep_p, A_HEADS, A_HEAD_DIM))
        outs[1].append(kv_tail[1].reshape(bp, keep_p, A_HEADS, A_HEAD_DIM))
        outs[2].append(conv_tail[0][:, SUBLANES - (B_CONV - 1):])
        outs[3].append(ssm_p)
        outs[4].append(mkv[0].reshape(bp, -1, C_HEADS, C_HEAD_DIM))
        outs[5].append(mkv[1].reshape(bp, -1, C_HEADS, C_HEAD_DIM))

        bias_s = bias_tab[:, 0:tn, 0:lc + tn].reshape(A_HEADS // A_GROUP, A_GROUP * tn, lc + tn)
        ck = cache_a_k[l].reshape(bs, lc, a_width)
        cv = cache_a_v[l].reshape(bs, lc, a_width)
        kv_all = {}

        def attn_sample(z3):
            kv_all["k"] = jnp.concatenate([ck, z3[:, :, a_width:2 * a_width]], axis=1)
            kv_all["v"] = jnp.concatenate([cv, z3[:, :, 2 * a_width:3 * a_width]], axis=1)
            return _attn_sample(z3, kv_all["k"], kv_all["v"], bias_s)

        xs, zs3, ssm_s = _group_layer(
            xs, F32, attn_sample, state_b_conv[l], state_b_ssm[l],
            cache_mem_k[l].reshape(bs, -1, C_HEADS * C_HEAD_DIM),
            cache_mem_v[l].reshape(bs, -1, C_HEADS * C_HEAD_DIM), lw)
        outs[6].append(kv_all["k"][:, tn:].reshape(bs, lc, A_HEADS, A_HEAD_DIM))
        outs[7].append(kv_all["v"][:, tn:].reshape(bs, lc, A_HEADS, A_HEAD_DIM))
        qkvb_s = zs3[:, :, 3 * a_width:3 * a_width + b_width3]
        outs[8].append(jnp.concatenate([state_b_conv[l], qkvb_s], axis=1)[:, tn:])
        outs[9].append(ssm_s)
    return (xp, xs) + tuple(jnp.stack(o) for o in outs)
```

```python
import functools

import jax
import jax.numpy as jnp
from jax import lax
from jax.experimental import pallas as pl
from jax.experimental.pallas import tpu as pltpu

F32 = jnp.float32
BF16 = jnp.bfloat16

CHUNK = 64
A_HEADS = 16
A_HEAD_DIM = 64
A_GROUP = 4
A_LEFT_CHUNKS = 8
A_REL_CLIP = 128
B_HEADS = 8
B_HEAD_DIM = 128
B_CONV = 4
C_HEADS = 4
C_HEAD_DIM = 256
N_BRANCH = 3
LN_EPS = 1e-5
RMS_EPS = 1e-6
L2_EPS = 1e-6
NEG_INF = -1e30

LANES = 128
SUBLANES = 8
VMEM_LIMIT_BYTES = 56 * 1024 * 1024


def _params(*sem):
    return pltpu.CompilerParams(dimension_semantics=sem, vmem_limit_bytes=VMEM_LIMIT_BYTES)


def _dot(a, b):
    return jnp.dot(a, b, preferred_element_type=F32)


def _dot_nt(a, b):
    return lax.dot_general(a, b, (((1,), (1,)), ((), ())), preferred_element_type=F32)


def _dot_tn(a, b):
    return lax.dot_general(a, b, (((0,), (0,)), ((), ())), preferred_element_type=F32)


def _sigmoid(x):
    return 0.5 + 0.5 * jnp.tanh(0.5 * x)


def _silu(x):
    hx = 0.5 * x
    return hx + hx * jnp.tanh(hx)


K_BLOCK, V_BLOCK, CONV_BLOCK0, CONV_BLOCKS = 1, 2, 3, 3


def _inproj_kernel(x_ref, w_ref, wab_ref, z_ref, ab_ref, *rest, tails, blocks_per_seq):
    j = pl.program_id(1)
    xb_ref = rest[-1]

    @pl.when(j == 0)
    def _():
        xb = x_ref[...].astype(BF16)
        xb_ref[...] = xb
        ab_ref[...] = _dot(xb, wab_ref[...])

    acc = _dot(xb_ref[...], w_ref[...])
    z_ref[...] = acc.astype(z_ref.dtype)

    if tails:
        kt_ref, vt_ref, ct_ref = rest[0:3]
        bm = acc.shape[0]
        last = pl.program_id(0) % blocks_per_seq == blocks_per_seq - 1

        @pl.when(last & (j == K_BLOCK))
        def _():
            kt_ref[...] = acc[bm - kt_ref.shape[0]:, :]

        @pl.when(last & (j == V_BLOCK))
        def _():
            vt_ref[...] = acc[bm - vt_ref.shape[0]:, :]

        @pl.when(last & (j >= CONV_BLOCK0) & (j < CONV_BLOCK0 + CONV_BLOCKS))
        def _():
            ct_ref[...] = acc[bm - ct_ref.shape[0]:, :]


def _inproj(x2, w, wab, n_main, out_dtype, bm, bn, seq_len=None, keep=None):
    m, d = x2.shape
    tails = seq_len is not None
    out_specs = [
        pl.BlockSpec((bm, bn), lambda i, j: (i, j)),
        pl.BlockSpec((bm, LANES), lambda i, j: (i, 0)),
    ]
    out_shape = [
        jax.ShapeDtypeStruct((m, n_main), out_dtype),
        jax.ShapeDtypeStruct((m, LANES), F32),
    ]
    if tails:
        assert bn == d and seq_len % bm == 0 and keep <= bm
        bps = seq_len // bm
        nseq = m // seq_len
        conv_blk = lambda i, j: (i // bps, 0, jnp.where(i % bps == bps - 1,
                                                         jnp.clip(j - CONV_BLOCK0, 0, CONV_BLOCKS - 1), 0))
        out_specs += [
            pl.BlockSpec((None, keep, bn), lambda i, j: (i // bps, 0, 0)),
            pl.BlockSpec((None, keep, bn), lambda i, j: (i // bps, 0, 0)),
            pl.BlockSpec((None, SUBLANES, bn), conv_blk),
        ]
        out_shape += [
            jax.ShapeDtypeStruct((nseq, keep, bn), F32),
            jax.ShapeDtypeStruct((nseq, keep, bn), F32),
            jax.ShapeDtypeStruct((nseq, SUBLANES, CONV_BLOCKS * bn), F32),
        ]
    return pl.pallas_call(
        functools.partial(_inproj_kernel, tails=tails, blocks_per_seq=seq_len // bm if tails else 1),
        grid=(m // bm, n_main // bn),
        in_specs=[
            pl.BlockSpec((bm, d), lambda i, j: (i, 0)),
            pl.BlockSpec((d, bn), lambda i, j: (0, j)),
            pl.BlockSpec((d, LANES), lambda i, j: (0, 0)),
        ],
        out_specs=out_specs,
        out_shape=out_shape,
        scratch_shapes=[pltpu.VMEM((bm, d), BF16)],
        compiler_params=_params("arbitrary", "arbitrary"),
        name="inproj",
    )(x2, w, wab)


def _mem_kv_kernel(x_ref, w_ref, k_ref, v_ref):
    acc = _dot(x_ref[...].astype(BF16), w_ref[...])
    half = k_ref.shape[1]
    k_ref[...] = acc[:, 0:half]
    v_ref[...] = acc[:, half:2 * half]


def _mem_kv(mem, w):
    bsz, mt, d = mem.shape
    half = w.shape[1] // 2
    out = jax.ShapeDtypeStruct((bsz, mt, half), F32)
    return pl.pallas_call(
        _mem_kv_kernel,
        grid=(bsz,),
        in_specs=[
            pl.BlockSpec((None, mt, d), lambda b: (b, 0, 0)),
            pl.BlockSpec(w.shape, lambda b: (0, 0)),
        ],
        out_specs=[pl.BlockSpec((None, mt, half), lambda b: (b, 0, 0))] * 2,
        out_shape=[out, out],
        compiler_params=_params("arbitrary"),
        name="mem_kv",
    )(mem, w)


LOG2E = 1.4426950408889634


def _bias_col0(window):
    return (window - A_REL_CLIP + 1) // LANES * LANES


def _bias_table_kernel(rb_ref, o_ref, *, window, col0):
    h = pl.program_id(0)
    tq, cols = o_ref.shape
    i = lax.broadcasted_iota(jnp.int32, (tq, cols), 0)
    j = lax.broadcasted_iota(jnp.int32, (tq, cols), 1) + col0
    idx = jnp.clip(i - j + window, -A_REL_CLIP, A_REL_CLIP) + A_REL_CLIP
    lowest = max(-A_REL_CLIP, window - (col0 + cols - 1)) + A_REL_CLIP

    def body(r, acc):
        return jnp.where(idx == r, rb_ref[h, r], acc)

    tab = lax.fori_loop(lowest, 2 * A_REL_CLIP + 1, body, jnp.zeros((tq, cols), F32))
    o_ref[...] = (tab - rb_ref[h, 2 * A_REL_CLIP]) * LOG2E


def _bias_table(rel_bias, tq, band):
    heads = rel_bias.shape[0]
    window = band - tq
    col0 = _bias_col0(window)
    return pl.pallas_call(
        functools.partial(_bias_table_kernel, window=window, col0=col0),
        grid=(heads,),
        in_specs=[pl.BlockSpec(memory_space=pltpu.SMEM)],
        out_specs=pl.BlockSpec((None, tq, band - col0), lambda h: (h, 0, 0)),
        out_shape=jax.ShapeDtypeStruct((heads, tq, band - col0), F32),
        compiler_params=_params("arbitrary"),
        name="bias_table",
    )(rel_bias)


def _attn_groups(q_ref, k_get, v_get, bias_ref, o_ref, valid_from):
    tq = q_ref.shape[0]
    gw = A_GROUP * A_HEAD_DIM
    lane_head = jnp.right_shift(lax.broadcasted_iota(jnp.int32, (tq, gw), 1), A_HEAD_DIM.bit_length() - 1)
    for g in range(A_HEADS // A_GROUP):
        qg = q_ref[:, g * gw:(g + 1) * gw].astype(BF16)
        zero = jnp.zeros_like(qg)
        qm = jnp.concatenate([jnp.where(lane_head == h, qg, zero) for h in range(A_GROUP)], axis=0)
        s = _dot_nt(qm, k_get(g))
        col0 = s.shape[1] - bias_ref.shape[2]
        s = jnp.concatenate([s[:, 0:col0], s[:, col0:] + bias_ref[g]], axis=1)
        if valid_from is not None:
            col = lax.broadcasted_iota(jnp.int32, s.shape, 1)
            s = jnp.where(col >= valid_from, s, NEG_INF)
        m = jnp.max(s, axis=-1, keepdims=True)
        p = jnp.exp2(s - m)
        l = jnp.sum(p, axis=-1, keepdims=True)
        o = _dot(p.astype(BF16), v_get(g)) * (1.0 / l)
        out = jnp.zeros((tq, gw), F32)
        for h in range(A_GROUP):
            out = jnp.where(lane_head == h, o[h * tq:(h + 1) * tq], out)
        o_ref[:, g * gw:(g + 1) * gw] = out.astype(o_ref.dtype)


def _attn_prompt_kernel(q_ref, k_ref, v_ref, bias_ref, o_ref, kpad_ref, vpad_ref, *, window):
    c = pl.program_id(1)
    tq = q_ref.shape[0]
    band = window + tq
    gw = A_GROUP * A_HEAD_DIM

    @pl.when(c == 0)
    def _():
        zeros = jnp.zeros((window, kpad_ref.shape[1]), BF16)
        kpad_ref[0:window, :] = zeros
        vpad_ref[0:window, :] = zeros
        kpad_ref[window:, :] = k_ref[...]
        vpad_ref[window:, :] = v_ref[...]

    start = pl.multiple_of(c * tq, tq)
    k_get = lambda g: kpad_ref[pl.ds(start, band), g * gw:(g + 1) * gw]
    v_get = lambda g: vpad_ref[pl.ds(start, band), g * gw:(g + 1) * gw]

    @pl.when(c * tq < window)
    def _():
        _attn_groups(q_ref, k_get, v_get, bias_ref, o_ref, window - c * tq)

    @pl.when(c * tq >= window)
    def _():
        _attn_groups(q_ref, k_get, v_get, bias_ref, o_ref, None)


def _attn_prompt(z3, bias_g, tq, window):
    bsz, t, _ = z3.shape
    width = A_HEADS * A_HEAD_DIM
    return pl.pallas_call(
        functools.partial(_attn_prompt_kernel, window=window),
        grid=(bsz, t // tq),
        in_specs=[
            pl.BlockSpec((None, tq, width), lambda b, c: (b, c, 0)),
            pl.BlockSpec((None, t, width), lambda b, c: (b, 0, K_BLOCK)),
            pl.BlockSpec((None, t, width), lambda b, c: (b, 0, V_BLOCK)),
            pl.BlockSpec(bias_g.shape, lambda b, c: (0, 0, 0)),
        ],
        out_specs=pl.BlockSpec((None, tq, width), lambda b, c: (b, c, 0)),
        out_shape=jax.ShapeDtypeStruct((bsz, t, width), BF16),
        scratch_shapes=[pltpu.VMEM((t + window, width), BF16), pltpu.VMEM((t + window, width), BF16)],
        compiler_params=_params("arbitrary", "arbitrary"),
        name="attn_prompt",
    )(z3, z3, z3, bias_g)


def _attn_sample_kernel(q_ref, kn_ref, vn_ref, ck_ref, cv_ref, bias_ref, o_ref, kb_ref, vb_ref):
    gw = A_GROUP * A_HEAD_DIM
    lc = ck_ref.shape[0]
    kb_ref[0:lc, :] = ck_ref[...].astype(BF16)
    vb_ref[0:lc, :] = cv_ref[...].astype(BF16)
    kb_ref[lc:, :] = kn_ref[...].astype(BF16)
    vb_ref[lc:, :] = vn_ref[...].astype(BF16)
    _attn_groups(
        q_ref,
        lambda g: kb_ref[:, g * gw:(g + 1) * gw],
        lambda g: vb_ref[:, g * gw:(g + 1) * gw],
        bias_ref, o_ref, None)


def _attn_sample(z3, ck, cv, bias_g):
    bsz, tq, _ = z3.shape
    lc = ck.shape[1]
    width = A_HEADS * A_HEAD_DIM
    return pl.pallas_call(
        _attn_sample_kernel,
        grid=(bsz,),
        in_specs=[
            pl.BlockSpec((None, tq, width), lambda b: (b, 0, 0)),
            pl.BlockSpec((None, tq, width), lambda b: (b, 0, K_BLOCK)),
            pl.BlockSpec((None, tq, width), lambda b: (b, 0, V_BLOCK)),
            pl.BlockSpec((None, lc, width), lambda b: (b, 0, 0)),
            pl.BlockSpec((None, lc, width), lambda b: (b, 0, 0)),
            pl.BlockSpec(bias_g.shape, lambda b: (0, 0, 0)),
        ],
        out_specs=pl.BlockSpec((None, tq, width), lambda b: (b, 0, 0)),
        out_shape=jax.ShapeDtypeStruct((bsz, tq, width), BF16),
        scratch_shapes=[pltpu.VMEM((lc + tq, width), BF16), pltpu.VMEM((lc + tq, width), BF16)],
        compiler_params=_params("arbitrary"),
        name="attn_sample",
    )(z3, z3, z3, ck, cv, bias_g)


def _split3(x):
    x1 = x.astype(BF16)
    r1 = x - x1.astype(F32)
    x2 = r1.astype(BF16)
    x3 = (r1 - x2.astype(F32)).astype(BF16)
    return x1, x2, x3


def _gdn_kernel(x_ref, ab_ref, abt_ref, conv0_ref, s0_ref, cw_ref, pcol_ref, prow_ref, ng_ref,
                o_ref, s_ref, xbuf_ref):
    c = pl.program_id(1)
    nb, ch, _ = x_ref.shape
    hd = B_HEAD_DIM
    width = B_HEADS * hd
    keep = B_CONV - 1
    top = SUBLANES
    chains = [(bi, h) for bi in range(nb) for h in range(B_HEADS)]
    every = range(len(chains))

    @pl.when(c == 0)
    def _():
        xbuf_ref[:, top - keep:top, :] = conv0_ref[...].astype(F32)
        s_ref[...] = s0_ref[...].astype(F32)

    xbuf_ref[:, top:top + ch, :] = x_ref[...].astype(F32)

    t_i = lax.broadcasted_iota(jnp.int32, (ch, ch), 0)
    s_i = lax.broadcasted_iota(jnp.int32, (ch, ch), 1)
    tri_incl = (t_i >= s_i)
    tri_strict = (t_i > s_i)
    lower01 = jnp.where(tri_incl, 1.0, 0.0).astype(BF16)
    upper01 = jnp.where(t_i <= s_i, 1.0, 0.0).astype(BF16)
    eye = jnp.where(t_i == s_i, 1.0, 0.0)
    lane = lax.broadcasted_iota(jnp.int32, (ch, LANES), 1)
    lane_row = lax.broadcasted_iota(jnp.int32, (1, ch), 1)

    gcum_col, beta_col, gcum_row = [], [], []
    for bi in range(nb):
        ab = ab_ref[bi]
        g_col = -jnp.exp(pcol_ref[0:1, :]) * jax.nn.softplus(ab + pcol_ref[1:2, :])
        abt = abt_ref[bi]
        g_row = -jnp.exp(prow_ref[0, :, 0:ch]) * jax.nn.softplus(abt + prow_ref[1, :, 0:ch])
        beta_col.append(_sigmoid(ab))
        gcum_col.append(sum(_dot(lower01, p) for p in _split3(g_col)))
        gcum_row.append(sum(_dot(p, upper01) for p in _split3(g_row)))

    def pick(x, idx):
        return jnp.sum(jnp.where(lane == idx, x, 0.0), axis=-1, keepdims=True)

    def conv_silu(bi, col0):
        acc = xbuf_ref[bi, top:top + ch, col0:col0 + hd] * cw_ref[keep:keep + 1, col0:col0 + hd]
        for i in range(keep):
            acc = acc + (xbuf_ref[bi, top - keep + i:top - keep + i + ch, col0:col0 + hd]
                         * cw_ref[i:i + 1, col0:col0 + hd])
        return _silu(acc)

    def l2norm(x):
        return x * lax.rsqrt(jnp.sum(x * x, axis=-1, keepdims=True) + L2_EPS)

    q = [l2norm(conv_silu(bi, h * hd)) * (hd ** -0.5) for bi, h in chains]
    k = [l2norm(conv_silu(bi, width + h * hd)) for bi, h in chains]
    v = [conv_silu(bi, 2 * width + h * hd) for bi, h in chains]
    gc = [pick(gcum_col[bi], h) for bi, h in chains]
    bc = [pick(beta_col[bi], B_HEADS + h) for bi, h in chains]
    gr = [gcum_row[bi][h:h + 1, :] for bi, h in chains]
    glast = [jnp.sum(jnp.where(lane_row == ch - 1, gr[i], 0.0), axis=-1, keepdims=True) for i in every]
    dec_incl = [jnp.exp(jnp.where(tri_incl, gc[i] - gr[i], NEG_INF)) for i in every]
    eg = [jnp.exp(gc[i]) for i in every]
    nbc = [-bc[i] for i in every]

    qk_kk = [_dot_nt(jnp.concatenate([q[i], k[i]], axis=0).astype(BF16), k[i].astype(BF16)) for i in every]
    a_qk = [qk_kk[i][0:ch] * dec_incl[i] for i in every]

    tp = [jnp.concatenate([(nbc[i] * qk_kk[i][ch:2 * ch]) * jnp.where(tri_strict, dec_incl[i], 0.0), eye], axis=1)
          for i in every]
    keep_t = jnp.concatenate([jnp.zeros((ch, ch), F32), eye], axis=1).astype(BF16)
    for _ in range(max(1, (ch - 1).bit_length())):
        tp_b = [tp[i].astype(BF16) for i in every]
        tp = [_dot(tp_b[i], jnp.concatenate([tp_b[i], keep_t], axis=0)) for i in every]
    rhs = [jnp.concatenate([bc[i] * v[i], (bc[i] * eg[i]) * k[i]], axis=1).astype(BF16) for i in every]
    zrhs = jnp.zeros((ch, 2 * hd), BF16)
    x_sol = [_dot(tp[i].astype(BF16), jnp.concatenate([zrhs, rhs[i]], axis=0)) for i in every]

    state = [s_ref[bi, h] for bi, h in chains]
    wq = [_dot(jnp.concatenate([x_sol[i][:, hd:2 * hd], q[i] * eg[i]], axis=0).astype(BF16),
               state[i].astype(BF16)) for i in every]
    wv_b = [(x_sol[i][:, 0:hd] - wq[i][0:ch]).astype(BF16) for i in every]
    o = [wq[i][ch:2 * ch] + _dot(a_qk[i].astype(BF16), wv_b[i]) for i in every]
    s_new = [jnp.exp(glast[i]) * state[i]
             + _dot_tn((k[i] * jnp.exp(glast[i] - gc[i])).astype(BF16), wv_b[i]) for i in every]
    for i, (bi, h) in enumerate(chains):
        s_ref[bi, h] = s_new[i]
        on = o[i] * lax.rsqrt(jnp.mean(o[i] * o[i], axis=-1, keepdims=True) + RMS_EPS) * ng_ref[...]
        o_ref[bi, :, h * hd:(h + 1) * hd] = on.astype(o_ref.dtype)

    xbuf_ref[:, top - keep:top, :] = xbuf_ref[:, top + ch - keep:top + ch, :]


GDN_BATCHES = 2


def _gdn(z3, ab3, abt4, conv0, s0, conv_w, pcol, prow, norm_g, ch):
    bsz, t, _ = z3.shape
    nb = GDN_BATCHES
    width3 = 3 * B_HEADS * B_HEAD_DIM
    width = B_HEADS * B_HEAD_DIM
    return pl.pallas_call(
        _gdn_kernel,
        grid=(bsz // nb, t // ch),
        in_specs=[
            pl.BlockSpec((nb, ch, width3), lambda b, c: (b, c, 1)),
            pl.BlockSpec((nb, ch, LANES), lambda b, c: (b, c, 0)),
            pl.BlockSpec((nb, None, 2 * B_HEADS, ch), lambda b, c: (b, c, 0, 0)),
            pl.BlockSpec((nb, B_CONV - 1, width3), lambda b, c: (b, 0, 0)),
            pl.BlockSpec((nb, B_HEADS, B_HEAD_DIM, B_HEAD_DIM), lambda b, c: (b, 0, 0, 0)),
            pl.BlockSpec(conv_w.shape, lambda b, c: (0, 0)),
            pl.BlockSpec(pcol.shape, lambda b, c: (0, 0)),
            pl.BlockSpec(prow.shape, lambda b, c: (0, 0, 0)),
            pl.BlockSpec(norm_g.shape, lambda b, c: (0, 0)),
        ],
        out_specs=[
            pl.BlockSpec((nb, ch, width), lambda b, c: (b, c, 0)),
            pl.BlockSpec((nb, B_HEADS, B_HEAD_DIM, B_HEAD_DIM), lambda b, c: (b, 0, 0, 0)),
        ],
        out_shape=[
            jax.ShapeDtypeStruct((bsz, t, width), BF16),
            jax.ShapeDtypeStruct((bsz, B_HEADS, B_HEAD_DIM, B_HEAD_DIM), F32),
        ],
        scratch_shapes=[pltpu.VMEM((nb, SUBLANES + ch, width3), F32)],
        compiler_params=_params("arbitrary", "arbitrary"),
        name="gdn",
    )(z3, ab3, abt4, conv0, s0, conv_w, pcol, prow, norm_g)


def _memattn_kernel(q_ref, mk_ref, mv_ref, o_ref, kb_ref, vb_ref):
    @pl.when(pl.program_id(1) == 0)
    def _():
        kb_ref[...] = mk_ref[...].astype(BF16)
        vb_ref[...] = mv_ref[...].astype(BF16)

    dh = C_HEAD_DIM
    for h in range(C_HEADS):
        q = q_ref[:, h * dh:(h + 1) * dh].astype(BF16)
        s = _dot_nt(q, kb_ref[:, h * dh:(h + 1) * dh]) * (dh ** -0.5)
        m = jnp.max(s, axis=-1, keepdims=True)
        p = jnp.exp(s - m)
        l = jnp.sum(p, axis=-1, keepdims=True)
        o = _dot(p.astype(BF16), vb_ref[:, h * dh:(h + 1) * dh]) * (1.0 / l)
        o_ref[:, h * dh:(h + 1) * dh] = o.astype(o_ref.dtype)


def _memattn(z3, q_blk, mk, mv, tq):
    bsz, t, _ = z3.shape
    mem = mk.shape[1]
    width = C_HEADS * C_HEAD_DIM
    return pl.pallas_call(
        _memattn_kernel,
        grid=(bsz, t // tq),
        in_specs=[
            pl.BlockSpec((None, tq, width), lambda b, i: (b, i, q_blk)),
            pl.BlockSpec((None, mem, width), lambda b, i: (b, 0, 0)),
            pl.BlockSpec((None, mem, width), lambda b, i: (b, 0, 0)),
        ],
        out_specs=pl.BlockSpec((None, tq, width), lambda b, i: (b, i, 0)),
        out_shape=jax.ShapeDtypeStruct((bsz, t, width), BF16),
        scratch_shapes=[pltpu.VMEM((mem, width), BF16), pltpu.VMEM((mem, width), BF16)],
        compiler_params=_params("arbitrary", "arbitrary"),
        name="memattn",
    )(z3, mk, mv)


def _layer_norm(x, g, b):
    mu = jnp.mean(x, axis=-1, keepdims=True)
    xc = x - mu
    var = jnp.mean(xc * xc, axis=-1, keepdims=True)
    return xc * lax.rsqrt(var + LN_EPS) * g + b


FFN_PARTS = 2


def _ffn_kernel(x_ref, ga_ref, gb_ref, gc_ref, oa_ref, ob_ref, oc_ref, wo_ref, w1_ref, w2_ref,
                vec_ref, b1_ref, y_ref, *, alpha, ff_blk):
    bm = x_ref.shape[0]
    rows = [pl.ds(r * (bm // FFN_PARTS), bm // FFN_PARTS) for r in range(FFN_PARTS)]
    parts = range(FFN_PARTS)

    def gated(g_ref, o_ref, r):
        return _sigmoid(g_ref[r, :].astype(F32)) * o_ref[r, :].astype(F32)

    merged = [(gated(ga_ref, oa_ref, r) + gated(gb_ref, ob_ref, r) + gated(gc_ref, oc_ref, r)).astype(BF16)
              for r in rows]
    proj = [_dot(merged[i], wo_ref[...]) for i in parts]
    h = [_layer_norm(alpha * x_ref[rows[i], :] + proj[i], vec_ref[0:1, :], vec_ref[1:2, :]) for i in parts]
    hb = [h[i].astype(BF16) for i in parts]
    acc = [None] * FFN_PARTS
    for k0 in range(0, w1_ref.shape[1], ff_blk):
        f = [_dot(hb[i], w1_ref[:, k0:k0 + ff_blk]) + b1_ref[:, k0:k0 + ff_blk] for i in parts]
        f = [jnp.square(jnp.maximum(f[i], 0.0)).astype(BF16) for i in parts]
        d = [_dot(f[i], w2_ref[k0:k0 + ff_blk, :]) for i in parts]
        acc = [d[i] if acc[i] is None else acc[i] + d[i] for i in parts]
    for i in parts:
        y_ref[rows[i], :] = _layer_norm(alpha * h[i] + acc[i] + vec_ref[2:3, :], vec_ref[3:4, :], vec_ref[4:5, :])


def _ffn(x2, z2, gate_blk0, oa, ob, oc, wo, w1, w2, vec, b1, alpha, bm):
    m, d = x2.shape
    dff = w1.shape[1]
    const = dict(pipeline_mode=pl.Buffered(1))
    row = lambda i: (i, 0)
    return pl.pallas_call(
        functools.partial(_ffn_kernel, alpha=alpha, ff_blk=1024),
        grid=(m // bm,),
        in_specs=[
            pl.BlockSpec((bm, d), row),
            pl.BlockSpec((bm, d), lambda i: (i, gate_blk0)),
            pl.BlockSpec((bm, d), lambda i: (i, gate_blk0 + 1)),
            pl.BlockSpec((bm, d), lambda i: (i, gate_blk0 + 2)),
            pl.BlockSpec((bm, d), row),
            pl.BlockSpec((bm, d), row),
            pl.BlockSpec((bm, d), row),
            pl.BlockSpec((d, d), lambda i: (0, 0), **const),
            pl.BlockSpec((d, dff), lambda i: (0, 0), **const),
            pl.BlockSpec((dff, d), lambda i: (0, 0), **const),
            pl.BlockSpec(vec.shape, lambda i: (0, 0), **const),
            pl.BlockSpec(b1.shape, lambda i: (0, 0), **const),
        ],
        out_specs=pl.BlockSpec((bm, d), row),
        out_shape=jax.ShapeDtypeStruct((m, d), F32),
        compiler_params=_params("arbitrary"),
        name="merge_ffn",
    )(x2, z2, z2, z2, oa, ob, oc, wo, w1, w2, vec, b1)


QC_BLOCK, GATE_BLOCK0 = 6, 7


def _group_layer(x3, z_dtype, attn_fn, conv0, ssm0, mem_k, mem_v, lw, keep=None):
    bsz, t, d = x3.shape
    m = bsz * t
    ch = min(CHUNK, t)
    x2 = x3.reshape(m, d)
    z2, ab, *tails = _inproj(x2, lw["w_in"], lw["wab"], lw["n_main"], z_dtype, min(1024, m), d,
                             seq_len=None if keep is None else t, keep=keep)
    z3 = z2.reshape(bsz, t, -1)
    o_a = attn_fn(z3)
    ab3 = ab.reshape(bsz, t, LANES)
    abt4 = ab[:, 0:2 * B_HEADS].reshape(bsz, t // ch, ch, 2 * B_HEADS).transpose(0, 1, 3, 2)
    o_b, ssm = _gdn(z3, ab3, abt4, conv0, ssm0, lw["conv_w"], lw["pcol"], lw["prow"], lw["norm_g"], ch)
    o_c = _memattn(z3, QC_BLOCK, mem_k, mem_v, min(512, t))
    y = _ffn(x2, z2, GATE_BLOCK0, o_a.reshape(m, d), o_b.reshape(m, d), o_c.reshape(m, d),
             lw["wo"], lw["w1"], lw["w2"], lw["vec"], lw["b1"], lw["alpha"], min(512, m))
    return y.reshape(bsz, t, d), z3, ssm, tails


def kernel(x_prompt, x_sample, cache_a_k, cache_a_v, state_b_conv, state_b_ssm, cache_mem_k, cache_mem_v, mem_prompt, w_in, w_b_conv, b_a_log, b_dt_bias, b_norm_g, a_rel_bias, w_mem_kv, w_out, ln1_g, ln1_b, w_ff1, b_ff1, w_ff2, b_ff2, ln2_g, ln2_b):
    depth = w_in.shape[0]
    bp, tp, d = x_prompt.shape
    bs, tn, _ = x_sample.shape
    window = A_LEFT_CHUNKS * CHUNK
    keep_p = min(window, tp)
    lc = cache_a_k.shape[2]
    n_main = w_in.shape[2] - 2 * B_HEADS
    a_width = A_HEADS * A_HEAD_DIM
    b_width3 = 3 * B_HEADS * B_HEAD_DIM
    alpha = (2.0 * depth) ** 0.25
    xp, xs = x_prompt, x_sample
    outs = [[] for _ in range(10)]
    for l in range(depth):
        q_scale = jnp.where(jnp.arange(w_in.shape[2]) < a_width, (A_HEAD_DIM ** -0.5) * LOG2E, 1.0)
        w_in_b = (w_in[l] * q_scale).astype(BF16)
        vec = jnp.zeros((SUBLANES, d), F32)
        vec = vec.at[0].set(ln1_g[l]).at[1].set(ln1_b[l]).at[2].set(b_ff2[l]).at[3].set(ln2_g[l]).at[4].set(ln2_b[l])
        pcol = jnp.zeros((SUBLANES, LANES), F32)
        pcol = pcol.at[0, 0:B_HEADS].set(b_a_log[l]).at[1, 0:B_HEADS].set(b_dt_bias[l])
        lw = dict(
            w_in=w_in_b, n_main=n_main,
            wab=jnp.pad(w_in_b[:, n_main:], ((0, 0), (0, LANES - 2 * B_HEADS))),
            conv_w=w_b_conv[l],
            pcol=pcol,
            prow=jnp.stack([jnp.broadcast_to(jnp.pad(b_a_log[l], (0, B_HEADS))[:, None], (2 * B_HEADS, LANES)),
                            jnp.broadcast_to(jnp.pad(b_dt_bias[l], (0, B_HEADS))[:, None], (2 * B_HEADS, LANES))]),
            norm_g=b_norm_g[l].reshape(1, B_HEAD_DIM),
            wo=w_out[l].astype(BF16), w1=w_ff1[l].astype(BF16), w2=w_ff2[l].astype(BF16),
            vec=vec, b1=b_ff1[l].reshape(1, -1), alpha=alpha,
        )
        band = window + CHUNK
        bias_tab = _bias_table(a_rel_bias[l], CHUNK, band)
        bias_cols = bias_tab.shape[2]
        col0 = band - bias_cols

        bias_p = bias_tab.reshape(A_HEADS // A_GROUP, A_GROUP * CHUNK, bias_cols)
        mk, mv = _mem_kv(mem_prompt, w_mem_kv[l].astype(BF16))
        conv0 = jnp.zeros((bp, B_CONV - 1, b_width3), F32)
        ssm0 = jnp.zeros((bp, B_HEADS, B_HEAD_DIM, B_HEAD_DIM), F32)
        xp, _, ssm_p, (k_tail, v_tail, conv_tail) = _group_layer(
            xp, BF16, lambda z3: _attn_prompt(z3, bias_p, CHUNK, window), conv0, ssm0, mk, mv, lw, keep=keep_p)
        outs[0].append(k_tail.reshape(bp, keep_p, A_HEADS, A_HEAD_DIM))
        outs[1].append(v_tail.reshape(bp, keep_p, A_HEADS, A_HEAD_DIM))
        outs[2].append(conv_tail[:, SUBLANES - (B_CONV - 1):])
        outs[3].append(ssm_p)
        outs[4].append(mk.reshape(bp, -1, C_HEADS, C_HEAD_DIM))
        outs[5].append(mv.reshape(bp, -1, C_HEADS, C_HEAD_DIM))

        bias_s = bias_tab[:, 0:tn, 0:lc + tn - col0].reshape(A_HEADS // A_GROUP, A_GROUP * tn, lc + tn - col0)
        ck = cache_a_k[l].reshape(bs, lc, a_width)
        cv = cache_a_v[l].reshape(bs, lc, a_width)
        xs, zs3, ssm_s, _ = _group_layer(
            xs, F32, lambda z3: _attn_sample(z3, ck, cv, bias_s), state_b_conv[l], state_b_ssm[l],
            cache_mem_k[l].reshape(bs, -1, C_HEADS * C_HEAD_DIM),
            cache_mem_v[l].reshape(bs, -1, C_HEADS * C_HEAD_DIM), lw)
        k_new = zs3[:, :, a_width:2 * a_width].reshape(bs, tn, A_HEADS, A_HEAD_DIM)
        v_new = zs3[:, :, 2 * a_width:3 * a_width].reshape(bs, tn, A_HEADS, A_HEAD_DIM)
        outs[6].append(jnp.concatenate([cache_a_k[l][:, tn:], k_new], axis=1))
        outs[7].append(jnp.concatenate([cache_a_v[l][:, tn:], v_new], axis=1))
        qkvb_s = zs3[:, :, 3 * a_width:3 * a_width + b_width3]
        outs[8].append(jnp.concatenate([state_b_conv[l], qkvb_s], axis=1)[:, tn:])
        outs[9].append(ssm_s)
    return (xp, xs) + tuple(jnp.stack(o) for o in outs)
```

```python
import functools

import jax
import jax.numpy as jnp
from jax import lax
from jax.experimental import pallas as pl
from jax.experimental.pallas import tpu as pltpu

F32 = jnp.float32
BF16 = jnp.bfloat16

CHUNK = 64
A_HEADS = 16
A_HEAD_DIM = 64
A_GROUP = 4
A_LEFT_CHUNKS = 8
A_REL_CLIP = 128
B_HEADS = 8
B_HEAD_DIM = 128
B_CONV = 4
C_HEADS = 4
C_HEAD_DIM = 256
N_BRANCH = 3
LN_EPS = 1e-5
RMS_EPS = 1e-6
L2_EPS = 1e-6
NEG_INF = -1e30

LANES = 128
SUBLANES = 8
VMEM_LIMIT_BYTES = 56 * 1024 * 1024


def _params(*sem):
    return pltpu.CompilerParams(dimension_semantics=sem, vmem_limit_bytes=VMEM_LIMIT_BYTES)


def _dot(a, b):
    return jnp.dot(a, b, preferred_element_type=F32)


def _dot_nt(a, b):
    return lax.dot_general(a, b, (((1,), (1,)), ((), ())), preferred_element_type=F32)


def _dot_tn(a, b):
    return lax.dot_general(a, b, (((0,), (0,)), ((), ())), preferred_element_type=F32)


def _sigmoid(x):
    return 0.5 + 0.5 * jnp.tanh(0.5 * x)


def _silu(x):
    hx = 0.5 * x
    return hx + hx * jnp.tanh(hx)


K_BLOCK, V_BLOCK, CONV_BLOCK0, CONV_BLOCKS = 1, 2, 3, 3


def _inproj_kernel(x_ref, w_ref, wab_ref, z_ref, ab_ref, xb_ref):
    @pl.when(pl.program_id(1) == 0)
    def _():
        xb = x_ref[...].astype(BF16)
        xb_ref[...] = xb
        ab_ref[...] = _dot(xb, wab_ref[...])

    z_ref[...] = _dot(xb_ref[...], w_ref[...]).astype(z_ref.dtype)


def _inproj(x2, w, wab, n_main, out_dtype, bm, bn):
    m, d = x2.shape
    return pl.pallas_call(
        _inproj_kernel,
        grid=(m // bm, n_main // bn),
        in_specs=[
            pl.BlockSpec((bm, d), lambda i, j: (i, 0)),
            pl.BlockSpec((d, bn), lambda i, j: (0, j)),
            pl.BlockSpec((d, LANES), lambda i, j: (0, 0)),
        ],
        out_specs=[
            pl.BlockSpec((bm, bn), lambda i, j: (i, j)),
            pl.BlockSpec((bm, LANES), lambda i, j: (i, 0)),
        ],
        out_shape=[
            jax.ShapeDtypeStruct((m, n_main), out_dtype),
            jax.ShapeDtypeStruct((m, LANES), F32),
        ],
        scratch_shapes=[pltpu.VMEM((bm, d), BF16)],
        compiler_params=_params("arbitrary", "arbitrary"),
        name="inproj",
    )(x2, w, wab)


INPROJ_SUB = 2 * B_HEAD_DIM


def _inproj_seq_kernel(x_ref, w_ref, wab_ref, z_ref, ab_ref, kt_ref, vt_ref, ct_ref, xb_ref):
    j = pl.program_id(1)
    bm, bn = z_ref.shape
    keep = kt_ref.shape[0]

    @pl.when(j == 0)
    def _():
        xb = x_ref[...].astype(BF16)
        xb_ref[...] = xb
        ab_ref[...] = _dot(xb, wab_ref[...])
        kt_ref[...] = jnp.zeros(kt_ref.shape, F32)
        vt_ref[...] = jnp.zeros(vt_ref.shape, F32)

    tails = []
    for n0 in range(0, bn, INPROJ_SUB):
        cols = slice(n0, n0 + INPROJ_SUB)
        acc = _dot(xb_ref[...], w_ref[:, cols])
        z_ref[:, cols] = acc.astype(z_ref.dtype)
        tail = acc[bm - keep:, :]
        kt_ref[:, cols] = jnp.where(j == K_BLOCK, tail, kt_ref[:, cols])
        vt_ref[:, cols] = jnp.where(j == V_BLOCK, tail, vt_ref[:, cols])
        tails.append(acc[bm - SUBLANES:, :])

    @pl.when((j >= CONV_BLOCK0) & (j < CONV_BLOCK0 + CONV_BLOCKS))
    def _():
        ct_ref[...] = jnp.concatenate(tails, axis=1)


def _inproj_seq(x3, w, wab, n_main, keep):
    nseq, t, d = x3.shape
    bm, bn = t, d
    conv_blk = lambda i, j: jnp.clip(j - CONV_BLOCK0, 0, CONV_BLOCKS - 1)
    return pl.pallas_call(
        _inproj_seq_kernel,
        grid=(nseq, n_main // bn),
        in_specs=[
            pl.BlockSpec((None, bm, d), lambda i, j: (i, 0, 0)),
            pl.BlockSpec((d, bn), lambda i, j: (0, j)),
            pl.BlockSpec((d, LANES), lambda i, j: (0, 0)),
        ],
        out_specs=[
            pl.BlockSpec((None, bm, bn), lambda i, j: (i, 0, j)),
            pl.BlockSpec((None, bm, LANES), lambda i, j: (i, 0, 0)),
            pl.BlockSpec((None, keep, bn), lambda i, j: (i, 0, 0)),
            pl.BlockSpec((None, keep, bn), lambda i, j: (i, 0, 0)),
            pl.BlockSpec((None, SUBLANES, bn), lambda i, j: (i, 0, conv_blk(i, j))),
        ],
        out_shape=[
            jax.ShapeDtypeStruct((nseq, t, n_main), BF16),
            jax.ShapeDtypeStruct((nseq, t, LANES), F32),
            jax.ShapeDtypeStruct((nseq, keep, bn), F32),
            jax.ShapeDtypeStruct((nseq, keep, bn), F32),
            jax.ShapeDtypeStruct((nseq, SUBLANES, CONV_BLOCKS * bn), F32),
        ],
        scratch_shapes=[pltpu.VMEM((bm, d), BF16)],
        compiler_params=_params("arbitrary", "arbitrary"),
        name="inproj_seq",
    )(x3, w, wab)


def _mem_kv_kernel(x_ref, w_ref, k_ref, v_ref):
    acc = _dot(x_ref[...].astype(BF16), w_ref[...])
    half = k_ref.shape[1]
    k_ref[...] = acc[:, 0:half]
    v_ref[...] = acc[:, half:2 * half]


def _mem_kv(mem, w):
    bsz, mt, d = mem.shape
    half = w.shape[1] // 2
    out = jax.ShapeDtypeStruct((bsz, mt, half), F32)
    return pl.pallas_call(
        _mem_kv_kernel,
        grid=(bsz,),
        in_specs=[
            pl.BlockSpec((None, mt, d), lambda b: (b, 0, 0)),
            pl.BlockSpec(w.shape, lambda b: (0, 0)),
        ],
        out_specs=[pl.BlockSpec((None, mt, half), lambda b: (b, 0, 0))] * 2,
        out_shape=[out, out],
        compiler_params=_params("arbitrary"),
        name="mem_kv",
    )(mem, w)


LOG2E = 1.4426950408889634


def _bias_col0(window):
    return (window - A_REL_CLIP + 1) // LANES * LANES


def _bias_table_kernel(rb_ref, o_ref, *, window, col0):
    h = pl.program_id(0)
    tq, cols = o_ref.shape
    i = lax.broadcasted_iota(jnp.int32, (tq, cols), 0)
    j = lax.broadcasted_iota(jnp.int32, (tq, cols), 1) + col0
    idx = jnp.clip(i - j + window, -A_REL_CLIP, A_REL_CLIP) + A_REL_CLIP
    lowest = max(-A_REL_CLIP, window - (col0 + cols - 1)) + A_REL_CLIP

    def body(r, acc):
        return jnp.where(idx == r, rb_ref[h, r], acc)

    tab = lax.fori_loop(lowest, 2 * A_REL_CLIP + 1, body, jnp.zeros((tq, cols), F32))
    o_ref[...] = (tab - rb_ref[h, 2 * A_REL_CLIP]) * LOG2E


def _bias_table(rel_bias, tq, band):
    heads = rel_bias.shape[0]
    window = band - tq
    col0 = _bias_col0(window)
    return pl.pallas_call(
        functools.partial(_bias_table_kernel, window=window, col0=col0),
        grid=(heads,),
        in_specs=[pl.BlockSpec(memory_space=pltpu.SMEM)],
        out_specs=pl.BlockSpec((None, tq, band - col0), lambda h: (h, 0, 0)),
        out_shape=jax.ShapeDtypeStruct((heads, tq, band - col0), F32),
        compiler_params=_params("arbitrary"),
        name="bias_table",
    )(rel_bias)


A_GROUP_WIDTH = A_GROUP * A_HEAD_DIM


def _lane_head(tq):
    return jnp.right_shift(lax.broadcasted_iota(jnp.int32, (tq, A_GROUP_WIDTH), 1), A_HEAD_DIM.bit_length() - 1)


def _attn_scores(qg, kg, bias_g, valid_from):
    lane_head = _lane_head(qg.shape[0])
    qg = qg.astype(BF16)
    zero = jnp.zeros_like(qg)
    qm = jnp.concatenate([jnp.where(lane_head == h, qg, zero) for h in range(A_GROUP)], axis=0)
    s = _dot_nt(qm, kg)
    col0 = s.shape[1] - bias_g.shape[1]
    s = jnp.concatenate([s[:, 0:col0], s[:, col0:] + bias_g], axis=1)
    if valid_from is not None:
        col = lax.broadcasted_iota(jnp.int32, s.shape, 1)
        s = jnp.where(col >= valid_from, s, NEG_INF)
    m = jnp.max(s, axis=-1, keepdims=True)
    p = jnp.exp2(s - m)
    return p.astype(BF16), 1.0 / jnp.sum(p, axis=-1, keepdims=True)


def _attn_values(p, rl, vg):
    tq = p.shape[0] // A_GROUP
    o = _dot(p, vg)
    if rl.shape[1] == 1:
        o = o * rl
    else:
        o = o * jnp.concatenate([rl] * (A_GROUP_WIDTH // LANES), axis=1)
    lane_head = _lane_head(tq)
    out = jnp.zeros((tq, A_GROUP_WIDTH), F32)
    for h in range(A_GROUP):
        out = jnp.where(lane_head == h, o[h * tq:(h + 1) * tq], out)
    return out


A_CHUNKS_PER_STEP = 4


def _attn_prompt_kernel(q_ref, k_ref, v_ref, bias_ref, o_ref, kpad_ref, vpad_ref, *, window, tq):
    s = pl.program_id(1)
    band = window + tq
    gw = A_GROUP_WIDTH
    cps = q_ref.shape[0] // tq

    @pl.when(s == 0)
    def _():
        zeros = jnp.zeros((window, kpad_ref.shape[1]), BF16)
        kpad_ref[0:window, :] = zeros
        vpad_ref[0:window, :] = zeros
        kpad_ref[window:, :] = k_ref[...]
        vpad_ref[window:, :] = v_ref[...]

    for j in range(cps):
        c = s * cps + j
        start = pl.multiple_of(c * tq, tq)
        rows = slice(j * tq, (j + 1) * tq)
        for g in range(A_HEADS // A_GROUP):
            cols = slice(g * gw, (g + 1) * gw)
            p, rl = _attn_scores(q_ref[rows, cols], kpad_ref[pl.ds(start, band), cols], bias_ref[g],
                                 window - c * tq)
            o_ref[rows, cols] = _attn_values(p, rl, vpad_ref[pl.ds(start, band), cols]).astype(o_ref.dtype)


def _attn_prompt(z3, bias_g, tq, window):
    bsz, t, _ = z3.shape
    width = A_HEADS * A_HEAD_DIM
    rows = tq * min(A_CHUNKS_PER_STEP, t // tq)
    return pl.pallas_call(
        functools.partial(_attn_prompt_kernel, window=window, tq=tq),
        grid=(bsz, t // rows),
        in_specs=[
            pl.BlockSpec((None, rows, width), lambda b, s: (b, s, 0)),
            pl.BlockSpec((None, t, width), lambda b, s: (b, 0, K_BLOCK)),
            pl.BlockSpec((None, t, width), lambda b, s: (b, 0, V_BLOCK)),
            pl.BlockSpec(bias_g.shape, lambda b, s: (0, 0, 0)),
        ],
        out_specs=pl.BlockSpec((None, rows, width), lambda b, s: (b, s, 0)),
        out_shape=jax.ShapeDtypeStruct((bsz, t, width), BF16),
        scratch_shapes=[pltpu.VMEM((t + window, width), BF16), pltpu.VMEM((t + window, width), BF16)],
        compiler_params=_params("arbitrary", "arbitrary"),
        name="attn_prompt",
    )(z3, z3, z3, bias_g)


def _attn_sample_kernel(q_ref, kn_ref, vn_ref, ck_ref, cv_ref, bias_ref, o_ref, kb_ref, vb_ref):
    gw = A_GROUP * A_HEAD_DIM
    lc = ck_ref.shape[0]
    kb_ref[0:lc, :] = ck_ref[...].astype(BF16)
    vb_ref[0:lc, :] = cv_ref[...].astype(BF16)
    kb_ref[lc:, :] = kn_ref[...].astype(BF16)
    vb_ref[lc:, :] = vn_ref[...].astype(BF16)
    for g in range(A_HEADS // A_GROUP):
        cols = slice(g * gw, (g + 1) * gw)
        p, rl = _attn_scores(q_ref[:, cols], kb_ref[:, cols], bias_ref[g], None)
        o_ref[:, cols] = _attn_values(p, rl, vb_ref[:, cols]).astype(o_ref.dtype)


def _attn_sample(z3, ck, cv, bias_g):
    bsz, tq, _ = z3.shape
    lc = ck.shape[1]
    width = A_HEADS * A_HEAD_DIM
    return pl.pallas_call(
        _attn_sample_kernel,
        grid=(bsz,),
        in_specs=[
            pl.BlockSpec((None, tq, width), lambda b: (b, 0, 0)),
            pl.BlockSpec((None, tq, width), lambda b: (b, 0, K_BLOCK)),
            pl.BlockSpec((None, tq, width), lambda b: (b, 0, V_BLOCK)),
            pl.BlockSpec((None, lc, width), lambda b: (b, 0, 0)),
            pl.BlockSpec((None, lc, width), lambda b: (b, 0, 0)),
            pl.BlockSpec(bias_g.shape, lambda b: (0, 0, 0)),
        ],
        out_specs=pl.BlockSpec((None, tq, width), lambda b: (b, 0, 0)),
        out_shape=jax.ShapeDtypeStruct((bsz, tq, width), BF16),
        scratch_shapes=[pltpu.VMEM((lc + tq, width), BF16), pltpu.VMEM((lc + tq, width), BF16)],
        compiler_params=_params("arbitrary"),
        name="attn_sample",
    )(z3, z3, z3, ck, cv, bias_g)


def _gdn_kernel(x_ref, ab_ref, conv0_ref, s0_ref, cw_ref, pcol_ref, ng_ref,
                o_ref, s_ref, xbuf_ref):
    c = pl.program_id(1)
    nb, ch, _ = x_ref.shape
    hd = B_HEAD_DIM
    width = B_HEADS * hd
    keep = B_CONV - 1
    shift_on_mxu = x_ref.dtype == BF16
    top = xbuf_ref.shape[1] - (0 if shift_on_mxu else ch)
    chains = [(bi, h) for bi in range(nb) for h in range(B_HEADS)]
    every = range(len(chains))

    @pl.when(c == 0)
    def _():
        hist = jnp.concatenate([jnp.zeros((nb, top - keep, xbuf_ref.shape[2]), F32),
                                conv0_ref[...].astype(F32)], axis=1)
        xbuf_ref[:, 0:top, :] = hist.astype(xbuf_ref.dtype)
        s_ref[...] = s0_ref[...].astype(F32)

    if shift_on_mxu:
        t_r = lax.broadcasted_iota(jnp.int32, (ch, top + ch), 0)
        u_c = lax.broadcasted_iota(jnp.int32, (ch, top + ch), 1)
        shift01 = jnp.concatenate([jnp.where(u_c == top + t_r - (keep - i), 1.0, 0.0) for i in range(keep)],
                                  axis=0).astype(BF16)
        shifted = [_dot(shift01, jnp.concatenate([xbuf_ref[bi], x_ref[bi]], axis=0)) for bi in range(nb)]
    else:
        xbuf_ref[:, top:top + ch, :] = x_ref[...]

    t_i = lax.broadcasted_iota(jnp.int32, (ch, ch), 0)
    s_i = lax.broadcasted_iota(jnp.int32, (ch, ch), 1)
    tri_incl = (t_i >= s_i)
    tri_strict = (t_i > s_i)
    eye = jnp.where(t_i == s_i, 1.0, 0.0)
    lane = lax.broadcasted_iota(jnp.int32, (ch, LANES), 1)
    lane_row = lax.broadcasted_iota(jnp.int32, (1, ch), 1)
    frame = lax.broadcasted_iota(jnp.int32, (ch, LANES), 0)

    gcum_col, beta_col, gcum_row = [], [], []
    for bi in range(nb):
        ab = ab_ref[bi]
        g = -jnp.exp(pcol_ref[0:1, :]) * jax.nn.softplus(ab + pcol_ref[1:2, :])
        step = 1
        while step < ch:
            g = g + jnp.where(frame >= step, pltpu.roll(g, step, axis=0), 0.0)
            step *= 2
        beta_col.append(_sigmoid(ab))
        gcum_col.append(g)
        gcum_row.append(g.T)

    def pick(x, idx):
        return jnp.sum(jnp.where(lane == idx, x, 0.0), axis=-1, keepdims=True)

    def conv_silu(bi, col0):
        cols = slice(col0, col0 + hd)
        if shift_on_mxu:
            acc = x_ref[bi, :, cols].astype(F32) * cw_ref[keep:keep + 1, cols]
            for i in range(keep):
                acc = acc + shifted[bi][i * ch:(i + 1) * ch, cols] * cw_ref[i:i + 1, cols]
        else:
            acc = xbuf_ref[bi, top:top + ch, cols] * cw_ref[keep:keep + 1, cols]
            for i in range(keep):
                acc = acc + xbuf_ref[bi, top - keep + i:top - keep + i + ch, cols] * cw_ref[i:i + 1, cols]
        return _silu(acc)

    def l2norm(x):
        return x * lax.rsqrt(jnp.sum(x * x, axis=-1, keepdims=True) + L2_EPS)

    q = [l2norm(conv_silu(bi, h * hd)) * (hd ** -0.5) for bi, h in chains]
    k = [l2norm(conv_silu(bi, width + h * hd)) for bi, h in chains]
    v = [conv_silu(bi, 2 * width + h * hd) for bi, h in chains]
    gc = [pick(gcum_col[bi], h) for bi, h in chains]
    bc = [pick(beta_col[bi], B_HEADS + h) for bi, h in chains]
    gr = [gcum_row[bi][h:h + 1, :] for bi, h in chains]
    glast = [jnp.sum(jnp.where(lane_row == ch - 1, gr[i], 0.0), axis=-1, keepdims=True) for i in every]
    dec_incl = [jnp.exp(jnp.where(tri_incl, gc[i] - gr[i], NEG_INF)) for i in every]
    eg = [jnp.exp(gc[i]) for i in every]
    nbc = [-bc[i] for i in every]

    qk_kk = [_dot_nt(jnp.concatenate([q[i], k[i]], axis=0).astype(BF16), k[i].astype(BF16)) for i in every]
    a_qk = [qk_kk[i][0:ch] * dec_incl[i] for i in every]

    tp = [jnp.concatenate([(nbc[i] * qk_kk[i][ch:2 * ch]) * jnp.where(tri_strict, dec_incl[i], 0.0), eye], axis=1)
          for i in every]
    t_half = lax.broadcasted_iota(jnp.int32, (ch, 2 * ch), 1) >= ch
    for _ in range(max(1, (ch - 1).bit_length())):
        tp_b = [tp[i].astype(BF16) for i in every]
        tp = [_dot(tp_b[i][:, 0:ch], tp_b[i]) + jnp.where(t_half, tp[i], 0.0) for i in every]
    rhs = [jnp.concatenate([bc[i] * v[i], (bc[i] * eg[i]) * k[i]], axis=1).astype(BF16) for i in every]
    x_sol = [_dot(tp[i][:, ch:2 * ch].astype(BF16), rhs[i]) for i in every]

    state = [s_ref[bi, h] for bi, h in chains]
    wq = [_dot(jnp.concatenate([x_sol[i][:, hd:2 * hd], q[i] * eg[i]], axis=0).astype(BF16),
               state[i].astype(BF16)) for i in every]
    wv_b = [(x_sol[i][:, 0:hd] - wq[i][0:ch]).astype(BF16) for i in every]
    o = [wq[i][ch:2 * ch] + _dot(a_qk[i].astype(BF16), wv_b[i]) for i in every]
    s_new = [jnp.exp(glast[i]) * state[i]
             + _dot_tn((k[i] * jnp.exp(glast[i] - gc[i])).astype(BF16), wv_b[i]) for i in every]
    for i, (bi, h) in enumerate(chains):
        s_ref[bi, h] = s_new[i]
        on = o[i] * lax.rsqrt(jnp.mean(o[i] * o[i], axis=-1, keepdims=True) + RMS_EPS) * ng_ref[...]
        o_ref[bi, :, h * hd:(h + 1) * hd] = on.astype(o_ref.dtype)

    if shift_on_mxu:
        xbuf_ref[...] = x_ref[:, ch - top:ch, :]
    else:
        xbuf_ref[:, top - keep:top, :] = xbuf_ref[:, top + ch - keep:top + ch, :]


GDN_BATCHES = 2


def _gdn(z3, ab3, conv0, s0, conv_w, pcol, norm_g, ch):
    bsz, t, _ = z3.shape
    nb = GDN_BATCHES
    width3 = 3 * B_HEADS * B_HEAD_DIM
    width = B_HEADS * B_HEAD_DIM
    if z3.dtype == BF16:
        xbuf = pltpu.VMEM((nb, 2 * SUBLANES, width3), BF16)
    else:
        xbuf = pltpu.VMEM((nb, SUBLANES + ch, width3), F32)
    return pl.pallas_call(
        _gdn_kernel,
        grid=(bsz // nb, t // ch),
        in_specs=[
            pl.BlockSpec((nb, ch, width3), lambda b, c: (b, c, 1)),
            pl.BlockSpec((nb, ch, LANES), lambda b, c: (b, c, 0)),
            pl.BlockSpec((nb, B_CONV - 1, width3), lambda b, c: (b, 0, 0)),
            pl.BlockSpec((nb, B_HEADS, B_HEAD_DIM, B_HEAD_DIM), lambda b, c: (b, 0, 0, 0)),
            pl.BlockSpec(conv_w.shape, lambda b, c: (0, 0)),
            pl.BlockSpec(pcol.shape, lambda b, c: (0, 0)),
            pl.BlockSpec(norm_g.shape, lambda b, c: (0, 0)),
        ],
        out_specs=[
            pl.BlockSpec((nb, ch, width), lambda b, c: (b, c, 0)),
            pl.BlockSpec((nb, B_HEADS, B_HEAD_DIM, B_HEAD_DIM), lambda b, c: (b, 0, 0, 0)),
        ],
        out_shape=[
            jax.ShapeDtypeStruct((bsz, t, width), BF16),
            jax.ShapeDtypeStruct((bsz, B_HEADS, B_HEAD_DIM, B_HEAD_DIM), F32),
        ],
        scratch_shapes=[xbuf],
        compiler_params=_params("arbitrary", "arbitrary"),
        name="gdn",
    )(z3, ab3, conv0, s0, conv_w, pcol, norm_g)


def _memattn_kernel(q_ref, mk_ref, mv_ref, o_ref, kb_ref, vb_ref):
    @pl.when(pl.program_id(1) == 0)
    def _():
        kb_ref[...] = mk_ref[...].astype(BF16)
        vb_ref[...] = mv_ref[...].astype(BF16)

    dh = C_HEAD_DIM
    for h in range(C_HEADS):
        q = q_ref[:, h * dh:(h + 1) * dh].astype(BF16)
        s = _dot_nt(q, kb_ref[:, h * dh:(h + 1) * dh]) * (dh ** -0.5)
        m = jnp.max(s, axis=-1, keepdims=True)
        p = jnp.exp(s - m)
        l = jnp.sum(p, axis=-1, keepdims=True)
        o = _dot(p.astype(BF16), vb_ref[:, h * dh:(h + 1) * dh]) * (1.0 / l)
        o_ref[:, h * dh:(h + 1) * dh] = o.astype(o_ref.dtype)


def _memattn(z3, q_blk, mk, mv, tq):
    bsz, t, _ = z3.shape
    mem = mk.shape[1]
    width = C_HEADS * C_HEAD_DIM
    return pl.pallas_call(
        _memattn_kernel,
        grid=(bsz, t // tq),
        in_specs=[
            pl.BlockSpec((None, tq, width), lambda b, i: (b, i, q_blk)),
            pl.BlockSpec((None, mem, width), lambda b, i: (b, 0, 0)),
            pl.BlockSpec((None, mem, width), lambda b, i: (b, 0, 0)),
        ],
        out_specs=pl.BlockSpec((None, tq, width), lambda b, i: (b, i, 0)),
        out_shape=jax.ShapeDtypeStruct((bsz, t, width), BF16),
        scratch_shapes=[pltpu.VMEM((mem, width), BF16), pltpu.VMEM((mem, width), BF16)],
        compiler_params=_params("arbitrary", "arbitrary"),
        name="memattn",
    )(z3, mk, mv)


def _layer_norm(x, g, b):
    mu = jnp.mean(x, axis=-1, keepdims=True)
    xc = x - mu
    var = jnp.mean(xc * xc, axis=-1, keepdims=True)
    return xc * lax.rsqrt(var + LN_EPS) * g + b


FFN_PARTS = 2
FFN_HIDDEN_BLOCK = 1024


def _ffn_kernel(x_ref, ga_ref, gb_ref, gc_ref, oa_ref, ob_ref, oc_ref, wo_ref, w1_ref, w2_ref,
                vec_ref, b1_ref, y_ref, *, alpha, ff_blk):
    bm = x_ref.shape[0]
    rows = [pl.ds(r * (bm // FFN_PARTS), bm // FFN_PARTS) for r in range(FFN_PARTS)]
    parts = range(FFN_PARTS)

    def gated(g_ref, o_ref, r):
        return _sigmoid(g_ref[r, :].astype(F32)) * o_ref[r, :].astype(F32)

    merged = [(gated(ga_ref, oa_ref, r) + gated(gb_ref, ob_ref, r) + gated(gc_ref, oc_ref, r)).astype(BF16)
              for r in rows]
    proj = [_dot(merged[i], wo_ref[...]) for i in parts]
    h = [_layer_norm(alpha * x_ref[rows[i], :] + proj[i], vec_ref[0:1, :], vec_ref[1:2, :]) for i in parts]
    hb = [h[i].astype(BF16) for i in parts]
    acc = [None] * FFN_PARTS
    for k0 in range(0, w1_ref.shape[1], ff_blk):
        f = [_dot(hb[i], w1_ref[:, k0:k0 + ff_blk]) + b1_ref[:, k0:k0 + ff_blk] for i in parts]
        f = [jnp.square(jnp.maximum(f[i], 0.0)).astype(BF16) for i in parts]
        d = [_dot(f[i], w2_ref[k0:k0 + ff_blk, :]) for i in parts]
        acc = [d[i] if acc[i] is None else acc[i] + d[i] for i in parts]
    for i in parts:
        y_ref[rows[i], :] = _layer_norm(alpha * h[i] + acc[i] + vec_ref[2:3, :], vec_ref[3:4, :], vec_ref[4:5, :])


def _ffn(x2, z2, gate_blk0, oa, ob, oc, wo, w1, w2, vec, b1, alpha, bm):
    m, d = x2.shape
    dff = w1.shape[1]
    const = dict(pipeline_mode=pl.Buffered(1))
    row = lambda i: (i, 0)
    return pl.pallas_call(
        functools.partial(_ffn_kernel, alpha=alpha, ff_blk=FFN_HIDDEN_BLOCK),
        grid=(m // bm,),
        in_specs=[
            pl.BlockSpec((bm, d), row),
            pl.BlockSpec((bm, d), lambda i: (i, gate_blk0)),
            pl.BlockSpec((bm, d), lambda i: (i, gate_blk0 + 1)),
            pl.BlockSpec((bm, d), lambda i: (i, gate_blk0 + 2)),
            pl.BlockSpec((bm, d), row),
            pl.BlockSpec((bm, d), row),
            pl.BlockSpec((bm, d), row),
            pl.BlockSpec((d, d), lambda i: (0, 0), **const),
            pl.BlockSpec((d, dff), lambda i: (0, 0), **const),
            pl.BlockSpec((dff, d), lambda i: (0, 0), **const),
            pl.BlockSpec(vec.shape, lambda i: (0, 0), **const),
            pl.BlockSpec(b1.shape, lambda i: (0, 0), **const),
        ],
        out_specs=pl.BlockSpec((bm, d), row),
        out_shape=jax.ShapeDtypeStruct((m, d), F32),
        compiler_params=_params("arbitrary"),
        name="merge_ffn",
    )(x2, z2, z2, z2, oa, ob, oc, wo, w1, w2, vec, b1)


QC_BLOCK, GATE_BLOCK0 = 6, 7
INPROJ_ROWS = 2048


def _group_layer(x3, z_dtype, attn_fn, conv0, ssm0, mem_k, mem_v, lw, keep=None):
    bsz, t, d = x3.shape
    m = bsz * t
    ch = min(CHUNK, t)
    x2 = x3.reshape(m, d)
    if keep is not None:
        z3, ab3, *tails = _inproj_seq(x3, lw["w_in"], lw["wab"], lw["n_main"], keep)
        z2 = z3.reshape(m, -1)
    else:
        z2, ab = _inproj(x2, lw["w_in"], lw["wab"], lw["n_main"], z_dtype, min(INPROJ_ROWS, m), d)
        z3, ab3, tails = z2.reshape(bsz, t, -1), ab.reshape(bsz, t, LANES), None
    o_a = attn_fn(z3)
    o_b, ssm = _gdn(z3, ab3, conv0, ssm0, lw["conv_w"], lw["pcol"], lw["norm_g"], ch)
    o_c = _memattn(z3, QC_BLOCK, mem_k, mem_v, min(512, t))
    y = _ffn(x2, z2, GATE_BLOCK0, o_a.reshape(m, d), o_b.reshape(m, d), o_c.reshape(m, d),
             lw["wo"], lw["w1"], lw["w2"], lw["vec"], lw["b1"], lw["alpha"], min(512, m))
    return y.reshape(bsz, t, d), z3, ssm, tails


def kernel(x_prompt, x_sample, cache_a_k, cache_a_v, state_b_conv, state_b_ssm, cache_mem_k, cache_mem_v, mem_prompt, w_in, w_b_conv, b_a_log, b_dt_bias, b_norm_g, a_rel_bias, w_mem_kv, w_out, ln1_g, ln1_b, w_ff1, b_ff1, w_ff2, b_ff2, ln2_g, ln2_b):
    depth = w_in.shape[0]
    bp, tp, d = x_prompt.shape
    bs, tn, _ = x_sample.shape
    window = A_LEFT_CHUNKS * CHUNK
    keep_p = min(window, tp)
    lc = cache_a_k.shape[2]
    n_main = w_in.shape[2] - 2 * B_HEADS
    a_width = A_HEADS * A_HEAD_DIM
    b_width3 = 3 * B_HEADS * B_HEAD_DIM
    alpha = (2.0 * depth) ** 0.25
    xp, xs = x_prompt, x_sample
    outs = [[] for _ in range(10)]
    for l in range(depth):
        q_scale = jnp.where(jnp.arange(w_in.shape[2]) < a_width, (A_HEAD_DIM ** -0.5) * LOG2E, 1.0)
        w_in_b = (w_in[l] * q_scale).astype(BF16)
        vec = jnp.zeros((SUBLANES, d), F32)
        vec = vec.at[0].set(ln1_g[l]).at[1].set(ln1_b[l]).at[2].set(b_ff2[l]).at[3].set(ln2_g[l]).at[4].set(ln2_b[l])
        pcol = jnp.zeros((SUBLANES, LANES), F32)
        pcol = pcol.at[0, 0:B_HEADS].set(b_a_log[l]).at[1, 0:B_HEADS].set(b_dt_bias[l])
        lw = dict(
            w_in=w_in_b, n_main=n_main,
            wab=jnp.pad(w_in_b[:, n_main:], ((0, 0), (0, LANES - 2 * B_HEADS))),
            conv_w=w_b_conv[l],
            pcol=pcol,
            norm_g=b_norm_g[l].reshape(1, B_HEAD_DIM),
            wo=w_out[l].astype(BF16), w1=w_ff1[l].astype(BF16), w2=w_ff2[l].astype(BF16),
            vec=vec, b1=b_ff1[l].reshape(1, -1), alpha=alpha,
        )
        band = window + CHUNK
        bias_tab = _bias_table(a_rel_bias[l], CHUNK, band)
        bias_cols = bias_tab.shape[2]
        col0 = band - bias_cols

        bias_p = bias_tab.reshape(A_HEADS // A_GROUP, A_GROUP * CHUNK, bias_cols)
        mk, mv = _mem_kv(mem_prompt, w_mem_kv[l].astype(BF16))
        conv0 = jnp.zeros((bp, B_CONV - 1, b_width3), F32)
        ssm0 = jnp.zeros((bp, B_HEADS, B_HEAD_DIM, B_HEAD_DIM), F32)
        xp, _, ssm_p, (k_tail, v_tail, conv_tail) = _group_layer(
            xp, BF16, lambda z3: _attn_prompt(z3, bias_p, CHUNK, window), conv0, ssm0, mk, mv, lw, keep=keep_p)
        outs[0].append(k_tail.reshape(bp, keep_p, A_HEADS, A_HEAD_DIM))
        outs[1].append(v_tail.reshape(bp, keep_p, A_HEADS, A_HEAD_DIM))
        outs[2].append(conv_tail[:, SUBLANES - (B_CONV - 1):])
        outs[3].append(ssm_p)
        outs[4].append(mk.reshape(bp, -1, C_HEADS, C_HEAD_DIM))
        outs[5].append(mv.reshape(bp, -1, C_HEADS, C_HEAD_DIM))

        bias_s = bias_tab[:, 0:tn, 0:lc + tn - col0].reshape(A_HEADS // A_GROUP, A_GROUP * tn, lc + tn - col0)
        ck = cache_a_k[l].reshape(bs, lc, a_width)
        cv = cache_a_v[l].reshape(bs, lc, a_width)
        xs, zs3, ssm_s, _ = _group_layer(
            xs, F32, lambda z3: _attn_sample(z3, ck, cv, bias_s), state_b_conv[l], state_b_ssm[l],
            cache_mem_k[l].reshape(bs, -1, C_HEADS * C_HEAD_DIM),
            cache_mem_v[l].reshape(bs, -1, C_HEADS * C_HEAD_DIM), lw)
        k_new = zs3[:, :, a_width:2 * a_width].reshape(bs, tn, A_HEADS, A_HEAD_DIM)
        v_new = zs3[:, :, 2 * a_width:3 * a_width].reshape(bs, tn, A_HEADS, A_HEAD_DIM)
        outs[6].append(jnp.concatenate([cache_a_k[l][:, tn:], k_new], axis=1))
        outs[7].append(jnp.concatenate([cache_a_v[l][:, tn:], v_new], axis=1))
        qkvb_s = zs3[:, :, 3 * a_width:3 * a_width + b_width3]
        outs[8].append(jnp.concatenate([state_b_conv[l], qkvb_s], axis=1)[:, tn:])
        outs[9].append(ssm_s)
    return (xp, xs) + tuple(jnp.stack(o) for o in outs)
```

```python
import functools

import jax
import jax.numpy as jnp
from jax import lax
from jax.experimental import pallas as pl
from jax.experimental.pallas import tpu as pltpu

F32 = jnp.float32
BF16 = jnp.bfloat16

CHUNK = 64
A_HEADS = 16
A_HEAD_DIM = 64
A_GROUP = 4
A_LEFT_CHUNKS = 8
A_REL_CLIP = 128
B_HEADS = 8
B_HEAD_DIM = 128
B_CONV = 4
C_HEADS = 4
C_HEAD_DIM = 256
N_BRANCH = 3
LN_EPS = 1e-5
RMS_EPS = 1e-6
L2_EPS = 1e-6
NEG_INF = -1e30

LANES = 128
SUBLANES = 8
VMEM_LIMIT_BYTES = 56 * 1024 * 1024


def _params(*sem):
    return pltpu.CompilerParams(dimension_semantics=sem, vmem_limit_bytes=VMEM_LIMIT_BYTES)


def _dot(a, b):
    return jnp.dot(a, b, preferred_element_type=F32)


def _dot_nt(a, b):
    return lax.dot_general(a, b, (((1,), (1,)), ((), ())), preferred_element_type=F32)


def _dot_tn(a, b):
    return lax.dot_general(a, b, (((0,), (0,)), ((), ())), preferred_element_type=F32)


def _sigmoid(x):
    return 0.5 + 0.5 * jnp.tanh(0.5 * x)


def _silu(x):
    hx = 0.5 * x
    return hx + hx * jnp.tanh(hx)


K_BLOCK, V_BLOCK, CONV_BLOCK0, CONV_BLOCKS = 1, 2, 3, 3


def _inproj_kernel(x_ref, w_ref, wab_ref, z_ref, ab_ref, xb_ref):
    @pl.when(pl.program_id(1) == 0)
    def _():
        xb = x_ref[...].astype(BF16)
        xb_ref[...] = xb
        ab_ref[...] = _dot(xb, wab_ref[...])

    z_ref[...] = _dot(xb_ref[...], w_ref[...]).astype(z_ref.dtype)


def _inproj(x2, w, wab, n_main, out_dtype, bm, bn):
    m, d = x2.shape
    return pl.pallas_call(
        _inproj_kernel,
        grid=(m // bm, n_main // bn),
        in_specs=[
            pl.BlockSpec((bm, d), lambda i, j: (i, 0)),
            pl.BlockSpec((d, bn), lambda i, j: (0, j)),
            pl.BlockSpec((d, LANES), lambda i, j: (0, 0)),
        ],
        out_specs=[
            pl.BlockSpec((bm, bn), lambda i, j: (i, j)),
            pl.BlockSpec((bm, LANES), lambda i, j: (i, 0)),
        ],
        out_shape=[
            jax.ShapeDtypeStruct((m, n_main), out_dtype),
            jax.ShapeDtypeStruct((m, LANES), F32),
        ],
        scratch_shapes=[pltpu.VMEM((bm, d), BF16)],
        compiler_params=_params("arbitrary", "arbitrary"),
        name="inproj",
    )(x2, w, wab)


INPROJ_SUB = 2 * B_HEAD_DIM


def _inproj_seq_kernel(x_ref, w_ref, wab_ref, z_ref, ab_ref, kt_ref, vt_ref, ct_ref, xb_ref):
    j = pl.program_id(1)
    bm, bn = z_ref.shape
    keep = kt_ref.shape[0]

    @pl.when(j == 0)
    def _():
        xb = x_ref[...].astype(BF16)
        xb_ref[...] = xb
        ab_ref[...] = _dot(xb, wab_ref[...])
        kt_ref[...] = jnp.zeros(kt_ref.shape, F32)
        vt_ref[...] = jnp.zeros(vt_ref.shape, F32)

    tails = []
    for n0 in range(0, bn, INPROJ_SUB):
        cols = slice(n0, n0 + INPROJ_SUB)
        acc = _dot(xb_ref[...], w_ref[:, cols])
        z_ref[:, cols] = acc.astype(z_ref.dtype)
        tail = acc[bm - keep:, :]
        kt_ref[:, cols] = jnp.where(j == K_BLOCK, tail, kt_ref[:, cols])
        vt_ref[:, cols] = jnp.where(j == V_BLOCK, tail, vt_ref[:, cols])
        tails.append(acc[bm - SUBLANES:, :])

    @pl.when((j >= CONV_BLOCK0) & (j < CONV_BLOCK0 + CONV_BLOCKS))
    def _():
        ct_ref[...] = jnp.concatenate(tails, axis=1)


def _inproj_seq(x3, w, wab, n_main, keep):
    nseq, t, d = x3.shape
    bm, bn = t, d
    conv_blk = lambda i, j: jnp.clip(j - CONV_BLOCK0, 0, CONV_BLOCKS - 1)
    return pl.pallas_call(
        _inproj_seq_kernel,
        grid=(nseq, n_main // bn),
        in_specs=[
            pl.BlockSpec((None, bm, d), lambda i, j: (i, 0, 0)),
            pl.BlockSpec((d, bn), lambda i, j: (0, j)),
            pl.BlockSpec((d, LANES), lambda i, j: (0, 0)),
        ],
        out_specs=[
            pl.BlockSpec((None, bm, bn), lambda i, j: (i, 0, j)),
            pl.BlockSpec((None, bm, LANES), lambda i, j: (i, 0, 0)),
            pl.BlockSpec((None, keep, bn), lambda i, j: (i, 0, 0)),
            pl.BlockSpec((None, keep, bn), lambda i, j: (i, 0, 0)),
            pl.BlockSpec((None, SUBLANES, bn), lambda i, j: (i, 0, conv_blk(i, j))),
        ],
        out_shape=[
            jax.ShapeDtypeStruct((nseq, t, n_main), BF16),
            jax.ShapeDtypeStruct((nseq, t, LANES), F32),
            jax.ShapeDtypeStruct((nseq, keep, bn), F32),
            jax.ShapeDtypeStruct((nseq, keep, bn), F32),
            jax.ShapeDtypeStruct((nseq, SUBLANES, CONV_BLOCKS * bn), F32),
        ],
        scratch_shapes=[pltpu.VMEM((bm, d), BF16)],
        compiler_params=_params("arbitrary", "arbitrary"),
        name="inproj_seq",
    )(x3, w, wab)


def _mem_kv_kernel(x_ref, w_ref, k_ref, v_ref):
    acc = _dot(x_ref[...].astype(BF16), w_ref[...])
    _, heads, dh = k_ref.shape
    for h in range(heads):
        k_ref[:, h, :] = acc[:, h * dh:(h + 1) * dh]
        v_ref[:, h, :] = acc[:, (heads + h) * dh:(heads + h + 1) * dh]


def _mem_kv(mem, w):
    bsz, mt, d = mem.shape
    out = jax.ShapeDtypeStruct((bsz, mt, C_HEADS, C_HEAD_DIM), F32)
    return pl.pallas_call(
        _mem_kv_kernel,
        grid=(bsz,),
        in_specs=[
            pl.BlockSpec((None, mt, d), lambda b: (b, 0, 0)),
            pl.BlockSpec(w.shape, lambda b: (0, 0)),
        ],
        out_specs=[pl.BlockSpec((None, mt, C_HEADS, C_HEAD_DIM), lambda b: (b, 0, 0, 0))] * 2,
        out_shape=[out, out],
        compiler_params=_params("arbitrary"),
        name="mem_kv",
    )(mem, w)


LOG2E = 1.4426950408889634


def _bias_col0(window):
    return (window - A_REL_CLIP + 1) // LANES * LANES


def _bias_table_kernel(rb_ref, o_ref, *, window, col0):
    h = pl.program_id(0)
    tq, cols = o_ref.shape
    i = lax.broadcasted_iota(jnp.int32, (tq, cols), 0)
    j = lax.broadcasted_iota(jnp.int32, (tq, cols), 1) + col0
    idx = jnp.clip(i - j + window, -A_REL_CLIP, A_REL_CLIP) + A_REL_CLIP
    lowest = max(-A_REL_CLIP, window - (col0 + cols - 1)) + A_REL_CLIP

    def body(r, acc):
        return jnp.where(idx == r, rb_ref[h, r], acc)

    tab = lax.fori_loop(lowest, 2 * A_REL_CLIP + 1, body, jnp.zeros((tq, cols), F32))
    o_ref[...] = (tab - rb_ref[h, 2 * A_REL_CLIP]) * LOG2E


def _bias_table(rel_bias, tq, band):
    heads = rel_bias.shape[0]
    window = band - tq
    col0 = _bias_col0(window)
    return pl.pallas_call(
        functools.partial(_bias_table_kernel, window=window, col0=col0),
        grid=(heads,),
        in_specs=[pl.BlockSpec(memory_space=pltpu.SMEM)],
        out_specs=pl.BlockSpec((None, tq, band - col0), lambda h: (h, 0, 0)),
        out_shape=jax.ShapeDtypeStruct((heads, tq, band - col0), F32),
        compiler_params=_params("arbitrary"),
        name="bias_table",
    )(rel_bias)


A_GROUP_WIDTH = A_GROUP * A_HEAD_DIM


def _lane_head(tq):
    return jnp.right_shift(lax.broadcasted_iota(jnp.int32, (tq, A_GROUP_WIDTH), 1), A_HEAD_DIM.bit_length() - 1)


def _attn_scores(qg, kg, bias_g, valid_from):
    lane_head = _lane_head(qg.shape[0])
    qg = qg.astype(BF16)
    zero = jnp.zeros_like(qg)
    qm = jnp.concatenate([jnp.where(lane_head == h, qg, zero) for h in range(A_GROUP)], axis=0)
    s = _dot_nt(qm, kg)
    col0 = s.shape[1] - bias_g.shape[1]
    s = jnp.concatenate([s[:, 0:col0], s[:, col0:] + bias_g], axis=1)
    if valid_from is not None:
        col = lax.broadcasted_iota(jnp.int32, s.shape, 1)
        s = jnp.where(col >= valid_from, s, NEG_INF)
    m = jnp.max(s, axis=-1, keepdims=True)
    p = jnp.exp2(s - m)
    return p.astype(BF16), 1.0 / jnp.sum(p, axis=-1, keepdims=True)


def _attn_values(p, rl, vg):
    tq = p.shape[0] // A_GROUP
    o = _dot(p, vg)
    if rl.shape[1] == 1:
        o = o * rl
    else:
        o = o * jnp.concatenate([rl] * (A_GROUP_WIDTH // LANES), axis=1)
    lane_head = _lane_head(tq)
    out = jnp.zeros((tq, A_GROUP_WIDTH), F32)
    for h in range(A_GROUP):
        out = jnp.where(lane_head == h, o[h * tq:(h + 1) * tq], out)
    return out


A_CHUNKS_PER_STEP = 4


def _attn_prompt_kernel(q_ref, k_ref, v_ref, bias_ref, o_ref, kpad_ref, vpad_ref, *, window, tq):
    s = pl.program_id(1)
    band = window + tq
    gw = A_GROUP_WIDTH
    cps = q_ref.shape[0] // tq

    @pl.when(s == 0)
    def _():
        zeros = jnp.zeros((window, kpad_ref.shape[1]), BF16)
        kpad_ref[0:window, :] = zeros
        vpad_ref[0:window, :] = zeros
        kpad_ref[window:, :] = k_ref[...]
        vpad_ref[window:, :] = v_ref[...]

    for j in range(cps):
        c = s * cps + j
        start = pl.multiple_of(c * tq, tq)
        rows = slice(j * tq, (j + 1) * tq)
        for g in range(A_HEADS // A_GROUP):
            cols = slice(g * gw, (g + 1) * gw)
            p, rl = _attn_scores(q_ref[rows, cols], kpad_ref[pl.ds(start, band), cols], bias_ref[g],
                                 window - c * tq)
            o_ref[rows, cols] = _attn_values(p, rl, vpad_ref[pl.ds(start, band), cols]).astype(o_ref.dtype)


def _attn_prompt(z3, bias_g, tq, window):
    bsz, t, _ = z3.shape
    width = A_HEADS * A_HEAD_DIM
    rows = tq * min(A_CHUNKS_PER_STEP, t // tq)
    return pl.pallas_call(
        functools.partial(_attn_prompt_kernel, window=window, tq=tq),
        grid=(bsz, t // rows),
        in_specs=[
            pl.BlockSpec((None, rows, width), lambda b, s: (b, s, 0)),
            pl.BlockSpec((None, t, width), lambda b, s: (b, 0, K_BLOCK)),
            pl.BlockSpec((None, t, width), lambda b, s: (b, 0, V_BLOCK)),
            pl.BlockSpec(bias_g.shape, lambda b, s: (0, 0, 0)),
        ],
        out_specs=pl.BlockSpec((None, rows, width), lambda b, s: (b, s, 0)),
        out_shape=jax.ShapeDtypeStruct((bsz, t, width), BF16),
        scratch_shapes=[pltpu.VMEM((t + window, width), BF16), pltpu.VMEM((t + window, width), BF16)],
        compiler_params=_params("arbitrary", "arbitrary"),
        name="attn_prompt",
    )(z3, z3, z3, bias_g)


def _attn_sample_kernel(q_ref, kn_ref, vn_ref, kt_ref, vt_ref, bias_ref, o_ref):
    tq = q_ref.shape[0]
    lc = kt_ref.shape[1]
    gw = A_GROUP_WIDTH
    lane_head = _lane_head(tq)
    for g in range(A_HEADS // A_GROUP):
        cols = slice(g * gw, (g + 1) * gw)
        qg = q_ref[:, cols].astype(BF16)
        zero = jnp.zeros_like(qg)
        qm = jnp.concatenate([jnp.where(lane_head == h, qg, zero) for h in range(A_GROUP)], axis=0)
        s_c = _dot(qm, kt_ref[cols, :].astype(BF16))
        s_n = _dot_nt(qm, kn_ref[:, cols].astype(BF16))
        bias = bias_ref[g]
        col0 = lc + tq - bias.shape[1]
        s_c = jnp.concatenate([s_c[:, 0:col0], s_c[:, col0:] + bias[:, 0:lc - col0]], axis=1)
        s_n = s_n + bias[:, lc - col0:]
        m = jnp.maximum(jnp.max(s_c, axis=-1, keepdims=True), jnp.max(s_n, axis=-1, keepdims=True))
        p_c = jnp.exp2(s_c - m)
        p_n = jnp.exp2(s_n - m)
        rl = 1.0 / (jnp.sum(p_c, axis=-1, keepdims=True) + jnp.sum(p_n, axis=-1, keepdims=True))
        o = (_dot_nt(p_c.astype(BF16), vt_ref[cols, :].astype(BF16))
             + _dot(p_n.astype(BF16), vn_ref[:, cols].astype(BF16))) * rl
        out = jnp.zeros((tq, gw), F32)
        for h in range(A_GROUP):
            out = jnp.where(lane_head == h, o[h * tq:(h + 1) * tq], out)
        o_ref[:, cols] = out.astype(o_ref.dtype)


def _attn_sample(z3, kt, vt, bias_g):
    bsz, tq, _ = z3.shape
    width, lc = kt.shape[1:]
    return pl.pallas_call(
        _attn_sample_kernel,
        grid=(bsz,),
        in_specs=[
            pl.BlockSpec((None, tq, width), lambda b: (b, 0, 0)),
            pl.BlockSpec((None, tq, width), lambda b: (b, 0, K_BLOCK)),
            pl.BlockSpec((None, tq, width), lambda b: (b, 0, V_BLOCK)),
            pl.BlockSpec((None, width, lc), lambda b: (b, 0, 0)),
            pl.BlockSpec((None, width, lc), lambda b: (b, 0, 0)),
            pl.BlockSpec(bias_g.shape, lambda b: (0, 0, 0)),
        ],
        out_specs=pl.BlockSpec((None, tq, width), lambda b: (b, 0, 0)),
        out_shape=jax.ShapeDtypeStruct((bsz, tq, width), BF16),
        compiler_params=_params("arbitrary"),
        name="attn_sample",
    )(z3, z3, z3, kt, vt, bias_g)


def _gdn_kernel(x_ref, ab_ref, conv0_ref, s0_ref, cw_ref, pcol_ref, ng_ref,
                o_ref, s_ref, xbuf_ref):
    c = pl.program_id(1)
    nb, ch, _ = x_ref.shape
    hd = B_HEAD_DIM
    width = B_HEADS * hd
    keep = B_CONV - 1
    shift_on_mxu = x_ref.dtype == BF16
    top = xbuf_ref.shape[1] - (0 if shift_on_mxu else ch)
    chains = [(bi, h) for bi in range(nb) for h in range(B_HEADS)]
    every = range(len(chains))

    @pl.when(c == 0)
    def _():
        hist = jnp.concatenate([jnp.zeros((nb, top - keep, xbuf_ref.shape[2]), F32),
                                conv0_ref[...].astype(F32)], axis=1)
        xbuf_ref[:, 0:top, :] = hist.astype(xbuf_ref.dtype)
        s_ref[...] = s0_ref[...].astype(F32)

    if shift_on_mxu:
        t_r = lax.broadcasted_iota(jnp.int32, (ch, top + ch), 0)
        u_c = lax.broadcasted_iota(jnp.int32, (ch, top + ch), 1)
        shift01 = jnp.concatenate([jnp.where(u_c == top + t_r - (keep - i), 1.0, 0.0) for i in range(keep)],
                                  axis=0).astype(BF16)
        shifted = [_dot(shift01, jnp.concatenate([xbuf_ref[bi], x_ref[bi]], axis=0)) for bi in range(nb)]
    else:
        xbuf_ref[:, top:top + ch, :] = x_ref[...]

    t_i = lax.broadcasted_iota(jnp.int32, (ch, ch), 0)
    s_i = lax.broadcasted_iota(jnp.int32, (ch, ch), 1)
    tri_incl = (t_i >= s_i)
    tri_strict = (t_i > s_i)
    eye = jnp.where(t_i == s_i, 1.0, 0.0)
    lane = lax.broadcasted_iota(jnp.int32, (ch, LANES), 1)
    lane_row = lax.broadcasted_iota(jnp.int32, (1, ch), 1)
    frame = lax.broadcasted_iota(jnp.int32, (ch, LANES), 0)

    gcum_col, beta_col, gcum_row = [], [], []
    for bi in range(nb):
        ab = ab_ref[bi]
        g = -jnp.exp(pcol_ref[0:1, :]) * jax.nn.softplus(ab + pcol_ref[1:2, :])
        step = 1
        while step < ch:
            g = g + jnp.where(frame >= step, pltpu.roll(g, step, axis=0), 0.0)
            step *= 2
        beta_col.append(_sigmoid(ab))
        gcum_col.append(g)
        gcum_row.append(g.T)

    def pick(x, idx):
        return jnp.sum(jnp.where(lane == idx, x, 0.0), axis=-1, keepdims=True)

    def conv_silu(bi, col0):
        cols = slice(col0, col0 + hd)
        if shift_on_mxu:
            acc = x_ref[bi, :, cols].astype(F32) * cw_ref[keep:keep + 1, cols]
            for i in range(keep):
                acc = acc + shifted[bi][i * ch:(i + 1) * ch, cols] * cw_ref[i:i + 1, cols]
        else:
            acc = xbuf_ref[bi, top:top + ch, cols] * cw_ref[keep:keep + 1, cols]
            for i in range(keep):
                acc = acc + xbuf_ref[bi, top - keep + i:top - keep + i + ch, cols] * cw_ref[i:i + 1, cols]
        return _silu(acc)

    def l2norm(x):
        return x * lax.rsqrt(jnp.sum(x * x, axis=-1, keepdims=True) + L2_EPS)

    q = [l2norm(conv_silu(bi, h * hd)) * (hd ** -0.5) for bi, h in chains]
    k = [l2norm(conv_silu(bi, width + h * hd)) for bi, h in chains]
    v = [conv_silu(bi, 2 * width + h * hd) for bi, h in chains]
    gc = [pick(gcum_col[bi], h) for bi, h in chains]
    bc = [pick(beta_col[bi], B_HEADS + h) for bi, h in chains]
    gr = [gcum_row[bi][h:h + 1, :] for bi, h in chains]
    glast = [jnp.sum(jnp.where(lane_row == ch - 1, gr[i], 0.0), axis=-1, keepdims=True) for i in every]
    dec_incl = [jnp.exp(jnp.where(tri_incl, gc[i] - gr[i], NEG_INF)) for i in every]
    eg = [jnp.exp(gc[i]) for i in every]
    nbc = [-bc[i] for i in every]

    qk_kk = [_dot_nt(jnp.concatenate([q[i], k[i]], axis=0).astype(BF16), k[i].astype(BF16)) for i in every]
    a_qk = [qk_kk[i][0:ch] * dec_incl[i] for i in every]

    tp = [jnp.concatenate([(nbc[i] * qk_kk[i][ch:2 * ch]) * jnp.where(tri_strict, dec_incl[i], 0.0), eye], axis=1)
          for i in every]
    t_half = lax.broadcasted_iota(jnp.int32, (ch, 2 * ch), 1) >= ch
    for _ in range(max(1, (ch - 1).bit_length())):
        tp_b = [tp[i].astype(BF16) for i in every]
        tp = [_dot(tp_b[i][:, 0:ch], tp_b[i]) + jnp.where(t_half, tp[i], 0.0) for i in every]
    rhs = [jnp.concatenate([bc[i] * v[i], (bc[i] * eg[i]) * k[i]], axis=1).astype(BF16) for i in every]
    x_sol = [_dot(tp[i][:, ch:2 * ch].astype(BF16), rhs[i]) for i in every]

    state = [s_ref[bi, h] for bi, h in chains]
    wq = [_dot(jnp.concatenate([x_sol[i][:, hd:2 * hd], q[i] * eg[i]], axis=0).astype(BF16),
               state[i].astype(BF16)) for i in every]
    wv_b = [(x_sol[i][:, 0:hd] - wq[i][0:ch]).astype(BF16) for i in every]
    o = [wq[i][ch:2 * ch] + _dot(a_qk[i].astype(BF16), wv_b[i]) for i in every]
    s_new = [jnp.exp(glast[i]) * state[i]
             + _dot_tn((k[i] * jnp.exp(glast[i] - gc[i])).astype(BF16), wv_b[i]) for i in every]
    for i, (bi, h) in enumerate(chains):
        s_ref[bi, h] = s_new[i]
        on = o[i] * lax.rsqrt(jnp.mean(o[i] * o[i], axis=-1, keepdims=True) + RMS_EPS) * ng_ref[...]
        o_ref[bi, :, h * hd:(h + 1) * hd] = on.astype(o_ref.dtype)

    if shift_on_mxu:
        xbuf_ref[...] = x_ref[:, ch - top:ch, :]
    else:
        xbuf_ref[:, top - keep:top, :] = xbuf_ref[:, top + ch - keep:top + ch, :]


GDN_BATCHES = 2


def _gdn(z3, ab3, conv0, s0, conv_w, pcol, norm_g, ch):
    bsz, t, _ = z3.shape
    nb = GDN_BATCHES
    width3 = 3 * B_HEADS * B_HEAD_DIM
    width = B_HEADS * B_HEAD_DIM
    if z3.dtype == BF16:
        xbuf = pltpu.VMEM((nb, 2 * SUBLANES, width3), BF16)
    else:
        xbuf = pltpu.VMEM((nb, SUBLANES + ch, width3), F32)
    return pl.pallas_call(
        _gdn_kernel,
        grid=(bsz // nb, t // ch),
        in_specs=[
            pl.BlockSpec((nb, ch, width3), lambda b, c: (b, c, 1)),
            pl.BlockSpec((nb, ch, LANES), lambda b, c: (b, c, 0)),
            pl.BlockSpec((nb, B_CONV - 1, width3), lambda b, c: (b, 0, 0)),
            pl.BlockSpec((nb, B_HEADS, B_HEAD_DIM, B_HEAD_DIM), lambda b, c: (b, 0, 0, 0)),
            pl.BlockSpec(conv_w.shape, lambda b, c: (0, 0)),
            pl.BlockSpec(pcol.shape, lambda b, c: (0, 0)),
            pl.BlockSpec(norm_g.shape, lambda b, c: (0, 0)),
        ],
        out_specs=[
            pl.BlockSpec((nb, ch, width), lambda b, c: (b, c, 0)),
            pl.BlockSpec((nb, B_HEADS, B_HEAD_DIM, B_HEAD_DIM), lambda b, c: (b, 0, 0, 0)),
        ],
        out_shape=[
            jax.ShapeDtypeStruct((bsz, t, width), BF16),
            jax.ShapeDtypeStruct((bsz, B_HEADS, B_HEAD_DIM, B_HEAD_DIM), F32),
        ],
        scratch_shapes=[xbuf],
        compiler_params=_params("arbitrary", "arbitrary"),
        name="gdn",
    )(z3, ab3, conv0, s0, conv_w, pcol, norm_g)


MEMATTN_ROWS = 512


def _memattn_kernel(q_ref, mk_ref, mv_ref, o_ref, kb_ref, vb_ref):
    dh = C_HEAD_DIM

    @pl.when(pl.program_id(1) == 0)
    def _():
        for h in range(C_HEADS):
            kb_ref[:, h * dh:(h + 1) * dh] = mk_ref[:, h, :].astype(BF16)
            vb_ref[:, h * dh:(h + 1) * dh] = mv_ref[:, h, :].astype(BF16)

    tq = q_ref.shape[0]
    rb = min(MEMATTN_ROWS, tq)
    for r0 in range(0, tq, rb):
        for h in range(C_HEADS):
            cols = slice(h * dh, (h + 1) * dh)
            q = q_ref[r0:r0 + rb, cols].astype(BF16)
            s = _dot_nt(q, kb_ref[:, cols]) * (dh ** -0.5)
            m = jnp.max(s, axis=-1, keepdims=True)
            p = jnp.exp(s - m)
            l = jnp.sum(p, axis=-1, keepdims=True)
            o = _dot(p.astype(BF16), vb_ref[:, cols]) * (1.0 / l)
            o_ref[r0:r0 + rb, cols] = o.astype(o_ref.dtype)


def _memattn(z3, q_blk, mk, mv, tq):
    bsz, t, _ = z3.shape
    mem = mk.shape[1]
    width = C_HEADS * C_HEAD_DIM
    kv_spec = pl.BlockSpec((None, mem, C_HEADS, C_HEAD_DIM), lambda b, i: (b, 0, 0, 0))
    return pl.pallas_call(
        _memattn_kernel,
        grid=(bsz, t // tq),
        in_specs=[
            pl.BlockSpec((None, tq, width), lambda b, i: (b, i, q_blk)),
            kv_spec,
            kv_spec,
        ],
        out_specs=pl.BlockSpec((None, tq, width), lambda b, i: (b, i, 0)),
        out_shape=jax.ShapeDtypeStruct((bsz, t, width), BF16),
        scratch_shapes=[pltpu.VMEM((mem, width), BF16), pltpu.VMEM((mem, width), BF16)],
        compiler_params=_params("arbitrary", "arbitrary"),
        name="memattn",
    )(z3, mk, mv)


def _layer_norm(x, g, b):
    mu = jnp.mean(x, axis=-1, keepdims=True)
    xc = x - mu
    var = jnp.mean(xc * xc, axis=-1, keepdims=True)
    return xc * lax.rsqrt(var + LN_EPS) * g + b


FFN_PARTS = 2
FFN_HIDDEN_BLOCK = 1024


def _ffn_kernel(x_ref, ga_ref, gb_ref, gc_ref, oa_ref, ob_ref, oc_ref, wo_ref, w1_ref, w2_ref,
                vec_ref, b1_ref, y_ref, *, alpha, ff_blk):
    bm = x_ref.shape[0]
    rows = [pl.ds(r * (bm // FFN_PARTS), bm // FFN_PARTS) for r in range(FFN_PARTS)]
    parts = range(FFN_PARTS)

    def gated(g_ref, o_ref, r):
        return _sigmoid(g_ref[r, :].astype(F32)) * o_ref[r, :].astype(F32)

    merged = [(gated(ga_ref, oa_ref, r) + gated(gb_ref, ob_ref, r) + gated(gc_ref, oc_ref, r)).astype(BF16)
              for r in rows]
    proj = [_dot(merged[i], wo_ref[...]) for i in parts]
    h = [_layer_norm(alpha * x_ref[rows[i], :] + proj[i], vec_ref[0:1, :], vec_ref[1:2, :]) for i in parts]
    hb = [h[i].astype(BF16) for i in parts]
    acc = [None] * FFN_PARTS
    for k0 in range(0, w1_ref.shape[1], ff_blk):
        f = [_dot(hb[i], w1_ref[:, k0:k0 + ff_blk]) + b1_ref[:, k0:k0 + ff_blk] for i in parts]
        f = [jnp.square(jnp.maximum(f[i], 0.0)).astype(BF16) for i in parts]
        d = [_dot(f[i], w2_ref[k0:k0 + ff_blk, :]) for i in parts]
        acc = [d[i] if acc[i] is None else acc[i] + d[i] for i in parts]
    for i in parts:
        y_ref[rows[i], :] = _layer_norm(alpha * h[i] + acc[i] + vec_ref[2:3, :], vec_ref[3:4, :], vec_ref[4:5, :])


def _ffn(x2, z2, gate_blk0, oa, ob, oc, wo, w1, w2, vec, b1, alpha, bm):
    m, d = x2.shape
    dff = w1.shape[1]
    const = dict(pipeline_mode=pl.Buffered(1))
    row = lambda i: (i, 0)
    return pl.pallas_call(
        functools.partial(_ffn_kernel, alpha=alpha, ff_blk=FFN_HIDDEN_BLOCK),
        grid=(m // bm,),
        in_specs=[
            pl.BlockSpec((bm, d), row),
            pl.BlockSpec((bm, d), lambda i: (i, gate_blk0)),
            pl.BlockSpec((bm, d), lambda i: (i, gate_blk0 + 1)),
            pl.BlockSpec((bm, d), lambda i: (i, gate_blk0 + 2)),
            pl.BlockSpec((bm, d), row),
            pl.BlockSpec((bm, d), row),
            pl.BlockSpec((bm, d), row),
            pl.BlockSpec((d, d), lambda i: (0, 0), **const),
            pl.BlockSpec((d, dff), lambda i: (0, 0), **const),
            pl.BlockSpec((dff, d), lambda i: (0, 0), **const),
            pl.BlockSpec(vec.shape, lambda i: (0, 0), **const),
            pl.BlockSpec(b1.shape, lambda i: (0, 0), **const),
        ],
        out_specs=pl.BlockSpec((bm, d), row),
        out_shape=jax.ShapeDtypeStruct((m, d), F32),
        compiler_params=_params("arbitrary"),
        name="merge_ffn",
    )(x2, z2, z2, z2, oa, ob, oc, wo, w1, w2, vec, b1)


QC_BLOCK, GATE_BLOCK0 = 6, 7
INPROJ_ROWS = 2048


def _group_layer(x3, z_dtype, attn_fn, conv0, ssm0, mem_k, mem_v, lw, keep=None):
    bsz, t, d = x3.shape
    m = bsz * t
    ch = min(CHUNK, t)
    x2 = x3.reshape(m, d)
    if keep is not None:
        z3, ab3, *tails = _inproj_seq(x3, lw["w_in"], lw["wab"], lw["n_main"], keep)
        z2 = z3.reshape(m, -1)
    else:
        z2, ab = _inproj(x2, lw["w_in"], lw["wab"], lw["n_main"], z_dtype, min(INPROJ_ROWS, m), d)
        z3, ab3, tails = z2.reshape(bsz, t, -1), ab.reshape(bsz, t, LANES), None
    o_a = attn_fn(z3)
    o_b, ssm = _gdn(z3, ab3, conv0, ssm0, lw["conv_w"], lw["pcol"], lw["norm_g"], ch)
    o_c = _memattn(z3, QC_BLOCK, mem_k, mem_v, min(2048, t))
    y = _ffn(x2, z2, GATE_BLOCK0, o_a.reshape(m, d), o_b.reshape(m, d), o_c.reshape(m, d),
             lw["wo"], lw["w1"], lw["w2"], lw["vec"], lw["b1"], lw["alpha"], min(512, m))
    return y.reshape(bsz, t, d), z3, ssm, tails


def kernel(x_prompt, x_sample, cache_a_k, cache_a_v, state_b_conv, state_b_ssm, cache_mem_k, cache_mem_v, mem_prompt, w_in, w_b_conv, b_a_log, b_dt_bias, b_norm_g, a_rel_bias, w_mem_kv, w_out, ln1_g, ln1_b, w_ff1, b_ff1, w_ff2, b_ff2, ln2_g, ln2_b):
    depth = w_in.shape[0]
    bp, tp, d = x_prompt.shape
    bs, tn, _ = x_sample.shape
    window = A_LEFT_CHUNKS * CHUNK
    keep_p = min(window, tp)
    lc = cache_a_k.shape[2]
    n_main = w_in.shape[2] - 2 * B_HEADS
    a_width = A_HEADS * A_HEAD_DIM
    b_width3 = 3 * B_HEADS * B_HEAD_DIM
    alpha = (2.0 * depth) ** 0.25
    xp, xs = x_prompt, x_sample
    outs = [[] for _ in range(10)]
    for l in range(depth):
        q_scale = jnp.where(jnp.arange(w_in.shape[2]) < a_width, (A_HEAD_DIM ** -0.5) * LOG2E, 1.0)
        w_in_b = (w_in[l] * q_scale).astype(BF16)
        vec = jnp.zeros((SUBLANES, d), F32)
        vec = vec.at[0].set(ln1_g[l]).at[1].set(ln1_b[l]).at[2].set(b_ff2[l]).at[3].set(ln2_g[l]).at[4].set(ln2_b[l])
        pcol = jnp.zeros((SUBLANES, LANES), F32)
        pcol = pcol.at[0, 0:B_HEADS].set(b_a_log[l]).at[1, 0:B_HEADS].set(b_dt_bias[l])
        lw = dict(
            w_in=w_in_b, n_main=n_main,
            wab=jnp.pad(w_in_b[:, n_main:], ((0, 0), (0, LANES - 2 * B_HEADS))),
            conv_w=w_b_conv[l],
            pcol=pcol,
            norm_g=b_norm_g[l].reshape(1, B_HEAD_DIM),
            wo=w_out[l].astype(BF16), w1=w_ff1[l].astype(BF16), w2=w_ff2[l].astype(BF16),
            vec=vec, b1=b_ff1[l].reshape(1, -1), alpha=alpha,
        )
        band = window + CHUNK
        bias_tab = _bias_table(a_rel_bias[l], CHUNK, band)
        bias_cols = bias_tab.shape[2]
        col0 = band - bias_cols

        bias_p = bias_tab.reshape(A_HEADS // A_GROUP, A_GROUP * CHUNK, bias_cols)
        mk, mv = _mem_kv(mem_prompt, w_mem_kv[l].astype(BF16))
        conv0 = jnp.zeros((bp, B_CONV - 1, b_width3), F32)
        ssm0 = jnp.zeros((bp, B_HEADS, B_HEAD_DIM, B_HEAD_DIM), F32)
        xp, _, ssm_p, (k_tail, v_tail, conv_tail) = _group_layer(
            xp, BF16, lambda z3: _attn_prompt(z3, bias_p, CHUNK, window), conv0, ssm0, mk, mv, lw, keep=keep_p)
        outs[0].append(k_tail.reshape(bp, keep_p, A_HEADS, A_HEAD_DIM))
        outs[1].append(v_tail.reshape(bp, keep_p, A_HEADS, A_HEAD_DIM))
        outs[2].append(conv_tail[:, SUBLANES - (B_CONV - 1):])
        outs[3].append(ssm_p)
        outs[4].append(mk)
        outs[5].append(mv)

        bias_s = bias_tab[:, 0:tn, 0:lc + tn - col0].reshape(A_HEADS // A_GROUP, A_GROUP * tn, lc + tn - col0)
        kt = cache_a_k[l].transpose(0, 2, 3, 1).reshape(bs, a_width, lc)
        vt = cache_a_v[l].transpose(0, 2, 3, 1).reshape(bs, a_width, lc)
        xs, zs3, ssm_s, _ = _group_layer(
            xs, F32, lambda z3: _attn_sample(z3, kt, vt, bias_s), state_b_conv[l], state_b_ssm[l],
            cache_mem_k[l], cache_mem_v[l], lw)
        k_new = zs3[:, :, a_width:2 * a_width].reshape(bs, tn, A_HEADS, A_HEAD_DIM)
        v_new = zs3[:, :, 2 * a_width:3 * a_width].reshape(bs, tn, A_HEADS, A_HEAD_DIM)
        outs[6].append(jnp.concatenate([cache_a_k[l][:, tn:], k_new], axis=1))
        outs[7].append(jnp.concatenate([cache_a_v[l][:, tn:], v_new], axis=1))
        qkvb_s = zs3[:, :, 3 * a_width:3 * a_width + b_width3]
        outs[8].append(jnp.concatenate([state_b_conv[l], qkvb_s], axis=1)[:, tn:])
        outs[9].append(ssm_s)
    return (xp, xs) + tuple(jnp.stack(o) for o in outs)
```

```python
import functools

import jax
import jax.numpy as jnp
from jax import lax
from jax.experimental import pallas as pl
from jax.experimental.pallas import tpu as pltpu

F32 = jnp.float32
BF16 = jnp.bfloat16

CHUNK = 64
A_HEADS = 16
A_HEAD_DIM = 64
A_GROUP = 4
A_LEFT_CHUNKS = 8
A_REL_CLIP = 128
B_HEADS = 8
B_HEAD_DIM = 128
B_CONV = 4
C_HEADS = 4
C_HEAD_DIM = 256
N_BRANCH = 3
LN_EPS = 1e-5
RMS_EPS = 1e-6
L2_EPS = 1e-6
NEG_INF = -1e30

LANES = 128
SUBLANES = 8
VMEM_LIMIT_BYTES = 56 * 1024 * 1024


def _params(*sem):
    return pltpu.CompilerParams(dimension_semantics=sem, vmem_limit_bytes=VMEM_LIMIT_BYTES)


def _dot(a, b):
    return jnp.dot(a, b, preferred_element_type=F32)


def _dot_nt(a, b):
    return lax.dot_general(a, b, (((1,), (1,)), ((), ())), preferred_element_type=F32)


def _dot_tn(a, b):
    return lax.dot_general(a, b, (((0,), (0,)), ((), ())), preferred_element_type=F32)


def _sigmoid(x):
    return 0.5 + 0.5 * jnp.tanh(0.5 * x)


def _silu(x):
    hx = 0.5 * x
    return hx + hx * jnp.tanh(hx)


def _scale_cast_kernel(w_ref, s_ref, o_ref):
    o_ref[...] = (w_ref[...] * s_ref[...]).astype(o_ref.dtype)


def _scale_cast(w, layer, col_scale, bn):
    _, k, n = w.shape
    return pl.pallas_call(
        _scale_cast_kernel,
        grid=(pl.cdiv(n, bn),),
        in_specs=[
            pl.BlockSpec((None, k, bn), lambda j: (layer, 0, j)),
            pl.BlockSpec((1, bn), lambda j: (0, j)),
        ],
        out_specs=pl.BlockSpec((k, bn), lambda j: (0, j)),
        out_shape=jax.ShapeDtypeStruct((k, n), BF16),
        compiler_params=_params("arbitrary"),
        name="scale_cast",
    )(w, col_scale)


K_BLOCK, V_BLOCK, CONV_BLOCK0, CONV_BLOCKS = 1, 2, 3, 3


def _inproj_kernel(x_ref, w_ref, wab_ref, z_ref, ab_ref, xb_ref):
    @pl.when(pl.program_id(1) == 0)
    def _():
        xb = x_ref[...].astype(BF16)
        xb_ref[...] = xb
        ab_ref[...] = _dot(xb, wab_ref[...])

    z_ref[...] = _dot(xb_ref[...], w_ref[...]).astype(z_ref.dtype)


def _inproj(x2, w, wab, n_main, out_dtype, bm, bn):
    m, d = x2.shape
    return pl.pallas_call(
        _inproj_kernel,
        grid=(m // bm, n_main // bn),
        in_specs=[
            pl.BlockSpec((bm, d), lambda i, j: (i, 0)),
            pl.BlockSpec((d, bn), lambda i, j: (0, j)),
            pl.BlockSpec((d, LANES), lambda i, j: (0, 0)),
        ],
        out_specs=[
            pl.BlockSpec((bm, bn), lambda i, j: (i, j)),
            pl.BlockSpec((bm, LANES), lambda i, j: (i, 0)),
        ],
        out_shape=[
            jax.ShapeDtypeStruct((m, n_main), out_dtype),
            jax.ShapeDtypeStruct((m, LANES), F32),
        ],
        scratch_shapes=[pltpu.VMEM((bm, d), BF16)],
        compiler_params=_params("arbitrary", "arbitrary"),
        name="inproj",
    )(x2, w, wab)


INPROJ_SUB = 2 * B_HEAD_DIM


def _inproj_seq_kernel(x_ref, w_ref, wab_ref, z_ref, ab_ref, kt_ref, vt_ref, ct_ref, xb_ref):
    j = pl.program_id(1)
    bm, bn = z_ref.shape
    keep = kt_ref.shape[0]

    @pl.when(j == 0)
    def _():
        xb = x_ref[...].astype(BF16)
        xb_ref[...] = xb
        ab_ref[...] = _dot(xb, wab_ref[...])
        kt_ref[...] = jnp.zeros(kt_ref.shape, F32)
        vt_ref[...] = jnp.zeros(vt_ref.shape, F32)

    tails = []
    for n0 in range(0, bn, INPROJ_SUB):
        cols = slice(n0, n0 + INPROJ_SUB)
        acc = _dot(xb_ref[...], w_ref[:, cols])
        z_ref[:, cols] = acc.astype(z_ref.dtype)
        tail = acc[bm - keep:, :]
        kt_ref[:, cols] = jnp.where(j == K_BLOCK, tail, kt_ref[:, cols])
        vt_ref[:, cols] = jnp.where(j == V_BLOCK, tail, vt_ref[:, cols])
        tails.append(acc[bm - SUBLANES:, :])

    @pl.when((j >= CONV_BLOCK0) & (j < CONV_BLOCK0 + CONV_BLOCKS))
    def _():
        ct_ref[...] = jnp.concatenate(tails, axis=1)


def _inproj_seq(x3, w, wab, n_main, keep):
    nseq, t, d = x3.shape
    bm, bn = t, d
    conv_blk = lambda i, j: jnp.clip(j - CONV_BLOCK0, 0, CONV_BLOCKS - 1)
    return pl.pallas_call(
        _inproj_seq_kernel,
        grid=(nseq, n_main // bn),
        in_specs=[
            pl.BlockSpec((None, bm, d), lambda i, j: (i, 0, 0)),
            pl.BlockSpec((d, bn), lambda i, j: (0, j)),
            pl.BlockSpec((d, LANES), lambda i, j: (0, 0)),
        ],
        out_specs=[
            pl.BlockSpec((None, bm, bn), lambda i, j: (i, 0, j)),
            pl.BlockSpec((None, bm, LANES), lambda i, j: (i, 0, 0)),
            pl.BlockSpec((None, keep, bn), lambda i, j: (i, 0, 0)),
            pl.BlockSpec((None, keep, bn), lambda i, j: (i, 0, 0)),
            pl.BlockSpec((None, SUBLANES, bn), lambda i, j: (i, 0, conv_blk(i, j))),
        ],
        out_shape=[
            jax.ShapeDtypeStruct((nseq, t, n_main), BF16),
            jax.ShapeDtypeStruct((nseq, t, LANES), F32),
            jax.ShapeDtypeStruct((nseq, keep, bn), F32),
            jax.ShapeDtypeStruct((nseq, keep, bn), F32),
            jax.ShapeDtypeStruct((nseq, SUBLANES, CONV_BLOCKS * bn), F32),
        ],
        scratch_shapes=[pltpu.VMEM((bm, d), BF16)],
        compiler_params=_params("arbitrary", "arbitrary"),
        name="inproj_seq",
    )(x3, w, wab)


def _mem_kv_kernel(x_ref, w_ref, k_ref, v_ref):
    acc = _dot(x_ref[...].astype(BF16), w_ref[...])
    _, heads, dh = k_ref.shape
    for h in range(heads):
        k_ref[:, h, :] = acc[:, h * dh:(h + 1) * dh]
        v_ref[:, h, :] = acc[:, (heads + h) * dh:(heads + h + 1) * dh]


def _mem_kv(mem, w):
    bsz, mt, d = mem.shape
    out = jax.ShapeDtypeStruct((bsz, mt, C_HEADS, C_HEAD_DIM), F32)
    return pl.pallas_call(
        _mem_kv_kernel,
        grid=(bsz,),
        in_specs=[
            pl.BlockSpec((None, mt, d), lambda b: (b, 0, 0)),
            pl.BlockSpec(w.shape, lambda b: (0, 0)),
        ],
        out_specs=[pl.BlockSpec((None, mt, C_HEADS, C_HEAD_DIM), lambda b: (b, 0, 0, 0))] * 2,
        out_shape=[out, out],
        compiler_params=_params("arbitrary"),
        name="mem_kv",
    )(mem, w)


LOG2E = 1.4426950408889634


def _bias_col0(window):
    return (window - A_REL_CLIP + 1) // LANES * LANES


def _bias_table_kernel(rev_ref, o_ref):
    heads, tq, cols = o_ref.shape
    width = rev_ref.shape[1]
    lane = lax.broadcasted_iota(jnp.int32, (heads, width), 1)
    w = jnp.where(lane < cols, (rev_ref[...] - rev_ref[:, 0:1]) * LOG2E, 0.0)
    for h in range(heads):
        rows = jnp.broadcast_to(w[h:h + 1, :], (tq, width))
        o_ref[h] = pltpu.roll(rows, 0, axis=1, stride=1, stride_axis=0)[:, 0:cols]


def _bias_table(rel_bias, tq, band):
    heads = rel_bias.shape[0]
    window = band - tq
    col0 = _bias_col0(window)
    cols = band - col0
    width = 2 * A_REL_CLIP
    assert window - col0 == A_REL_CLIP and window - (band - 1) >= -A_REL_CLIP
    assert cols + tq <= width and width % LANES == 0
    rev = rel_bias[:, ::-1][:, 0:width]
    return pl.pallas_call(
        _bias_table_kernel,
        out_shape=jax.ShapeDtypeStruct((heads, tq, cols), F32),
        compiler_params=pltpu.CompilerParams(vmem_limit_bytes=VMEM_LIMIT_BYTES),
        name="bias_table",
    )(rev)


A_GROUP_WIDTH = A_GROUP * A_HEAD_DIM


def _lane_head(tq):
    return jnp.right_shift(lax.broadcasted_iota(jnp.int32, (tq, A_GROUP_WIDTH), 1), A_HEAD_DIM.bit_length() - 1)


def _attn_scores(qg, kg, bias_g, valid_from):
    lane_head = _lane_head(qg.shape[0])
    qg = qg.astype(BF16)
    zero = jnp.zeros_like(qg)
    qm = jnp.concatenate([jnp.where(lane_head == h, qg, zero) for h in range(A_GROUP)], axis=0)
    s = _dot_nt(qm, kg)
    col0 = s.shape[1] - bias_g.shape[1]
    s = jnp.concatenate([s[:, 0:col0], s[:, col0:] + bias_g], axis=1)
    if valid_from is not None:
        col = lax.broadcasted_iota(jnp.int32, s.shape, 1)
        s = jnp.where(col >= valid_from, s, NEG_INF)
    m = jnp.max(s, axis=-1, keepdims=True)
    p = jnp.exp2(s - m)
    return p.astype(BF16), 1.0 / jnp.sum(p, axis=-1, keepdims=True)


def _attn_values(p, rl, vg):
    tq = p.shape[0] // A_GROUP
    o = _dot(p, vg)
    if rl.shape[1] == 1:
        o = o * rl
    else:
        o = o * jnp.concatenate([rl] * (A_GROUP_WIDTH // LANES), axis=1)
    lane_head = _lane_head(tq)
    out = jnp.zeros((tq, A_GROUP_WIDTH), F32)
    for h in range(A_GROUP):
        out = jnp.where(lane_head == h, o[h * tq:(h + 1) * tq], out)
    return out


A_CHUNKS_PER_STEP = 4


def _attn_prompt_kernel(q_ref, k_ref, v_ref, bias_ref, o_ref, kpad_ref, vpad_ref, *, window, tq):
    s = pl.program_id(1)
    band = window + tq
    gw = A_GROUP_WIDTH
    cps = q_ref.shape[0] // tq

    @pl.when(s == 0)
    def _():
        zeros = jnp.zeros((window, kpad_ref.shape[1]), BF16)
        kpad_ref[0:window, :] = zeros
        vpad_ref[0:window, :] = zeros
        kpad_ref[window:, :] = k_ref[...]
        vpad_ref[window:, :] = v_ref[...]

    for j in range(cps):
        c = s * cps + j
        start = pl.multiple_of(c * tq, tq)
        rows = slice(j * tq, (j + 1) * tq)
        for g in range(A_HEADS // A_GROUP):
            cols = slice(g * gw, (g + 1) * gw)
            p, rl = _attn_scores(q_ref[rows, cols], kpad_ref[pl.ds(start, band), cols], bias_ref[g],
                                 window - c * tq)
            o_ref[rows, cols] = _attn_values(p, rl, vpad_ref[pl.ds(start, band), cols]).astype(o_ref.dtype)


def _attn_prompt(z3, bias_g, tq, window):
    bsz, t, _ = z3.shape
    width = A_HEADS * A_HEAD_DIM
    rows = tq * min(A_CHUNKS_PER_STEP, t // tq)
    return pl.pallas_call(
        functools.partial(_attn_prompt_kernel, window=window, tq=tq),
        grid=(bsz, t // rows),
        in_specs=[
            pl.BlockSpec((None, rows, width), lambda b, s: (b, s, 0)),
            pl.BlockSpec((None, t, width), lambda b, s: (b, 0, K_BLOCK)),
            pl.BlockSpec((None, t, width), lambda b, s: (b, 0, V_BLOCK)),
            pl.BlockSpec(bias_g.shape, lambda b, s: (0, 0, 0)),
        ],
        out_specs=pl.BlockSpec((None, rows, width), lambda b, s: (b, s, 0)),
        out_shape=jax.ShapeDtypeStruct((bsz, t, width), BF16),
        scratch_shapes=[pltpu.VMEM((t + window, width), BF16), pltpu.VMEM((t + window, width), BF16)],
        compiler_params=_params("arbitrary", "arbitrary"),
        name="attn_prompt",
    )(z3, z3, z3, bias_g)


def _attn_sample_kernel(q_ref, kn_ref, vn_ref, knt_ref, vnt_ref, kt_ref, vt_ref, bias_ref,
                        o_ref, ko_ref, vo_ref):
    tq = q_ref.shape[0]
    lc = kt_ref.shape[1]
    gw = A_GROUP_WIDTH
    lane_head = _lane_head(tq)
    ko_ref[...] = jnp.concatenate([kt_ref[:, tq:], knt_ref[...]], axis=1)
    vo_ref[...] = jnp.concatenate([vt_ref[:, tq:], vnt_ref[...]], axis=1)
    for g in range(A_HEADS // A_GROUP):
        cols = slice(g * gw, (g + 1) * gw)
        qg = q_ref[:, cols].astype(BF16)
        zero = jnp.zeros_like(qg)
        qm = jnp.concatenate([jnp.where(lane_head == h, qg, zero) for h in range(A_GROUP)], axis=0)
        s_c = _dot(qm, kt_ref[cols, :].astype(BF16))
        s_n = _dot_nt(qm, kn_ref[:, cols].astype(BF16))
        bias = bias_ref[g]
        col0 = lc + tq - bias.shape[1]
        s_c = jnp.concatenate([s_c[:, 0:col0], s_c[:, col0:] + bias[:, 0:lc - col0]], axis=1)
        s_n = s_n + bias[:, lc - col0:]
        m = jnp.maximum(jnp.max(s_c, axis=-1, keepdims=True), jnp.max(s_n, axis=-1, keepdims=True))
        p_c = jnp.exp2(s_c - m)
        p_n = jnp.exp2(s_n - m)
        rl = 1.0 / (jnp.sum(p_c, axis=-1, keepdims=True) + jnp.sum(p_n, axis=-1, keepdims=True))
        o = (_dot_nt(p_c.astype(BF16), vt_ref[cols, :].astype(BF16))
             + _dot(p_n.astype(BF16), vn_ref[:, cols].astype(BF16))) * rl
        out = jnp.zeros((tq, gw), F32)
        for h in range(A_GROUP):
            out = jnp.where(lane_head == h, o[h * tq:(h + 1) * tq], out)
        o_ref[:, cols] = out.astype(o_ref.dtype)


def _attn_sample(z3, knt, vnt, kt, vt, bias_g):
    bsz, tq, _ = z3.shape
    width, lc = kt.shape[1:]
    cache = jax.ShapeDtypeStruct(kt.shape, kt.dtype)
    cache_spec = pl.BlockSpec((None, width, lc), lambda b: (b, 0, 0))
    new_spec = pl.BlockSpec((None, width, tq), lambda b: (b, 0, 0))
    return pl.pallas_call(
        _attn_sample_kernel,
        grid=(bsz,),
        in_specs=[
            pl.BlockSpec((None, tq, width), lambda b: (b, 0, 0)),
            pl.BlockSpec((None, tq, width), lambda b: (b, 0, K_BLOCK)),
            pl.BlockSpec((None, tq, width), lambda b: (b, 0, V_BLOCK)),
            new_spec, new_spec, cache_spec, cache_spec,
            pl.BlockSpec(bias_g.shape, lambda b: (0, 0, 0)),
        ],
        out_specs=[pl.BlockSpec((None, tq, width), lambda b: (b, 0, 0)), cache_spec, cache_spec],
        out_shape=[jax.ShapeDtypeStruct((bsz, tq, width), BF16), cache, cache],
        compiler_params=_params("arbitrary"),
        name="attn_sample",
    )(z3, z3, z3, knt, vnt, kt, vt, bias_g)


def _gdn_kernel(x_ref, ab_ref, conv0_ref, s0_ref, cw_ref, pcol_ref, ng_ref,
                o_ref, s_ref, xbuf_ref):
    c = pl.program_id(1)
    nb, ch, _ = x_ref.shape
    hd = B_HEAD_DIM
    width = B_HEADS * hd
    keep = B_CONV - 1
    shift_on_mxu = x_ref.dtype == BF16
    top = xbuf_ref.shape[1] - (0 if shift_on_mxu else ch)
    chains = [(bi, h) for bi in range(nb) for h in range(B_HEADS)]
    every = range(len(chains))

    @pl.when(c == 0)
    def _():
        hist = jnp.concatenate([jnp.zeros((nb, top - keep, xbuf_ref.shape[2]), F32),
                                conv0_ref[...].astype(F32)], axis=1)
        xbuf_ref[:, 0:top, :] = hist.astype(xbuf_ref.dtype)
        s_ref[...] = s0_ref[...].astype(F32)

    if shift_on_mxu:
        t_r = lax.broadcasted_iota(jnp.int32, (ch, top + ch), 0)
        u_c = lax.broadcasted_iota(jnp.int32, (ch, top + ch), 1)
        shift01 = jnp.concatenate([jnp.where(u_c == top + t_r - (keep - i), 1.0, 0.0) for i in range(keep)],
                                  axis=0).astype(BF16)
        shifted = [_dot(shift01, jnp.concatenate([xbuf_ref[bi], x_ref[bi]], axis=0)) for bi in range(nb)]
    else:
        xbuf_ref[:, top:top + ch, :] = x_ref[...]

    t_i = lax.broadcasted_iota(jnp.int32, (ch, ch), 0)
    s_i = lax.broadcasted_iota(jnp.int32, (ch, ch), 1)
    tri_incl = (t_i >= s_i)
    tri_strict = (t_i > s_i)
    eye = jnp.where(t_i == s_i, 1.0, 0.0)
    lane = lax.broadcasted_iota(jnp.int32, (ch, LANES), 1)
    lane_row = lax.broadcasted_iota(jnp.int32, (1, ch), 1)
    frame = lax.broadcasted_iota(jnp.int32, (ch, LANES), 0)

    gcum_col, beta_col, gcum_row = [], [], []
    for bi in range(nb):
        ab = ab_ref[bi]
        g = -jnp.exp(pcol_ref[0:1, :]) * jax.nn.softplus(ab + pcol_ref[1:2, :])
        step = 1
        while step < ch:
            g = g + jnp.where(frame >= step, pltpu.roll(g, step, axis=0), 0.0)
            step *= 2
        beta_col.append(_sigmoid(ab))
        gcum_col.append(g)
        gcum_row.append(g.T)

    def pick(x, idx):
        return jnp.sum(jnp.where(lane == idx, x, 0.0), axis=-1, keepdims=True)

    def conv_silu(bi, col0):
        cols = slice(col0, col0 + hd)
        if shift_on_mxu:
            acc = x_ref[bi, :, cols].astype(F32) * cw_ref[keep:keep + 1, cols]
            for i in range(keep):
                acc = acc + shifted[bi][i * ch:(i + 1) * ch, cols] * cw_ref[i:i + 1, cols]
        else:
            acc = xbuf_ref[bi, top:top + ch, cols] * cw_ref[keep:keep + 1, cols]
            for i in range(keep):
                acc = acc + xbuf_ref[bi, top - keep + i:top - keep + i + ch, cols] * cw_ref[i:i + 1, cols]
        return _silu(acc)

    def l2norm(x):
        return x * lax.rsqrt(jnp.sum(x * x, axis=-1, keepdims=True) + L2_EPS)

    q = [l2norm(conv_silu(bi, h * hd)) * (hd ** -0.5) for bi, h in chains]
    k = [l2norm(conv_silu(bi, width + h * hd)) for bi, h in chains]
    v = [conv_silu(bi, 2 * width + h * hd) for bi, h in chains]
    gc = [pick(gcum_col[bi], h) for bi, h in chains]
    bc = [pick(beta_col[bi], B_HEADS + h) for bi, h in chains]
    gr = [gcum_row[bi][h:h + 1, :] for bi, h in chains]
    glast = [jnp.sum(jnp.where(lane_row == ch - 1, gr[i], 0.0), axis=-1, keepdims=True) for i in every]
    dec_incl = [jnp.exp(jnp.where(tri_incl, gc[i] - gr[i], NEG_INF)) for i in every]
    eg = [jnp.exp(gc[i]) for i in every]
    nbc = [-bc[i] for i in every]

    qk_kk = [_dot_nt(jnp.concatenate([q[i], k[i]], axis=0).astype(BF16), k[i].astype(BF16)) for i in every]
    a_qk = [qk_kk[i][0:ch] * dec_incl[i] for i in every]

    tp = [jnp.concatenate([(nbc[i] * qk_kk[i][ch:2 * ch]) * jnp.where(tri_strict, dec_incl[i], 0.0), eye], axis=1)
          for i in every]
    t_half = lax.broadcasted_iota(jnp.int32, (ch, 2 * ch), 1) >= ch
    for _ in range(max(1, (ch - 1).bit_length())):
        tp_b = [tp[i].astype(BF16) for i in every]
        tp = [_dot(tp_b[i][:, 0:ch], tp_b[i]) + jnp.where(t_half, tp[i], 0.0) for i in every]
    rhs = [jnp.concatenate([bc[i] * v[i], (bc[i] * eg[i]) * k[i]], axis=1).astype(BF16) for i in every]
    x_sol = [_dot(tp[i][:, ch:2 * ch].astype(BF16), rhs[i]) for i in every]

    state = [s_ref[bi, h] for bi, h in chains]
    wq = [_dot(jnp.concatenate([x_sol[i][:, hd:2 * hd], q[i] * eg[i]], axis=0).astype(BF16),
               state[i].astype(BF16)) for i in every]
    wv_b = [(x_sol[i][:, 0:hd] - wq[i][0:ch]).astype(BF16) for i in every]
    o = [wq[i][ch:2 * ch] + _dot(a_qk[i].astype(BF16), wv_b[i]) for i in every]
    s_new = [jnp.exp(glast[i]) * state[i]
             + _dot_tn((k[i] * jnp.exp(glast[i] - gc[i])).astype(BF16), wv_b[i]) for i in every]
    for i, (bi, h) in enumerate(chains):
        s_ref[bi, h] = s_new[i]
        on = o[i] * lax.rsqrt(jnp.mean(o[i] * o[i], axis=-1, keepdims=True) + RMS_EPS) * ng_ref[...]
        o_ref[bi, :, h * hd:(h + 1) * hd] = on.astype(o_ref.dtype)

    if shift_on_mxu:
        xbuf_ref[...] = x_ref[:, ch - top:ch, :]
    else:
        xbuf_ref[:, top - keep:top, :] = xbuf_ref[:, top + ch - keep:top + ch, :]


GDN_BATCHES = 2


def _gdn(z3, ab3, conv0, s0, conv_w, pcol, norm_g, ch):
    bsz, t, _ = z3.shape
    nb = GDN_BATCHES
    width3 = 3 * B_HEADS * B_HEAD_DIM
    width = B_HEADS * B_HEAD_DIM
    if z3.dtype == BF16:
        xbuf = pltpu.VMEM((nb, 2 * SUBLANES, width3), BF16)
    else:
        xbuf = pltpu.VMEM((nb, SUBLANES + ch, width3), F32)
    return pl.pallas_call(
        _gdn_kernel,
        grid=(bsz // nb, t // ch),
        in_specs=[
            pl.BlockSpec((nb, ch, width3), lambda b, c: (b, c, 1)),
            pl.BlockSpec((nb, ch, LANES), lambda b, c: (b, c, 0)),
            pl.BlockSpec((nb, B_CONV - 1, width3), lambda b, c: (b, 0, 0)),
            pl.BlockSpec((nb, B_HEADS, B_HEAD_DIM, B_HEAD_DIM), lambda b, c: (b, 0, 0, 0)),
            pl.BlockSpec(conv_w.shape, lambda b, c: (0, 0)),
            pl.BlockSpec(pcol.shape, lambda b, c: (0, 0)),
            pl.BlockSpec(norm_g.shape, lambda b, c: (0, 0)),
        ],
        out_specs=[
            pl.BlockSpec((nb, ch, width), lambda b, c: (b, c, 0)),
            pl.BlockSpec((nb, B_HEADS, B_HEAD_DIM, B_HEAD_DIM), lambda b, c: (b, 0, 0, 0)),
        ],
        out_shape=[
            jax.ShapeDtypeStruct((bsz, t, width), BF16),
            jax.ShapeDtypeStruct((bsz, B_HEADS, B_HEAD_DIM, B_HEAD_DIM), F32),
        ],
        scratch_shapes=[xbuf],
        compiler_params=_params("arbitrary", "arbitrary"),
        name="gdn",
    )(z3, ab3, conv0, s0, conv_w, pcol, norm_g)


MEMATTN_ROWS = 512


def _memattn_kernel(q_ref, mk_ref, mv_ref, o_ref, kb_ref, vb_ref):
    dh = C_HEAD_DIM

    @pl.when(pl.program_id(1) == 0)
    def _():
        for h in range(C_HEADS):
            kb_ref[:, h * dh:(h + 1) * dh] = mk_ref[:, h, :].astype(BF16)
            vb_ref[:, h * dh:(h + 1) * dh] = mv_ref[:, h, :].astype(BF16)

    tq = q_ref.shape[0]
    rb = min(MEMATTN_ROWS, tq)
    for r0 in range(0, tq, rb):
        for h in range(C_HEADS):
            cols = slice(h * dh, (h + 1) * dh)
            q = q_ref[r0:r0 + rb, cols].astype(BF16)
            s = _dot_nt(q, kb_ref[:, cols]) * (dh ** -0.5)
            m = jnp.max(s, axis=-1, keepdims=True)
            p = jnp.exp(s - m)
            l = jnp.sum(p, axis=-1, keepdims=True)
            o = _dot(p.astype(BF16), vb_ref[:, cols]) * (1.0 / l)
            o_ref[r0:r0 + rb, cols] = o.astype(o_ref.dtype)


def _memattn(z3, q_blk, mk, mv, tq):
    bsz, t, _ = z3.shape
    mem = mk.shape[1]
    width = C_HEADS * C_HEAD_DIM
    kv_spec = pl.BlockSpec((None, mem, C_HEADS, C_HEAD_DIM), lambda b, i: (b, 0, 0, 0))
    return pl.pallas_call(
        _memattn_kernel,
        grid=(bsz, t // tq),
        in_specs=[
            pl.BlockSpec((None, tq, width), lambda b, i: (b, i, q_blk)),
            kv_spec,
            kv_spec,
        ],
        out_specs=pl.BlockSpec((None, tq, width), lambda b, i: (b, i, 0)),
        out_shape=jax.ShapeDtypeStruct((bsz, t, width), BF16),
        scratch_shapes=[pltpu.VMEM((mem, width), BF16), pltpu.VMEM((mem, width), BF16)],
        compiler_params=_params("arbitrary", "arbitrary"),
        name="memattn",
    )(z3, mk, mv)


def _layer_norm(x, g, b):
    mu = jnp.mean(x, axis=-1, keepdims=True)
    xc = x - mu
    var = jnp.mean(xc * xc, axis=-1, keepdims=True)
    return xc * lax.rsqrt(var + LN_EPS) * g + b


FFN_PARTS = 2
FFN_HIDDEN_BLOCK = 1024


def _ffn_kernel(x_ref, ga_ref, gb_ref, gc_ref, oa_ref, ob_ref, oc_ref, wo_ref, w1_ref, w2_ref,
                vec_ref, b1_ref, y_ref, *, alpha, ff_blk):
    bm = x_ref.shape[0]
    rows = [pl.ds(r * (bm // FFN_PARTS), bm // FFN_PARTS) for r in range(FFN_PARTS)]
    parts = range(FFN_PARTS)

    def gated(g_ref, o_ref, r):
        return _sigmoid(g_ref[r, :].astype(F32)) * o_ref[r, :].astype(F32)

    merged = [(gated(ga_ref, oa_ref, r) + gated(gb_ref, ob_ref, r) + gated(gc_ref, oc_ref, r)).astype(BF16)
              for r in rows]
    proj = [_dot(merged[i], wo_ref[...]) for i in parts]
    h = [_layer_norm(alpha * x_ref[rows[i], :] + proj[i], vec_ref[0:1, :], vec_ref[1:2, :]) for i in parts]
    hb = [h[i].astype(BF16) for i in parts]
    acc = [None] * FFN_PARTS
    for k0 in range(0, w1_ref.shape[1], ff_blk):
        f = [_dot(hb[i], w1_ref[:, k0:k0 + ff_blk]) + b1_ref[:, k0:k0 + ff_blk] for i in parts]
        f = [jnp.square(jnp.maximum(f[i], 0.0)).astype(BF16) for i in parts]
        d = [_dot(f[i], w2_ref[k0:k0 + ff_blk, :]) for i in parts]
        acc = [d[i] if acc[i] is None else acc[i] + d[i] for i in parts]
    for i in parts:
        y_ref[rows[i], :] = _layer_norm(alpha * h[i] + acc[i] + vec_ref[2:3, :], vec_ref[3:4, :], vec_ref[4:5, :])


def _ffn(x2, z2, gate_blk0, oa, ob, oc, wo, w1, w2, vec, b1, alpha, bm):
    m, d = x2.shape
    dff = w1.shape[1]
    const = dict(pipeline_mode=pl.Buffered(1))
    row = lambda i: (i, 0)
    return pl.pallas_call(
        functools.partial(_ffn_kernel, alpha=alpha, ff_blk=FFN_HIDDEN_BLOCK),
        grid=(m // bm,),
        in_specs=[
            pl.BlockSpec((bm, d), row),
            pl.BlockSpec((bm, d), lambda i: (i, gate_blk0)),
            pl.BlockSpec((bm, d), lambda i: (i, gate_blk0 + 1)),
            pl.BlockSpec((bm, d), lambda i: (i, gate_blk0 + 2)),
            pl.BlockSpec((bm, d), row),
            pl.BlockSpec((bm, d), row),
            pl.BlockSpec((bm, d), row),
            pl.BlockSpec((d, d), lambda i: (0, 0), **const),
            pl.BlockSpec((d, dff), lambda i: (0, 0), **const),
            pl.BlockSpec((dff, d), lambda i: (0, 0), **const),
            pl.BlockSpec(vec.shape, lambda i: (0, 0), **const),
            pl.BlockSpec(b1.shape, lambda i: (0, 0), **const),
        ],
        out_specs=pl.BlockSpec((bm, d), row),
        out_shape=jax.ShapeDtypeStruct((m, d), F32),
        compiler_params=_params("arbitrary"),
        name="merge_ffn",
    )(x2, z2, z2, z2, oa, ob, oc, wo, w1, w2, vec, b1)


QC_BLOCK, GATE_BLOCK0 = 6, 7
INPROJ_ROWS = 2048


def _group_layer(x3, z_dtype, attn_fn, conv0, ssm0, mem_k, mem_v, lw, keep=None):
    bsz, t, d = x3.shape
    m = bsz * t
    ch = min(CHUNK, t)
    x2 = x3.reshape(m, d)
    if keep is not None:
        z3, ab3, *tails = _inproj_seq(x3, lw["w_in"], lw["wab"], lw["n_main"], keep)
        z2 = z3.reshape(m, -1)
    else:
        z2, ab = _inproj(x2, lw["w_in"], lw["wab"], lw["n_main"], z_dtype, min(INPROJ_ROWS, m), d)
        z3, ab3, tails = z2.reshape(bsz, t, -1), ab.reshape(bsz, t, LANES), None
    o_a = attn_fn(z3)
    o_b, ssm = _gdn(z3, ab3, conv0, ssm0, lw["conv_w"], lw["pcol"], lw["norm_g"], ch)
    o_c = _memattn(z3, QC_BLOCK, mem_k, mem_v, min(2048, t))
    y = _ffn(x2, z2, GATE_BLOCK0, o_a.reshape(m, d), o_b.reshape(m, d), o_c.reshape(m, d),
             lw["wo"], lw["w1"], lw["w2"], lw["vec"], lw["b1"], lw["alpha"], min(512, m))
    return y.reshape(bsz, t, d), z3, ssm, tails


def kernel(x_prompt, x_sample, cache_a_k, cache_a_v, state_b_conv, state_b_ssm, cache_mem_k, cache_mem_v, mem_prompt, w_in, w_b_conv, b_a_log, b_dt_bias, b_norm_g, a_rel_bias, w_mem_kv, w_out, ln1_g, ln1_b, w_ff1, b_ff1, w_ff2, b_ff2, ln2_g, ln2_b):
    depth = w_in.shape[0]
    bp, tp, d = x_prompt.shape
    bs, tn, _ = x_sample.shape
    window = A_LEFT_CHUNKS * CHUNK
    keep_p = min(window, tp)
    lc = cache_a_k.shape[2]
    n_main = w_in.shape[2] - 2 * B_HEADS
    a_width = A_HEADS * A_HEAD_DIM
    b_width3 = 3 * B_HEADS * B_HEAD_DIM
    alpha = (2.0 * depth) ** 0.25
    xp, xs = x_prompt, x_sample
    outs = [[] for _ in range(10)]
    for l in range(depth):
        q_scale = jnp.where(jnp.arange(w_in.shape[2]) < a_width, (A_HEAD_DIM ** -0.5) * LOG2E, 1.0)
        w_in_b = _scale_cast(w_in, l, q_scale.astype(F32).reshape(1, -1), d)
        vec = jnp.zeros((SUBLANES, d), F32)
        vec = vec.at[0].set(ln1_g[l]).at[1].set(ln1_b[l]).at[2].set(b_ff2[l]).at[3].set(ln2_g[l]).at[4].set(ln2_b[l])
        pcol = jnp.zeros((SUBLANES, LANES), F32)
        pcol = pcol.at[0, 0:B_HEADS].set(b_a_log[l]).at[1, 0:B_HEADS].set(b_dt_bias[l])
        lw = dict(
            w_in=w_in_b, n_main=n_main,
            wab=jnp.pad(w_in_b[:, n_main:], ((0, 0), (0, LANES - 2 * B_HEADS))),
            conv_w=w_b_conv[l],
            pcol=pcol,
            norm_g=b_norm_g[l].reshape(1, B_HEAD_DIM),
            wo=w_out[l].astype(BF16), w1=w_ff1[l].astype(BF16), w2=w_ff2[l].astype(BF16),
            vec=vec, b1=b_ff1[l].reshape(1, -1), alpha=alpha,
        )
        band = window + CHUNK
        bias_tab = _bias_table(a_rel_bias[l], CHUNK, band)
        bias_cols = bias_tab.shape[2]
        col0 = band - bias_cols

        bias_p = bias_tab.reshape(A_HEADS // A_GROUP, A_GROUP * CHUNK, bias_cols)
        mk, mv = _mem_kv(mem_prompt, w_mem_kv[l].astype(BF16))
        conv0 = jnp.zeros((bp, B_CONV - 1, b_width3), F32)
        ssm0 = jnp.zeros((bp, B_HEADS, B_HEAD_DIM, B_HEAD_DIM), F32)
        xp, _, ssm_p, (k_tail, v_tail, conv_tail) = _group_layer(
            xp, BF16, lambda z3: _attn_prompt(z3, bias_p, CHUNK, window), conv0, ssm0, mk, mv, lw, keep=keep_p)
        outs[0].append(k_tail.reshape(bp, keep_p, A_HEADS, A_HEAD_DIM))
        outs[1].append(v_tail.reshape(bp, keep_p, A_HEADS, A_HEAD_DIM))
        outs[2].append(conv_tail[:, SUBLANES - (B_CONV - 1):])
        outs[3].append(ssm_p)
        outs[4].append(mk)
        outs[5].append(mv)

        bias_s = bias_tab[:, 0:tn, 0:lc + tn - col0].reshape(A_HEADS // A_GROUP, A_GROUP * tn, lc + tn - col0)
        kt = cache_a_k[l].transpose(0, 2, 3, 1).reshape(bs, a_width, lc)
        vt = cache_a_v[l].transpose(0, 2, 3, 1).reshape(bs, a_width, lc)
        advanced = {}

        def attn_sample(z3):
            knt = z3[:, :, a_width:2 * a_width].transpose(0, 2, 1)
            vnt = z3[:, :, 2 * a_width:3 * a_width].transpose(0, 2, 1)
            o_a, advanced["k"], advanced["v"] = _attn_sample(z3, knt, vnt, kt, vt, bias_s)
            return o_a

        xs, zs3, ssm_s, _ = _group_layer(
            xs, F32, attn_sample, state_b_conv[l], state_b_ssm[l], cache_mem_k[l], cache_mem_v[l], lw)
        outs[6].append(advanced["k"].reshape(bs, A_HEADS, A_HEAD_DIM, lc).transpose(0, 3, 1, 2))
        outs[7].append(advanced["v"].reshape(bs, A_HEADS, A_HEAD_DIM, lc).transpose(0, 3, 1, 2))
        qkvb_s = zs3[:, :, 3 * a_width:3 * a_width + b_width3]
        outs[8].append(jnp.concatenate([state_b_conv[l], qkvb_s], axis=1)[:, tn:])
        outs[9].append(ssm_s)
    return (xp, xs) + tuple(jnp.stack(o) for o in outs)
```

```python
import functools

import jax
import jax.numpy as jnp
from jax import lax
from jax.experimental import pallas as pl
from jax.experimental.pallas import tpu as pltpu

F32 = jnp.float32
BF16 = jnp.bfloat16

CHUNK = 64
A_HEADS = 16
A_HEAD_DIM = 64
A_GROUP = 4
A_LEFT_CHUNKS = 8
A_REL_CLIP = 128
B_HEADS = 8
B_HEAD_DIM = 128
B_CONV = 4
C_HEADS = 4
C_HEAD_DIM = 256
N_BRANCH = 3
LN_EPS = 1e-5
RMS_EPS = 1e-6
L2_EPS = 1e-6
NEG_INF = -1e30

LANES = 128
SUBLANES = 8
VMEM_LIMIT_BYTES = 56 * 1024 * 1024


def _params(*sem):
    return pltpu.CompilerParams(dimension_semantics=sem, vmem_limit_bytes=VMEM_LIMIT_BYTES)


def _dot(a, b):
    return jnp.dot(a, b, preferred_element_type=F32)


def _dot_nt(a, b):
    return lax.dot_general(a, b, (((1,), (1,)), ((), ())), preferred_element_type=F32)


def _dot_tn(a, b):
    return lax.dot_general(a, b, (((0,), (0,)), ((), ())), preferred_element_type=F32)


def _sigmoid(x):
    return 0.5 + 0.5 * jnp.tanh(0.5 * x)


def _silu(x):
    hx = 0.5 * x
    return hx + hx * jnp.tanh(hx)


def _scale_cast_kernel(w_ref, s_ref, o_ref):
    o_ref[...] = (w_ref[...] * s_ref[...]).astype(o_ref.dtype)


def _scale_cast(w, layer, col_scale, bn):
    _, k, n = w.shape
    return pl.pallas_call(
        _scale_cast_kernel,
        grid=(pl.cdiv(n, bn),),
        in_specs=[
            pl.BlockSpec((None, k, bn), lambda j: (layer, 0, j)),
            pl.BlockSpec((1, bn), lambda j: (0, j)),
        ],
        out_specs=pl.BlockSpec((k, bn), lambda j: (0, j)),
        out_shape=jax.ShapeDtypeStruct((k, n), BF16),
        compiler_params=_params("arbitrary"),
        name="scale_cast",
    )(w, col_scale)


K_BLOCK, V_BLOCK, CONV_BLOCK0, CONV_BLOCKS = 1, 2, 3, 3
QC_BLOCK, GATE_BLOCK0 = 6, 7


def _inproj_kernel(x_ref, w_ref, wab_ref, z_ref, ab_ref, xb_ref):
    @pl.when(pl.program_id(1) == 0)
    def _():
        xb = x_ref[...].astype(BF16)
        xb_ref[...] = xb
        ab_ref[...] = _dot(xb, wab_ref[...])

    z_ref[...] = _dot(xb_ref[...], w_ref[...]).astype(z_ref.dtype)


def _inproj(x2, w, wab, n_main, out_dtype, bm, bn):
    m, d = x2.shape
    return pl.pallas_call(
        _inproj_kernel,
        grid=(m // bm, n_main // bn),
        in_specs=[
            pl.BlockSpec((bm, d), lambda i, j: (i, 0)),
            pl.BlockSpec((d, bn), lambda i, j: (0, j)),
            pl.BlockSpec((d, LANES), lambda i, j: (0, 0)),
        ],
        out_specs=[
            pl.BlockSpec((bm, bn), lambda i, j: (i, j)),
            pl.BlockSpec((bm, LANES), lambda i, j: (i, 0)),
        ],
        out_shape=[
            jax.ShapeDtypeStruct((m, n_main), out_dtype),
            jax.ShapeDtypeStruct((m, LANES), F32),
        ],
        scratch_shapes=[pltpu.VMEM((bm, d), BF16)],
        compiler_params=_params("arbitrary", "arbitrary"),
        name="inproj",
    )(x2, w, wab)


INPROJ_SUB = 2 * B_HEAD_DIM


def _inproj_seq_kernel(x_ref, w_ref, wab_ref, z_ref, ab_ref, kt_ref, vt_ref, ct_ref, xb_ref):
    j = pl.program_id(1)
    bm, bn = z_ref.shape
    keep = kt_ref.shape[0]

    @pl.when(j == 0)
    def _():
        xb = x_ref[...].astype(BF16)
        xb_ref[...] = xb
        ab_ref[...] = _dot(xb, wab_ref[...])
        kt_ref[...] = jnp.zeros(kt_ref.shape, F32)
        vt_ref[...] = jnp.zeros(vt_ref.shape, F32)

    tails = []
    for n0 in range(0, bn, INPROJ_SUB):
        cols = slice(n0, n0 + INPROJ_SUB)
        acc = _dot(xb_ref[...], w_ref[:, cols])
        z_ref[:, cols] = acc.astype(z_ref.dtype)
        tail = acc[bm - keep:, :]
        kt_ref[:, cols] = jnp.where(j == K_BLOCK, tail, kt_ref[:, cols])
        vt_ref[:, cols] = jnp.where(j == V_BLOCK, tail, vt_ref[:, cols])
        tails.append(acc[bm - SUBLANES:, :])

    @pl.when((j >= CONV_BLOCK0) & (j < CONV_BLOCK0 + CONV_BLOCKS))
    def _():
        ct_ref[...] = jnp.concatenate(tails, axis=1)


def _inproj_seq(x3, w, wab, n_main, keep):
    nseq, t, d = x3.shape
    bm, bn = t, d
    conv_blk = lambda i, j: jnp.clip(j - CONV_BLOCK0, 0, CONV_BLOCKS - 1)
    return pl.pallas_call(
        _inproj_seq_kernel,
        grid=(nseq, n_main // bn),
        in_specs=[
            pl.BlockSpec((None, bm, d), lambda i, j: (i, 0, 0)),
            pl.BlockSpec((d, bn), lambda i, j: (0, j)),
            pl.BlockSpec((d, LANES), lambda i, j: (0, 0)),
        ],
        out_specs=[
            pl.BlockSpec((None, bm, bn), lambda i, j: (i, 0, j)),
            pl.BlockSpec((None, bm, LANES), lambda i, j: (i, 0, 0)),
            pl.BlockSpec((None, keep, bn), lambda i, j: (i, 0, 0)),
            pl.BlockSpec((None, keep, bn), lambda i, j: (i, 0, 0)),
            pl.BlockSpec((None, SUBLANES, bn), lambda i, j: (i, 0, conv_blk(i, j))),
        ],
        out_shape=[
            jax.ShapeDtypeStruct((nseq, t, n_main), BF16),
            jax.ShapeDtypeStruct((nseq, t, LANES), F32),
            jax.ShapeDtypeStruct((nseq, keep, bn), F32),
            jax.ShapeDtypeStruct((nseq, keep, bn), F32),
            jax.ShapeDtypeStruct((nseq, SUBLANES, CONV_BLOCKS * bn), F32),
        ],
        scratch_shapes=[pltpu.VMEM((bm, d), BF16)],
        compiler_params=_params("arbitrary", "arbitrary"),
        name="inproj_seq",
    )(x3, w, wab)


def _mem_kv_kernel(x_ref, w_ref, k_ref, v_ref):
    acc = _dot(x_ref[...].astype(BF16), w_ref[...])
    _, heads, dh = k_ref.shape
    for h in range(heads):
        k_ref[:, h, :] = acc[:, h * dh:(h + 1) * dh]
        v_ref[:, h, :] = acc[:, (heads + h) * dh:(heads + h + 1) * dh]


def _mem_kv(mem, w):
    bsz, mt, d = mem.shape
    out = jax.ShapeDtypeStruct((bsz, mt, C_HEADS, C_HEAD_DIM), F32)
    return pl.pallas_call(
        _mem_kv_kernel,
        grid=(bsz,),
        in_specs=[
            pl.BlockSpec((None, mt, d), lambda b: (b, 0, 0)),
            pl.BlockSpec(w.shape, lambda b: (0, 0)),
        ],
        out_specs=[pl.BlockSpec((None, mt, C_HEADS, C_HEAD_DIM), lambda b: (b, 0, 0, 0))] * 2,
        out_shape=[out, out],
        compiler_params=_params("arbitrary"),
        name="mem_kv",
    )(mem, w)


LOG2E = 1.4426950408889634


def _bias_col0(window):
    return (window - A_REL_CLIP + 1) // LANES * LANES


def _bias_table_kernel(rev_ref, o_ref):
    heads, tq, cols = o_ref.shape
    width = rev_ref.shape[1]
    lane = lax.broadcasted_iota(jnp.int32, (heads, width), 1)
    w = jnp.where(lane < cols, (rev_ref[...] - rev_ref[:, 0:1]) * LOG2E, 0.0)
    for h in range(heads):
        rows = jnp.broadcast_to(w[h:h + 1, :], (tq, width))
        o_ref[h] = pltpu.roll(rows, 0, axis=1, stride=1, stride_axis=0)[:, 0:cols]


def _bias_table(rel_bias, tq, band):
    heads = rel_bias.shape[0]
    window = band - tq
    col0 = _bias_col0(window)
    cols = band - col0
    width = 2 * A_REL_CLIP
    assert window - col0 == A_REL_CLIP and window - (band - 1) >= -A_REL_CLIP
    assert cols + tq <= width and width % LANES == 0
    rev = rel_bias[:, ::-1][:, 0:width]
    return pl.pallas_call(
        _bias_table_kernel,
        out_shape=jax.ShapeDtypeStruct((heads, tq, cols), F32),
        compiler_params=pltpu.CompilerParams(vmem_limit_bytes=VMEM_LIMIT_BYTES),
        name="bias_table",
    )(rev)


A_GROUP_WIDTH = A_GROUP * A_HEAD_DIM


def _lane_head(tq):
    return jnp.right_shift(lax.broadcasted_iota(jnp.int32, (tq, A_GROUP_WIDTH), 1), A_HEAD_DIM.bit_length() - 1)


def _attn_scores(qg, kg, bias_g, valid_from):
    lane_head = _lane_head(qg.shape[0])
    qg = qg.astype(BF16)
    zero = jnp.zeros_like(qg)
    qm = jnp.concatenate([jnp.where(lane_head == h, qg, zero) for h in range(A_GROUP)], axis=0)
    s = _dot_nt(qm, kg)
    col0 = s.shape[1] - bias_g.shape[1]
    s = jnp.concatenate([s[:, 0:col0], s[:, col0:] + bias_g], axis=1)
    if valid_from is not None:
        col = lax.broadcasted_iota(jnp.int32, s.shape, 1)
        s = jnp.where(col >= valid_from, s, NEG_INF)
    m = jnp.max(s, axis=-1, keepdims=True)
    p = jnp.exp2(s - m)
    return p.astype(BF16), 1.0 / jnp.sum(p, axis=-1, keepdims=True)


def _attn_values(p, rl, vg):
    tq = p.shape[0] // A_GROUP
    o = _dot(p, vg)
    if rl.shape[1] == 1:
        o = o * rl
    else:
        o = o * jnp.concatenate([rl] * (A_GROUP_WIDTH // LANES), axis=1)
    lane_head = _lane_head(tq)
    out = jnp.zeros((tq, A_GROUP_WIDTH), F32)
    for h in range(A_GROUP):
        out = jnp.where(lane_head == h, o[h * tq:(h + 1) * tq], out)
    return out


A_CHUNKS_PER_STEP = 8


def _attn_prompt_kernel(q_ref, k_ref, v_ref, gate_ref, bias_ref, o_ref, kpad_ref, vpad_ref, *, window, tq):
    s = pl.program_id(1)
    band = window + tq
    gw = A_GROUP_WIDTH
    cps = q_ref.shape[0] // tq

    @pl.when(s == 0)
    def _():
        zeros = jnp.zeros((window, kpad_ref.shape[1]), BF16)
        kpad_ref[0:window, :] = zeros
        vpad_ref[0:window, :] = zeros
        kpad_ref[window:, :] = k_ref[...]
        vpad_ref[window:, :] = v_ref[...]

    for j in range(cps):
        c = s * cps + j
        start = pl.multiple_of(c * tq, tq)
        rows = slice(j * tq, (j + 1) * tq)
        for g in range(A_HEADS // A_GROUP):
            cols = slice(g * gw, (g + 1) * gw)
            p, rl = _attn_scores(q_ref[rows, cols], kpad_ref[pl.ds(start, band), cols], bias_ref[g],
                                 window - c * tq)
            out = _attn_values(p, rl, vpad_ref[pl.ds(start, band), cols])
            o_ref[rows, cols] = (out * _sigmoid(gate_ref[rows, cols].astype(F32))).astype(o_ref.dtype)


def _attn_prompt(z3, bias_g, tq, window):
    bsz, t, _ = z3.shape
    width = A_HEADS * A_HEAD_DIM
    rows = tq * min(A_CHUNKS_PER_STEP, t // tq)
    return pl.pallas_call(
        functools.partial(_attn_prompt_kernel, window=window, tq=tq),
        grid=(bsz, t // rows),
        in_specs=[
            pl.BlockSpec((None, rows, width), lambda b, s: (b, s, 0)),
            pl.BlockSpec((None, t, width), lambda b, s: (b, 0, K_BLOCK)),
            pl.BlockSpec((None, t, width), lambda b, s: (b, 0, V_BLOCK)),
            pl.BlockSpec((None, rows, width), lambda b, s: (b, s, GATE_BLOCK0)),
            pl.BlockSpec(bias_g.shape, lambda b, s: (0, 0, 0)),
        ],
        out_specs=pl.BlockSpec((None, rows, width), lambda b, s: (b, s, 0)),
        out_shape=jax.ShapeDtypeStruct((bsz, t, width), BF16),
        scratch_shapes=[pltpu.VMEM((t + window, width), BF16), pltpu.VMEM((t + window, width), BF16)],
        compiler_params=_params("arbitrary", "arbitrary"),
        name="attn_prompt",
    )(z3, z3, z3, z3, bias_g)


def _attn_sample_kernel(q_ref, kn_ref, vn_ref, gate_ref, knt_ref, vnt_ref, kt_ref, vt_ref, bias_ref,
                        o_ref, ko_ref, vo_ref):
    tq = q_ref.shape[0]
    lc = kt_ref.shape[1]
    gw = A_GROUP_WIDTH
    lane_head = _lane_head(tq)
    ko_ref[...] = jnp.concatenate([kt_ref[:, tq:], knt_ref[...]], axis=1)
    vo_ref[...] = jnp.concatenate([vt_ref[:, tq:], vnt_ref[...]], axis=1)
    for g in range(A_HEADS // A_GROUP):
        cols = slice(g * gw, (g + 1) * gw)
        qg = q_ref[:, cols].astype(BF16)
        zero = jnp.zeros_like(qg)
        qm = jnp.concatenate([jnp.where(lane_head == h, qg, zero) for h in range(A_GROUP)], axis=0)
        s_c = _dot(qm, kt_ref[cols, :].astype(BF16))
        s_n = _dot_nt(qm, kn_ref[:, cols].astype(BF16))
        bias = bias_ref[g]
        col0 = lc + tq - bias.shape[1]
        s_c = jnp.concatenate([s_c[:, 0:col0], s_c[:, col0:] + bias[:, 0:lc - col0]], axis=1)
        s_n = s_n + bias[:, lc - col0:]
        m = jnp.maximum(jnp.max(s_c, axis=-1, keepdims=True), jnp.max(s_n, axis=-1, keepdims=True))
        p_c = jnp.exp2(s_c - m)
        p_n = jnp.exp2(s_n - m)
        rl = 1.0 / (jnp.sum(p_c, axis=-1, keepdims=True) + jnp.sum(p_n, axis=-1, keepdims=True))
        o = (_dot_nt(p_c.astype(BF16), vt_ref[cols, :].astype(BF16))
             + _dot(p_n.astype(BF16), vn_ref[:, cols].astype(BF16))) * rl
        out = jnp.zeros((tq, gw), F32)
        for h in range(A_GROUP):
            out = jnp.where(lane_head == h, o[h * tq:(h + 1) * tq], out)
        o_ref[:, cols] = (out * _sigmoid(gate_ref[:, cols].astype(F32))).astype(o_ref.dtype)


def _attn_sample(z3, knt, vnt, kt, vt, bias_g):
    bsz, tq, _ = z3.shape
    width, lc = kt.shape[1:]
    cache = jax.ShapeDtypeStruct(kt.shape, kt.dtype)
    cache_spec = pl.BlockSpec((None, width, lc), lambda b: (b, 0, 0))
    new_spec = pl.BlockSpec((None, width, tq), lambda b: (b, 0, 0))
    return pl.pallas_call(
        _attn_sample_kernel,
        grid=(bsz,),
        in_specs=[
            pl.BlockSpec((None, tq, width), lambda b: (b, 0, 0)),
            pl.BlockSpec((None, tq, width), lambda b: (b, 0, K_BLOCK)),
            pl.BlockSpec((None, tq, width), lambda b: (b, 0, V_BLOCK)),
            pl.BlockSpec((None, tq, width), lambda b: (b, 0, GATE_BLOCK0)),
            new_spec, new_spec, cache_spec, cache_spec,
            pl.BlockSpec(bias_g.shape, lambda b: (0, 0, 0)),
        ],
        out_specs=[pl.BlockSpec((None, tq, width), lambda b: (b, 0, 0)), cache_spec, cache_spec],
        out_shape=[jax.ShapeDtypeStruct((bsz, tq, width), BF16), cache, cache],
        compiler_params=_params("arbitrary"),
        name="attn_sample",
    )(z3, z3, z3, z3, knt, vnt, kt, vt, bias_g)


def _gdn_kernel(x_ref, ab_ref, gate_ref, conv0_ref, s0_ref, cw_ref, pcol_ref, ng_ref,
                o_ref, s_ref, xbuf_ref):
    c = pl.program_id(1)
    nb, ch, _ = x_ref.shape
    hd = B_HEAD_DIM
    width = B_HEADS * hd
    keep = B_CONV - 1
    shift_on_mxu = x_ref.dtype == BF16
    top = xbuf_ref.shape[1] - (0 if shift_on_mxu else ch)
    chains = [(bi, h) for bi in range(nb) for h in range(B_HEADS)]
    every = range(len(chains))

    @pl.when(c == 0)
    def _():
        hist = jnp.concatenate([jnp.zeros((nb, top - keep, xbuf_ref.shape[2]), F32),
                                conv0_ref[...].astype(F32)], axis=1)
        xbuf_ref[:, 0:top, :] = hist.astype(xbuf_ref.dtype)
        s_ref[...] = s0_ref[...].astype(F32)

    if shift_on_mxu:
        t_r = lax.broadcasted_iota(jnp.int32, (ch, top + ch), 0)
        u_c = lax.broadcasted_iota(jnp.int32, (ch, top + ch), 1)
        shift01 = jnp.concatenate([jnp.where(u_c == top + t_r - (keep - i), 1.0, 0.0) for i in range(keep)],
                                  axis=0).astype(BF16)
        shifted = [_dot(shift01, jnp.concatenate([xbuf_ref[bi], x_ref[bi]], axis=0)) for bi in range(nb)]
    else:
        xbuf_ref[:, top:top + ch, :] = x_ref[...]

    t_i = lax.broadcasted_iota(jnp.int32, (ch, ch), 0)
    s_i = lax.broadcasted_iota(jnp.int32, (ch, ch), 1)
    tri_incl = (t_i >= s_i)
    tri_strict = (t_i > s_i)
    eye = jnp.where(t_i == s_i, 1.0, 0.0)
    lane = lax.broadcasted_iota(jnp.int32, (ch, LANES), 1)
    lane_row = lax.broadcasted_iota(jnp.int32, (1, ch), 1)
    frame = lax.broadcasted_iota(jnp.int32, (ch, LANES), 0)

    gcum_col, beta_col, gcum_row = [], [], []
    for bi in range(nb):
        ab = ab_ref[bi]
        g = -jnp.exp(pcol_ref[0:1, :]) * jax.nn.softplus(ab + pcol_ref[1:2, :])
        step = 1
        while step < ch:
            g = g + jnp.where(frame >= step, pltpu.roll(g, step, axis=0), 0.0)
            step *= 2
        beta_col.append(_sigmoid(ab))
        gcum_col.append(g)
        gcum_row.append(g.T)

    def pick(x, idx):
        return jnp.sum(jnp.where(lane == idx, x, 0.0), axis=-1, keepdims=True)

    def conv_silu(bi, col0):
        cols = slice(col0, col0 + hd)
        if shift_on_mxu:
            acc = x_ref[bi, :, cols].astype(F32) * cw_ref[keep:keep + 1, cols]
            for i in range(keep):
                acc = acc + shifted[bi][i * ch:(i + 1) * ch, cols] * cw_ref[i:i + 1, cols]
        else:
            acc = xbuf_ref[bi, top:top + ch, cols] * cw_ref[keep:keep + 1, cols]
            for i in range(keep):
                acc = acc + xbuf_ref[bi, top - keep + i:top - keep + i + ch, cols] * cw_ref[i:i + 1, cols]
        return _silu(acc)

    def l2norm(x):
        return x * lax.rsqrt(jnp.sum(x * x, axis=-1, keepdims=True) + L2_EPS)

    q = [l2norm(conv_silu(bi, h * hd)) * (hd ** -0.5) for bi, h in chains]
    k = [l2norm(conv_silu(bi, width + h * hd)) for bi, h in chains]
    v = [conv_silu(bi, 2 * width + h * hd) for bi, h in chains]
    gc = [pick(gcum_col[bi], h) for bi, h in chains]
    bc = [pick(beta_col[bi], B_HEADS + h) for bi, h in chains]
    gr = [gcum_row[bi][h:h + 1, :] for bi, h in chains]
    glast = [jnp.sum(jnp.where(lane_row == ch - 1, gr[i], 0.0), axis=-1, keepdims=True) for i in every]
    dec_incl = [jnp.exp(jnp.where(tri_incl, gc[i] - gr[i], NEG_INF)) for i in every]
    eg = [jnp.exp(gc[i]) for i in every]
    nbc = [-bc[i] for i in every]

    qk_kk = [_dot_nt(jnp.concatenate([q[i], k[i]], axis=0).astype(BF16), k[i].astype(BF16)) for i in every]
    a_qk = [qk_kk[i][0:ch] * dec_incl[i] for i in every]

    tp = [jnp.concatenate([(nbc[i] * qk_kk[i][ch:2 * ch]) * jnp.where(tri_strict, dec_incl[i], 0.0), eye], axis=1)
          for i in every]
    t_half = lax.broadcasted_iota(jnp.int32, (ch, 2 * ch), 1) >= ch
    for _ in range(max(1, (ch - 1).bit_length())):
        tp_b = [tp[i].astype(BF16) for i in every]
        tp = [_dot(tp_b[i][:, 0:ch], tp_b[i]) + jnp.where(t_half, tp[i], 0.0) for i in every]
    rhs = [jnp.concatenate([bc[i] * v[i], (bc[i] * eg[i]) * k[i]], axis=1).astype(BF16) for i in every]
    x_sol = [_dot(tp[i][:, ch:2 * ch].astype(BF16), rhs[i]) for i in every]

    state = [s_ref[bi, h] for bi, h in chains]
    wq = [_dot(jnp.concatenate([x_sol[i][:, hd:2 * hd], q[i] * eg[i]], axis=0).astype(BF16),
               state[i].astype(BF16)) for i in every]
    wv_b = [(x_sol[i][:, 0:hd] - wq[i][0:ch]).astype(BF16) for i in every]
    o = [wq[i][ch:2 * ch] + _dot(a_qk[i].astype(BF16), wv_b[i]) for i in every]
    s_new = [jnp.exp(glast[i]) * state[i]
             + _dot_tn((k[i] * jnp.exp(glast[i] - gc[i])).astype(BF16), wv_b[i]) for i in every]
    for i, (bi, h) in enumerate(chains):
        s_ref[bi, h] = s_new[i]
        on = o[i] * lax.rsqrt(jnp.mean(o[i] * o[i], axis=-1, keepdims=True) + RMS_EPS) * ng_ref[...]
        gate = _sigmoid(gate_ref[bi, :, h * hd:(h + 1) * hd].astype(F32))
        o_ref[bi, :, h * hd:(h + 1) * hd] = (on * gate).astype(o_ref.dtype)

    if shift_on_mxu:
        xbuf_ref[...] = x_ref[:, ch - top:ch, :]
    else:
        xbuf_ref[:, top - keep:top, :] = xbuf_ref[:, top + ch - keep:top + ch, :]


GDN_BATCHES = 2


def _gdn(z3, ab3, conv0, s0, conv_w, pcol, norm_g, ch):
    bsz, t, _ = z3.shape
    nb = GDN_BATCHES
    width3 = 3 * B_HEADS * B_HEAD_DIM
    width = B_HEADS * B_HEAD_DIM
    if z3.dtype == BF16:
        xbuf = pltpu.VMEM((nb, 2 * SUBLANES, width3), BF16)
    else:
        xbuf = pltpu.VMEM((nb, SUBLANES + ch, width3), F32)
    return pl.pallas_call(
        _gdn_kernel,
        grid=(bsz // nb, t // ch),
        in_specs=[
            pl.BlockSpec((nb, ch, width3), lambda b, c: (b, c, 1)),
            pl.BlockSpec((nb, ch, LANES), lambda b, c: (b, c, 0)),
            pl.BlockSpec((nb, ch, width), lambda b, c: (b, c, GATE_BLOCK0 + 1)),
            pl.BlockSpec((nb, B_CONV - 1, width3), lambda b, c: (b, 0, 0)),
            pl.BlockSpec((nb, B_HEADS, B_HEAD_DIM, B_HEAD_DIM), lambda b, c: (b, 0, 0, 0)),
            pl.BlockSpec(conv_w.shape, lambda b, c: (0, 0)),
            pl.BlockSpec(pcol.shape, lambda b, c: (0, 0)),
            pl.BlockSpec(norm_g.shape, lambda b, c: (0, 0)),
        ],
        out_specs=[
            pl.BlockSpec((nb, ch, width), lambda b, c: (b, c, 0)),
            pl.BlockSpec((nb, B_HEADS, B_HEAD_DIM, B_HEAD_DIM), lambda b, c: (b, 0, 0, 0)),
        ],
        out_shape=[
            jax.ShapeDtypeStruct((bsz, t, width), BF16),
            jax.ShapeDtypeStruct((bsz, B_HEADS, B_HEAD_DIM, B_HEAD_DIM), F32),
        ],
        scratch_shapes=[xbuf],
        compiler_params=_params("arbitrary", "arbitrary"),
        name="gdn",
    )(z3, ab3, z3, conv0, s0, conv_w, pcol, norm_g)


MEMATTN_ROWS = 512


def _memattn_kernel(q_ref, gate_ref, mk_ref, mv_ref, o_ref, kb_ref, vb_ref):
    dh = C_HEAD_DIM

    @pl.when(pl.program_id(1) == 0)
    def _():
        for h in range(C_HEADS):
            kb_ref[:, h * dh:(h + 1) * dh] = mk_ref[:, h, :].astype(BF16)
            vb_ref[:, h * dh:(h + 1) * dh] = mv_ref[:, h, :].astype(BF16)

    tq = q_ref.shape[0]
    rb = min(MEMATTN_ROWS, tq)
    for r0 in range(0, tq, rb):
        for h in range(C_HEADS):
            cols = slice(h * dh, (h + 1) * dh)
            q = q_ref[r0:r0 + rb, cols].astype(BF16)
            s = _dot_nt(q, kb_ref[:, cols]) * (dh ** -0.5)
            m = jnp.max(s, axis=-1, keepdims=True)
            p = jnp.exp(s - m)
            l = jnp.sum(p, axis=-1, keepdims=True)
            o = _dot(p.astype(BF16), vb_ref[:, cols]) * (1.0 / l)
            gate = _sigmoid(gate_ref[r0:r0 + rb, cols].astype(F32))
            o_ref[r0:r0 + rb, cols] = (o * gate).astype(o_ref.dtype)


def _memattn(z3, q_blk, mk, mv, tq):
    bsz, t, _ = z3.shape
    mem = mk.shape[1]
    width = C_HEADS * C_HEAD_DIM
    kv_spec = pl.BlockSpec((None, mem, C_HEADS, C_HEAD_DIM), lambda b, i: (b, 0, 0, 0))
    return pl.pallas_call(
        _memattn_kernel,
        grid=(bsz, t // tq),
        in_specs=[
            pl.BlockSpec((None, tq, width), lambda b, i: (b, i, q_blk)),
            pl.BlockSpec((None, tq, width), lambda b, i: (b, i, GATE_BLOCK0 + 2)),
            kv_spec,
            kv_spec,
        ],
        out_specs=pl.BlockSpec((None, tq, width), lambda b, i: (b, i, 0)),
        out_shape=jax.ShapeDtypeStruct((bsz, t, width), BF16),
        scratch_shapes=[pltpu.VMEM((mem, width), BF16), pltpu.VMEM((mem, width), BF16)],
        compiler_params=_params("arbitrary", "arbitrary"),
        name="memattn",
    )(z3, z3, mk, mv)


def _layer_norm(x, g, b):
    mu = jnp.mean(x, axis=-1, keepdims=True)
    xc = x - mu
    var = jnp.mean(xc * xc, axis=-1, keepdims=True)
    return xc * lax.rsqrt(var + LN_EPS) * g + b


FFN_PARTS = 2
FFN_HIDDEN_BLOCK = 1024


def _ffn_kernel(x_ref, oa_ref, ob_ref, oc_ref, wo_ref, w1_ref, w2_ref, vec_ref, b1_ref, y_ref, *, alpha, ff_blk):
    bm = x_ref.shape[0]
    rows = [pl.ds(r * (bm // FFN_PARTS), bm // FFN_PARTS) for r in range(FFN_PARTS)]
    parts = range(FFN_PARTS)

    merged = [(oa_ref[r, :].astype(F32) + ob_ref[r, :].astype(F32) + oc_ref[r, :].astype(F32)).astype(BF16)
              for r in rows]
    proj = [_dot(merged[i], wo_ref[...]) for i in parts]
    h = [_layer_norm(alpha * x_ref[rows[i], :] + proj[i], vec_ref[0:1, :], vec_ref[1:2, :]) for i in parts]
    hb = [h[i].astype(BF16) for i in parts]
    acc = [None] * FFN_PARTS
    for k0 in range(0, w1_ref.shape[1], ff_blk):
        f = [_dot(hb[i], w1_ref[:, k0:k0 + ff_blk]) + b1_ref[:, k0:k0 + ff_blk] for i in parts]
        f = [jnp.square(jnp.maximum(f[i], 0.0)).astype(BF16) for i in parts]
        d = [_dot(f[i], w2_ref[k0:k0 + ff_blk, :]) for i in parts]
        acc = [d[i] if acc[i] is None else acc[i] + d[i] for i in parts]
    for i in parts:
        y_ref[rows[i], :] = _layer_norm(alpha * h[i] + acc[i] + vec_ref[2:3, :], vec_ref[3:4, :], vec_ref[4:5, :])


def _ffn(x2, oa, ob, oc, wo, w1, w2, vec, b1, alpha, bm):
    m, d = x2.shape
    dff = w1.shape[1]
    const = dict(pipeline_mode=pl.Buffered(1))
    row = lambda i: (i, 0)
    return pl.pallas_call(
        functools.partial(_ffn_kernel, alpha=alpha, ff_blk=FFN_HIDDEN_BLOCK),
        grid=(m // bm,),
        in_specs=[
            pl.BlockSpec((bm, d), row),
            pl.BlockSpec((bm, d), row),
            pl.BlockSpec((bm, d), row),
            pl.BlockSpec((bm, d), row),
            pl.BlockSpec((d, d), lambda i: (0, 0), **const),
            pl.BlockSpec((d, dff), lambda i: (0, 0), **const),
            pl.BlockSpec((dff, d), lambda i: (0, 0), **const),
            pl.BlockSpec(vec.shape, lambda i: (0, 0), **const),
            pl.BlockSpec(b1.shape, lambda i: (0, 0), **const),
        ],
        out_specs=pl.BlockSpec((bm, d), row),
        out_shape=jax.ShapeDtypeStruct((m, d), F32),
        compiler_params=_params("arbitrary"),
        name="merge_ffn",
    )(x2, oa, ob, oc, wo, w1, w2, vec, b1)


INPROJ_ROWS = 2048


def _group_layer(x3, z_dtype, attn_fn, conv0, ssm0, mem_k, mem_v, lw, keep=None):
    bsz, t, d = x3.shape
    m = bsz * t
    ch = min(CHUNK, t)
    x2 = x3.reshape(m, d)
    if keep is not None:
        z3, ab3, *tails = _inproj_seq(x3, lw["w_in"], lw["wab"], lw["n_main"], keep)
        z2 = z3.reshape(m, -1)
    else:
        z2, ab = _inproj(x2, lw["w_in"], lw["wab"], lw["n_main"], z_dtype, min(INPROJ_ROWS, m), 2 * d)
        z3, ab3, tails = z2.reshape(bsz, t, -1), ab.reshape(bsz, t, LANES), None
    o_a = attn_fn(z3)
    o_b, ssm = _gdn(z3, ab3, conv0, ssm0, lw["conv_w"], lw["pcol"], lw["norm_g"], ch)
    o_c = _memattn(z3, QC_BLOCK, mem_k, mem_v, min(2048, t))
    y = _ffn(x2, o_a.reshape(m, d), o_b.reshape(m, d), o_c.reshape(m, d),
             lw["wo"], lw["w1"], lw["w2"], lw["vec"], lw["b1"], lw["alpha"], min(512, m))
    return y.reshape(bsz, t, d), z3, ssm, tails


def kernel(x_prompt, x_sample, cache_a_k, cache_a_v, state_b_conv, state_b_ssm, cache_mem_k, cache_mem_v, mem_prompt, w_in, w_b_conv, b_a_log, b_dt_bias, b_norm_g, a_rel_bias, w_mem_kv, w_out, ln1_g, ln1_b, w_ff1, b_ff1, w_ff2, b_ff2, ln2_g, ln2_b):
    depth = w_in.shape[0]
    bp, tp, d = x_prompt.shape
    bs, tn, _ = x_sample.shape
    window = A_LEFT_CHUNKS * CHUNK
    keep_p = min(window, tp)
    lc = cache_a_k.shape[2]
    n_main = w_in.shape[2] - 2 * B_HEADS
    a_width = A_HEADS * A_HEAD_DIM
    b_width3 = 3 * B_HEADS * B_HEAD_DIM
    alpha = (2.0 * depth) ** 0.25
    xp, xs = x_prompt, x_sample
    outs = [[] for _ in range(10)]
    for l in range(depth):
        q_scale = jnp.where(jnp.arange(w_in.shape[2]) < a_width, (A_HEAD_DIM ** -0.5) * LOG2E, 1.0)
        w_in_b = _scale_cast(w_in, l, q_scale.astype(F32).reshape(1, -1), d)
        vec = jnp.zeros((SUBLANES, d), F32)
        vec = vec.at[0].set(ln1_g[l]).at[1].set(ln1_b[l]).at[2].set(b_ff2[l]).at[3].set(ln2_g[l]).at[4].set(ln2_b[l])
        pcol = jnp.zeros((SUBLANES, LANES), F32)
        pcol = pcol.at[0, 0:B_HEADS].set(b_a_log[l]).at[1, 0:B_HEADS].set(b_dt_bias[l])
        lw = dict(
            w_in=w_in_b, n_main=n_main,
            wab=jnp.pad(w_in_b[:, n_main:], ((0, 0), (0, LANES - 2 * B_HEADS))),
            conv_w=w_b_conv[l],
            pcol=pcol,
            norm_g=b_norm_g[l].reshape(1, B_HEAD_DIM),
            wo=w_out[l].astype(BF16), w1=w_ff1[l].astype(BF16), w2=w_ff2[l].astype(BF16),
            vec=vec, b1=b_ff1[l].reshape(1, -1), alpha=alpha,
        )
        band = window + CHUNK
        bias_tab = _bias_table(a_rel_bias[l], CHUNK, band)
        bias_cols = bias_tab.shape[2]
        col0 = band - bias_cols

        bias_p = bias_tab.reshape(A_HEADS // A_GROUP, A_GROUP * CHUNK, bias_cols)
        mk, mv = _mem_kv(mem_prompt, w_mem_kv[l].astype(BF16))
        conv0 = jnp.zeros((bp, B_CONV - 1, b_width3), F32)
        ssm0 = jnp.zeros((bp, B_HEADS, B_HEAD_DIM, B_HEAD_DIM), F32)
        xp, _, ssm_p, (k_tail, v_tail, conv_tail) = _group_layer(
            xp, BF16, lambda z3: _attn_prompt(z3, bias_p, CHUNK, window), conv0, ssm0, mk, mv, lw, keep=keep_p)
        outs[0].append(k_tail.reshape(bp, keep_p, A_HEADS, A_HEAD_DIM))
        outs[1].append(v_tail.reshape(bp, keep_p, A_HEADS, A_HEAD_DIM))
        outs[2].append(conv_tail[:, SUBLANES - (B_CONV - 1):])
        outs[3].append(ssm_p)
        outs[4].append(mk)
        outs[5].append(mv)

        bias_s = bias_tab[:, 0:tn, 0:lc + tn - col0].reshape(A_HEADS // A_GROUP, A_GROUP * tn, lc + tn - col0)
        kt = cache_a_k[l].transpose(0, 2, 3, 1).reshape(bs, a_width, lc)
        vt = cache_a_v[l].transpose(0, 2, 3, 1).reshape(bs, a_width, lc)
        advanced = {}

        def attn_sample(z3):
            knt = z3[:, :, a_width:2 * a_width].transpose(0, 2, 1)
            vnt = z3[:, :, 2 * a_width:3 * a_width].transpose(0, 2, 1)
            o_a, advanced["k"], advanced["v"] = _attn_sample(z3, knt, vnt, kt, vt, bias_s)
            return o_a

        xs, zs3, ssm_s, _ = _group_layer(
            xs, F32, attn_sample, state_b_conv[l], state_b_ssm[l], cache_mem_k[l], cache_mem_v[l], lw)
        outs[6].append(advanced["k"].reshape(bs, A_HEADS, A_HEAD_DIM, lc).transpose(0, 3, 1, 2))
        outs[7].append(advanced["v"].reshape(bs, A_HEADS, A_HEAD_DIM, lc).transpose(0, 3, 1, 2))
        qkvb_s = zs3[:, :, 3 * a_width:3 * a_width + b_width3]
        outs[8].append(jnp.concatenate([state_b_conv[l], qkvb_s], axis=1)[:, tn:])
        outs[9].append(ssm_s)
    return (xp, xs) + tuple(jnp.stack(o) for o in outs)
```

```python
import functools

import jax
import jax.numpy as jnp
from jax import lax
from jax.experimental import pallas as pl
from jax.experimental.pallas import tpu as pltpu

F32 = jnp.float32
BF16 = jnp.bfloat16

CHUNK = 64
A_HEADS = 16
A_HEAD_DIM = 64
A_GROUP = 4
A_LEFT_CHUNKS = 8
A_REL_CLIP = 128
B_HEADS = 8
B_HEAD_DIM = 128
B_CONV = 4
C_HEADS = 4
C_HEAD_DIM = 256
N_BRANCH = 3
LN_EPS = 1e-5
RMS_EPS = 1e-6
L2_EPS = 1e-6
NEG_INF = -1e30

LANES = 128
SUBLANES = 8
VMEM_LIMIT_BYTES = 56 * 1024 * 1024


def _params(*sem):
    return pltpu.CompilerParams(dimension_semantics=sem, vmem_limit_bytes=VMEM_LIMIT_BYTES)


def _dot(a, b):
    return jnp.dot(a, b, preferred_element_type=F32)


def _dot_nt(a, b):
    return lax.dot_general(a, b, (((1,), (1,)), ((), ())), preferred_element_type=F32)


def _dot_tn(a, b):
    return lax.dot_general(a, b, (((0,), (0,)), ((), ())), preferred_element_type=F32)


def _sigmoid(x):
    return 0.5 + 0.5 * jnp.tanh(0.5 * x)


def _silu(x):
    hx = 0.5 * x
    return hx + hx * jnp.tanh(hx)


def _scale_cast_kernel(w_ref, o_ref, *, row_scales):
    rows = lax.broadcasted_iota(jnp.int32, w_ref.shape, 0) + pl.program_id(0) * w_ref.shape[0]
    w = w_ref[...]
    for start, stop, scale in row_scales:
        w = jnp.where((rows >= start) & (rows < stop), w * scale, w)
    o_ref[...] = w.astype(o_ref.dtype)


def _scale_cast(w_t, layer, row_scales, bn):
    _, n, k = w_t.shape
    return pl.pallas_call(
        functools.partial(_scale_cast_kernel, row_scales=row_scales),
        grid=(pl.cdiv(n, bn),),
        in_specs=[pl.BlockSpec((None, bn, k), lambda j: (layer, j, 0))],
        out_specs=pl.BlockSpec((bn, k), lambda j: (j, 0)),
        out_shape=jax.ShapeDtypeStruct((n, k), BF16),
        compiler_params=_params("arbitrary"),
        name="scale_cast",
    )(w_t)


K_BLOCK, V_BLOCK, CONV_BLOCK0, CONV_BLOCKS = 1, 2, 3, 3
QC_BLOCK, GATE_BLOCK0 = 6, 7


def _inproj_kernel(x_ref, w_ref, wab_ref, z_ref, ab_ref, xb_ref):
    @pl.when(pl.program_id(1) == 0)
    def _():
        xb = x_ref[...].astype(BF16)
        xb_ref[...] = xb
        ab_ref[...] = _dot_nt(xb, wab_ref[...])

    z_ref[...] = _dot_nt(xb_ref[...], w_ref[...]).astype(z_ref.dtype)


def _inproj(x2, w, wab, n_main, out_dtype, bm, bn):
    m, d = x2.shape
    return pl.pallas_call(
        _inproj_kernel,
        grid=(m // bm, n_main // bn),
        in_specs=[
            pl.BlockSpec((bm, d), lambda i, j: (i, 0)),
            pl.BlockSpec((bn, d), lambda i, j: (j, 0)),
            pl.BlockSpec((LANES, d), lambda i, j: (0, 0)),
        ],
        out_specs=[
            pl.BlockSpec((bm, bn), lambda i, j: (i, j)),
            pl.BlockSpec((bm, LANES), lambda i, j: (i, 0)),
        ],
        out_shape=[
            jax.ShapeDtypeStruct((m, n_main), out_dtype),
            jax.ShapeDtypeStruct((m, LANES), F32),
        ],
        scratch_shapes=[pltpu.VMEM((bm, d), BF16)],
        compiler_params=_params("arbitrary", "arbitrary"),
        name="inproj",
    )(x2, w, wab)


INPROJ_SUB = 2 * B_HEAD_DIM


def _inproj_seq_kernel(x_ref, w_ref, wab_ref, z_ref, ab_ref, kt_ref, vt_ref, ct_ref, xb_ref):
    j = pl.program_id(1)
    bm, bn = z_ref.shape
    keep = kt_ref.shape[0]

    @pl.when(j == 0)
    def _():
        xb = x_ref[...].astype(BF16)
        xb_ref[...] = xb
        ab_ref[...] = _dot_nt(xb, wab_ref[...])
        kt_ref[...] = jnp.zeros(kt_ref.shape, F32)
        vt_ref[...] = jnp.zeros(vt_ref.shape, F32)

    tails = []
    for n0 in range(0, bn, INPROJ_SUB):
        cols = slice(n0, n0 + INPROJ_SUB)
        acc = _dot_nt(xb_ref[...], w_ref[cols, :])
        z_ref[:, cols] = acc.astype(z_ref.dtype)
        tail = acc[bm - keep:, :]
        kt_ref[:, cols] = jnp.where(j == K_BLOCK, tail, kt_ref[:, cols])
        vt_ref[:, cols] = jnp.where(j == V_BLOCK, tail, vt_ref[:, cols])
        tails.append(acc[bm - SUBLANES:, :])

    @pl.when((j >= CONV_BLOCK0) & (j < CONV_BLOCK0 + CONV_BLOCKS))
    def _():
        ct_ref[...] = jnp.concatenate(tails, axis=1)


def _inproj_seq(x3, w, wab, n_main, keep):
    nseq, t, d = x3.shape
    bm, bn = t, d
    conv_blk = lambda i, j: jnp.clip(j - CONV_BLOCK0, 0, CONV_BLOCKS - 1)
    return pl.pallas_call(
        _inproj_seq_kernel,
        grid=(nseq, n_main // bn),
        in_specs=[
            pl.BlockSpec((None, bm, d), lambda i, j: (i, 0, 0)),
            pl.BlockSpec((bn, d), lambda i, j: (j, 0)),
            pl.BlockSpec((LANES, d), lambda i, j: (0, 0)),
        ],
        out_specs=[
            pl.BlockSpec((None, bm, bn), lambda i, j: (i, 0, j)),
            pl.BlockSpec((None, bm, LANES), lambda i, j: (i, 0, 0)),
            pl.BlockSpec((None, keep, bn), lambda i, j: (i, 0, 0)),
            pl.BlockSpec((None, keep, bn), lambda i, j: (i, 0, 0)),
            pl.BlockSpec((None, SUBLANES, bn), lambda i, j: (i, 0, conv_blk(i, j))),
        ],
        out_shape=[
            jax.ShapeDtypeStruct((nseq, t, n_main), BF16),
            jax.ShapeDtypeStruct((nseq, t, LANES), F32),
            jax.ShapeDtypeStruct((nseq, keep, bn), F32),
            jax.ShapeDtypeStruct((nseq, keep, bn), F32),
            jax.ShapeDtypeStruct((nseq, SUBLANES, CONV_BLOCKS * bn), F32),
        ],
        scratch_shapes=[pltpu.VMEM((bm, d), BF16)],
        compiler_params=_params("arbitrary", "arbitrary"),
        name="inproj_seq",
    )(x3, w, wab)


def _mem_kv_kernel(x_ref, w_ref, k_ref, v_ref):
    acc = _dot(x_ref[...].astype(BF16), w_ref[...])
    _, heads, dh = k_ref.shape
    for h in range(heads):
        k_ref[:, h, :] = acc[:, h * dh:(h + 1) * dh]
        v_ref[:, h, :] = acc[:, (heads + h) * dh:(heads + h + 1) * dh]


def _mem_kv(mem, w):
    bsz, mt, d = mem.shape
    out = jax.ShapeDtypeStruct((bsz, mt, C_HEADS, C_HEAD_DIM), F32)
    return pl.pallas_call(
        _mem_kv_kernel,
        grid=(bsz,),
        in_specs=[
            pl.BlockSpec((None, mt, d), lambda b: (b, 0, 0)),
            pl.BlockSpec(w.shape, lambda b: (0, 0)),
        ],
        out_specs=[pl.BlockSpec((None, mt, C_HEADS, C_HEAD_DIM), lambda b: (b, 0, 0, 0))] * 2,
        out_shape=[out, out],
        compiler_params=_params("arbitrary"),
        name="mem_kv",
    )(mem, w)


LOG2E = 1.4426950408889634


def _bias_col0(window):
    return (window - A_REL_CLIP + 1) // LANES * LANES


def _bias_table_kernel(rev_ref, o_ref):
    heads, tq, cols = o_ref.shape
    width = rev_ref.shape[1]
    lane = lax.broadcasted_iota(jnp.int32, (heads, width), 1)
    w = jnp.where(lane < cols, (rev_ref[...] - rev_ref[:, 0:1]) * LOG2E, 0.0)
    for h in range(heads):
        rows = jnp.broadcast_to(w[h:h + 1, :], (tq, width))
        o_ref[h] = pltpu.roll(rows, 0, axis=1, stride=1, stride_axis=0)[:, 0:cols]


def _bias_table(rel_bias, tq, band):
    heads = rel_bias.shape[0]
    window = band - tq
    col0 = _bias_col0(window)
    cols = band - col0
    width = 2 * A_REL_CLIP
    assert window - col0 == A_REL_CLIP and window - (band - 1) >= -A_REL_CLIP
    assert cols + tq <= width and width % LANES == 0
    rev = rel_bias[:, ::-1][:, 0:width]
    return pl.pallas_call(
        _bias_table_kernel,
        out_shape=jax.ShapeDtypeStruct((heads, tq, cols), F32),
        compiler_params=pltpu.CompilerParams(vmem_limit_bytes=VMEM_LIMIT_BYTES),
        name="bias_table",
    )(rev)


A_GROUP_WIDTH = A_GROUP * A_HEAD_DIM


def _lane_head(tq):
    return jnp.right_shift(lax.broadcasted_iota(jnp.int32, (tq, A_GROUP_WIDTH), 1), A_HEAD_DIM.bit_length() - 1)


def _attn_scores(qg, kg, bias_g, valid_from):
    lane_head = _lane_head(qg.shape[0])
    qg = qg.astype(BF16)
    zero = jnp.zeros_like(qg)
    qm = jnp.concatenate([jnp.where(lane_head == h, qg, zero) for h in range(A_GROUP)], axis=0)
    s = _dot_nt(qm, kg)
    col0 = s.shape[1] - bias_g.shape[1]
    s = jnp.concatenate([s[:, 0:col0], s[:, col0:] + bias_g], axis=1)
    if valid_from is not None:
        col = lax.broadcasted_iota(jnp.int32, s.shape, 1)
        s = jnp.where(col >= valid_from, s, NEG_INF)
    m = jnp.max(s, axis=-1, keepdims=True)
    p = jnp.exp2(s - m)
    return p.astype(BF16), 1.0 / jnp.sum(p, axis=-1, keepdims=True)


def _attn_values(p, rl, vg):
    tq = p.shape[0] // A_GROUP
    o = _dot(p, vg)
    if rl.shape[1] == 1:
        o = o * rl
    else:
        o = o * jnp.concatenate([rl] * (A_GROUP_WIDTH // LANES), axis=1)
    lane_head = _lane_head(tq)
    out = jnp.zeros((tq, A_GROUP_WIDTH), F32)
    for h in range(A_GROUP):
        out = jnp.where(lane_head == h, o[h * tq:(h + 1) * tq], out)
    return out


A_CHUNKS_PER_STEP = 8


def _attn_prompt_kernel(q_ref, k_ref, v_ref, gate_ref, bias_ref, o_ref, kpad_ref, vpad_ref, *, window, tq):
    s = pl.program_id(1)
    band = window + tq
    gw = A_GROUP_WIDTH
    cps = q_ref.shape[0] // tq

    @pl.when(s == 0)
    def _():
        zeros = jnp.zeros((window, kpad_ref.shape[1]), BF16)
        kpad_ref[0:window, :] = zeros
        vpad_ref[0:window, :] = zeros
        kpad_ref[window:, :] = k_ref[...]
        vpad_ref[window:, :] = v_ref[...]

    for j in range(cps):
        c = s * cps + j
        start = pl.multiple_of(c * tq, tq)
        rows = slice(j * tq, (j + 1) * tq)
        for g in range(A_HEADS // A_GROUP):
            cols = slice(g * gw, (g + 1) * gw)
            p, rl = _attn_scores(q_ref[rows, cols], kpad_ref[pl.ds(start, band), cols], bias_ref[g],
                                 window - c * tq)
            out = _attn_values(p, rl, vpad_ref[pl.ds(start, band), cols])
            o_ref[rows, cols] = (out * _sigmoid(gate_ref[rows, cols].astype(F32))).astype(o_ref.dtype)


def _attn_prompt(z3, bias_g, tq, window):
    bsz, t, _ = z3.shape
    width = A_HEADS * A_HEAD_DIM
    rows = tq * min(A_CHUNKS_PER_STEP, t // tq)
    return pl.pallas_call(
        functools.partial(_attn_prompt_kernel, window=window, tq=tq),
        grid=(bsz, t // rows),
        in_specs=[
            pl.BlockSpec((None, rows, width), lambda b, s: (b, s, 0)),
            pl.BlockSpec((None, t, width), lambda b, s: (b, 0, K_BLOCK)),
            pl.BlockSpec((None, t, width), lambda b, s: (b, 0, V_BLOCK)),
            pl.BlockSpec((None, rows, width), lambda b, s: (b, s, GATE_BLOCK0)),
            pl.BlockSpec(bias_g.shape, lambda b, s: (0, 0, 0)),
        ],
        out_specs=pl.BlockSpec((None, rows, width), lambda b, s: (b, s, 0)),
        out_shape=jax.ShapeDtypeStruct((bsz, t, width), BF16),
        scratch_shapes=[pltpu.VMEM((t + window, width), BF16), pltpu.VMEM((t + window, width), BF16)],
        compiler_params=_params("arbitrary", "arbitrary"),
        name="attn_prompt",
    )(z3, z3, z3, z3, bias_g)


def _attn_sample_kernel(q_ref, kn_ref, vn_ref, gate_ref, kt_ref, vt_ref, bias_ref, o_ref, ko_ref, vo_ref):
    tq = q_ref.shape[0]
    lc = kt_ref.shape[1]
    gw = A_GROUP_WIDTH
    lane_head = _lane_head(tq)
    ko_ref[...] = jnp.concatenate([kt_ref[:, tq:], kn_ref[...].T], axis=1)
    vo_ref[...] = jnp.concatenate([vt_ref[:, tq:], vn_ref[...].T], axis=1)
    for g in range(A_HEADS // A_GROUP):
        cols = slice(g * gw, (g + 1) * gw)
        qg = q_ref[:, cols].astype(BF16)
        zero = jnp.zeros_like(qg)
        qm = jnp.concatenate([jnp.where(lane_head == h, qg, zero) for h in range(A_GROUP)], axis=0)
        s_c = _dot(qm, kt_ref[cols, :].astype(BF16))
        s_n = _dot_nt(qm, kn_ref[:, cols].astype(BF16))
        bias = bias_ref[g]
        col0 = lc + tq - bias.shape[1]
        s_c = jnp.concatenate([s_c[:, 0:col0], s_c[:, col0:] + bias[:, 0:lc - col0]], axis=1)
        s_n = s_n + bias[:, lc - col0:]
        m = jnp.maximum(jnp.max(s_c, axis=-1, keepdims=True), jnp.max(s_n, axis=-1, keepdims=True))
        p_c = jnp.exp2(s_c - m)
        p_n = jnp.exp2(s_n - m)
        rl = 1.0 / (jnp.sum(p_c, axis=-1, keepdims=True) + jnp.sum(p_n, axis=-1, keepdims=True))
        o = (_dot_nt(p_c.astype(BF16), vt_ref[cols, :].astype(BF16))
             + _dot(p_n.astype(BF16), vn_ref[:, cols].astype(BF16))) * rl
        out = jnp.zeros((tq, gw), F32)
        for h in range(A_GROUP):
            out = jnp.where(lane_head == h, o[h * tq:(h + 1) * tq], out)
        o_ref[:, cols] = (out * _sigmoid(gate_ref[:, cols].astype(F32))).astype(o_ref.dtype)


def _attn_sample(z3, kt, vt, bias_g):
    bsz, tq, _ = z3.shape
    width, lc = kt.shape[1:]
    cache = jax.ShapeDtypeStruct(kt.shape, kt.dtype)
    cache_spec = pl.BlockSpec((None, width, lc), lambda b: (b, 0, 0))
    return pl.pallas_call(
        _attn_sample_kernel,
        grid=(bsz,),
        in_specs=[
            pl.BlockSpec((None, tq, width), lambda b: (b, 0, 0)),
            pl.BlockSpec((None, tq, width), lambda b: (b, 0, K_BLOCK)),
            pl.BlockSpec((None, tq, width), lambda b: (b, 0, V_BLOCK)),
            pl.BlockSpec((None, tq, width), lambda b: (b, 0, GATE_BLOCK0)),
            cache_spec, cache_spec,
            pl.BlockSpec(bias_g.shape, lambda b: (0, 0, 0)),
        ],
        out_specs=[pl.BlockSpec((None, tq, width), lambda b: (b, 0, 0)), cache_spec, cache_spec],
        out_shape=[jax.ShapeDtypeStruct((bsz, tq, width), BF16), cache, cache],
        compiler_params=_params("arbitrary"),
        name="attn_sample",
    )(z3, z3, z3, z3, kt, vt, bias_g)


def _gdn_kernel(x_ref, ab_ref, gate_ref, conv0_ref, s0_ref, cw_ref, pcol_ref, ng_ref,
                o_ref, s_ref, xbuf_ref):
    c = pl.program_id(1)
    nb, ch, _ = x_ref.shape
    hd = B_HEAD_DIM
    width = B_HEADS * hd
    keep = B_CONV - 1
    shift_on_mxu = x_ref.dtype == BF16
    top = xbuf_ref.shape[1] - (0 if shift_on_mxu else ch)
    chains = [(bi, h) for bi in range(nb) for h in range(B_HEADS)]
    every = range(len(chains))

    @pl.when(c == 0)
    def _():
        hist = jnp.concatenate([jnp.zeros((nb, top - keep, xbuf_ref.shape[2]), F32),
                                conv0_ref[...].astype(F32)], axis=1)
        xbuf_ref[:, 0:top, :] = hist.astype(xbuf_ref.dtype)
        s_ref[...] = s0_ref[...].astype(F32)

    if shift_on_mxu:
        t_r = lax.broadcasted_iota(jnp.int32, (ch, top + ch), 0)
        u_c = lax.broadcasted_iota(jnp.int32, (ch, top + ch), 1)
        shift01 = jnp.concatenate([jnp.where(u_c == top + t_r - (keep - i), 1.0, 0.0) for i in range(keep)],
                                  axis=0).astype(BF16)
        shifted = [_dot(shift01, jnp.concatenate([xbuf_ref[bi], x_ref[bi]], axis=0)) for bi in range(nb)]
    else:
        xbuf_ref[:, top:top + ch, :] = x_ref[...]

    t_i = lax.broadcasted_iota(jnp.int32, (ch, ch), 0)
    s_i = lax.broadcasted_iota(jnp.int32, (ch, ch), 1)
    tri_incl = (t_i >= s_i)
    tri_strict = (t_i > s_i)
    eye = jnp.where(t_i == s_i, 1.0, 0.0)
    lane = lax.broadcasted_iota(jnp.int32, (ch, LANES), 1)
    lane_row = lax.broadcasted_iota(jnp.int32, (1, ch), 1)
    frame = lax.broadcasted_iota(jnp.int32, (ch, LANES), 0)

    gcum_col, beta_col, gcum_row = [], [], []
    for bi in range(nb):
        ab = ab_ref[bi]
        g = -jnp.exp(pcol_ref[0:1, :]) * jax.nn.softplus(ab + pcol_ref[1:2, :])
        step = 1
        while step < ch:
            g = g + jnp.where(frame >= step, pltpu.roll(g, step, axis=0), 0.0)
            step *= 2
        beta_col.append(_sigmoid(ab))
        gcum_col.append(g)
        gcum_row.append(g.T)

    def pick(x, idx):
        return jnp.sum(jnp.where(lane == idx, x, 0.0), axis=-1, keepdims=True)

    def conv_silu(bi, col0):
        cols = slice(col0, col0 + hd)
        if shift_on_mxu:
            acc = x_ref[bi, :, cols].astype(F32) * cw_ref[keep:keep + 1, cols]
            for i in range(keep):
                acc = acc + shifted[bi][i * ch:(i + 1) * ch, cols] * cw_ref[i:i + 1, cols]
        else:
            acc = xbuf_ref[bi, top:top + ch, cols] * cw_ref[keep:keep + 1, cols]
            for i in range(keep):
                acc = acc + xbuf_ref[bi, top - keep + i:top - keep + i + ch, cols] * cw_ref[i:i + 1, cols]
        return _silu(acc)

    def l2norm(x):
        return x * lax.rsqrt(jnp.sum(x * x, axis=-1, keepdims=True) + L2_EPS)

    q = [l2norm(conv_silu(bi, h * hd)) * (hd ** -0.5) for bi, h in chains]
    k = [l2norm(conv_silu(bi, width + h * hd)) for bi, h in chains]
    v = [conv_silu(bi, 2 * width + h * hd) for bi, h in chains]
    gc = [pick(gcum_col[bi], h) for bi, h in chains]
    bc = [pick(beta_col[bi], B_HEADS + h) for bi, h in chains]
    gr = [gcum_row[bi][h:h + 1, :] for bi, h in chains]
    glast = [jnp.sum(jnp.where(lane_row == ch - 1, gr[i], 0.0), axis=-1, keepdims=True) for i in every]
    dec_incl = [jnp.exp(jnp.where(tri_incl, gc[i] - gr[i], NEG_INF)) for i in every]
    eg = [jnp.exp(gc[i]) for i in every]
    nbc = [-bc[i] for i in every]

    qk_kk = [_dot_nt(jnp.concatenate([q[i], k[i]], axis=0).astype(BF16), k[i].astype(BF16)) for i in every]
    a_qk = [qk_kk[i][0:ch] * dec_incl[i] for i in every]

    tp = [jnp.concatenate([(nbc[i] * qk_kk[i][ch:2 * ch]) * jnp.where(tri_strict, dec_incl[i], 0.0), eye], axis=1)
          for i in every]
    t_half = lax.broadcasted_iota(jnp.int32, (ch, 2 * ch), 1) >= ch
    for _ in range(max(1, (ch - 1).bit_length())):
        tp_b = [tp[i].astype(BF16) for i in every]
        tp = [_dot(tp_b[i][:, 0:ch], tp_b[i]) + jnp.where(t_half, tp[i], 0.0) for i in every]
    rhs = [jnp.concatenate([bc[i] * v[i], (bc[i] * eg[i]) * k[i]], axis=1).astype(BF16) for i in every]
    x_sol = [_dot(tp[i][:, ch:2 * ch].astype(BF16), rhs[i]) for i in every]

    state = [s_ref[bi, h] for bi, h in chains]
    wq = [_dot(jnp.concatenate([x_sol[i][:, hd:2 * hd], q[i] * eg[i]], axis=0).astype(BF16),
               state[i].astype(BF16)) for i in every]
    wv_b = [(x_sol[i][:, 0:hd] - wq[i][0:ch]).astype(BF16) for i in every]
    o = [wq[i][ch:2 * ch] + _dot(a_qk[i].astype(BF16), wv_b[i]) for i in every]
    s_new = [jnp.exp(glast[i]) * state[i]
             + _dot_tn((k[i] * jnp.exp(glast[i] - gc[i])).astype(BF16), wv_b[i]) for i in every]
    for i, (bi, h) in enumerate(chains):
        s_ref[bi, h] = s_new[i]
        on = o[i] * lax.rsqrt(jnp.mean(o[i] * o[i], axis=-1, keepdims=True) + RMS_EPS) * ng_ref[...]
        gate = _sigmoid(gate_ref[bi, :, h * hd:(h + 1) * hd].astype(F32))
        o_ref[bi, :, h * hd:(h + 1) * hd] = (on * gate).astype(o_ref.dtype)

    if shift_on_mxu:
        xbuf_ref[...] = x_ref[:, ch - top:ch, :]
    else:
        xbuf_ref[:, top - keep:top, :] = xbuf_ref[:, top + ch - keep:top + ch, :]


GDN_BATCHES = 2


def _gdn(z3, ab3, conv0, s0, conv_w, pcol, norm_g, ch):
    bsz, t, _ = z3.shape
    nb = GDN_BATCHES
    width3 = 3 * B_HEADS * B_HEAD_DIM
    width = B_HEADS * B_HEAD_DIM
    if z3.dtype == BF16:
        xbuf = pltpu.VMEM((nb, 2 * SUBLANES, width3), BF16)
    else:
        xbuf = pltpu.VMEM((nb, SUBLANES + ch, width3), F32)
    return pl.pallas_call(
        _gdn_kernel,
        grid=(bsz // nb, t // ch),
        in_specs=[
            pl.BlockSpec((nb, ch, width3), lambda b, c: (b, c, 1)),
            pl.BlockSpec((nb, ch, LANES), lambda b, c: (b, c, 0)),
            pl.BlockSpec((nb, ch, width), lambda b, c: (b, c, GATE_BLOCK0 + 1)),
            pl.BlockSpec((nb, B_CONV - 1, width3), lambda b, c: (b, 0, 0)),
            pl.BlockSpec((nb, B_HEADS, B_HEAD_DIM, B_HEAD_DIM), lambda b, c: (b, 0, 0, 0)),
            pl.BlockSpec(conv_w.shape, lambda b, c: (0, 0)),
            pl.BlockSpec(pcol.shape, lambda b, c: (0, 0)),
            pl.BlockSpec(norm_g.shape, lambda b, c: (0, 0)),
        ],
        out_specs=[
            pl.BlockSpec((nb, ch, width), lambda b, c: (b, c, 0)),
            pl.BlockSpec((nb, B_HEADS, B_HEAD_DIM, B_HEAD_DIM), lambda b, c: (b, 0, 0, 0)),
        ],
        out_shape=[
            jax.ShapeDtypeStruct((bsz, t, width), BF16),
            jax.ShapeDtypeStruct((bsz, B_HEADS, B_HEAD_DIM, B_HEAD_DIM), F32),
        ],
        scratch_shapes=[xbuf],
        compiler_params=_params("arbitrary", "arbitrary"),
        name="gdn",
    )(z3, ab3, z3, conv0, s0, conv_w, pcol, norm_g)


MEMATTN_ROWS = 512


def _memattn_kernel(q_ref, gate_ref, mk_ref, mv_ref, o_ref, kb_ref, vb_ref):
    dh = C_HEAD_DIM

    @pl.when(pl.program_id(1) == 0)
    def _():
        for h in range(C_HEADS):
            kb_ref[:, h * dh:(h + 1) * dh] = mk_ref[:, h, :].astype(BF16)
            vb_ref[:, h * dh:(h + 1) * dh] = mv_ref[:, h, :].astype(BF16)

    tq = q_ref.shape[0]
    rb = min(MEMATTN_ROWS, tq)
    for r0 in range(0, tq, rb):
        for h in range(C_HEADS):
            cols = slice(h * dh, (h + 1) * dh)
            q = q_ref[r0:r0 + rb, cols].astype(BF16)
            s = _dot_nt(q, kb_ref[:, cols])
            m = jnp.max(s, axis=-1, keepdims=True)
            p = jnp.exp2(s - m)
            l = jnp.sum(p, axis=-1, keepdims=True)
            o = _dot(p.astype(BF16), vb_ref[:, cols]) * (1.0 / l)
            gate = _sigmoid(gate_ref[r0:r0 + rb, cols].astype(F32))
            o_ref[r0:r0 + rb, cols] = (o * gate).astype(o_ref.dtype)


def _memattn(z3, q_blk, mk, mv, tq):
    bsz, t, _ = z3.shape
    mem = mk.shape[1]
    width = C_HEADS * C_HEAD_DIM
    kv_spec = pl.BlockSpec((None, mem, C_HEADS, C_HEAD_DIM), lambda b, i: (b, 0, 0, 0))
    return pl.pallas_call(
        _memattn_kernel,
        grid=(bsz, t // tq),
        in_specs=[
            pl.BlockSpec((None, tq, width), lambda b, i: (b, i, q_blk)),
            pl.BlockSpec((None, tq, width), lambda b, i: (b, i, GATE_BLOCK0 + 2)),
            kv_spec,
            kv_spec,
        ],
        out_specs=pl.BlockSpec((None, tq, width), lambda b, i: (b, i, 0)),
        out_shape=jax.ShapeDtypeStruct((bsz, t, width), BF16),
        scratch_shapes=[pltpu.VMEM((mem, width), BF16), pltpu.VMEM((mem, width), BF16)],
        compiler_params=_params("arbitrary", "arbitrary"),
        name="memattn",
    )(z3, z3, mk, mv)


def _layer_norm(x, g, b):
    mu = jnp.mean(x, axis=-1, keepdims=True)
    xc = x - mu
    var = jnp.mean(xc * xc, axis=-1, keepdims=True)
    return xc * lax.rsqrt(var + LN_EPS) * g + b


FFN_PARTS = 2
FFN_HIDDEN_BLOCK = 1024


def _ffn_kernel(x_ref, oa_ref, ob_ref, oc_ref, wo_ref, w1_ref, w2_ref, vec_ref, b1_ref, y_ref, *, alpha, ff_blk):
    bm = x_ref.shape[0]
    rows = [pl.ds(r * (bm // FFN_PARTS), bm // FFN_PARTS) for r in range(FFN_PARTS)]
    parts = range(FFN_PARTS)

    merged = [(oa_ref[r, :].astype(F32) + ob_ref[r, :].astype(F32) + oc_ref[r, :].astype(F32)).astype(BF16)
              for r in rows]
    proj = [_dot(merged[i], wo_ref[...]) for i in parts]
    h = [_layer_norm(alpha * x_ref[rows[i], :] + proj[i], vec_ref[0:1, :], vec_ref[1:2, :]) for i in parts]
    hb = [h[i].astype(BF16) for i in parts]
    acc = [None] * FFN_PARTS
    for k0 in range(0, w1_ref.shape[1], ff_blk):
        f = [_dot(hb[i], w1_ref[:, k0:k0 + ff_blk]) + b1_ref[:, k0:k0 + ff_blk] for i in parts]
        f = [jnp.square(jnp.maximum(f[i], 0.0)).astype(BF16) for i in parts]
        d = [_dot(f[i], w2_ref[k0:k0 + ff_blk, :]) for i in parts]
        acc = [d[i] if acc[i] is None else acc[i] + d[i] for i in parts]
    for i in parts:
        y_ref[rows[i], :] = _layer_norm(alpha * h[i] + acc[i] + vec_ref[2:3, :], vec_ref[3:4, :], vec_ref[4:5, :])


def _ffn(x2, oa, ob, oc, wo, w1, w2, vec, b1, alpha, bm):
    m, d = x2.shape
    dff = w1.shape[1]
    const = dict(pipeline_mode=pl.Buffered(1))
    row = lambda i: (i, 0)
    return pl.pallas_call(
        functools.partial(_ffn_kernel, alpha=alpha, ff_blk=FFN_HIDDEN_BLOCK),
        grid=(m // bm,),
        in_specs=[
            pl.BlockSpec((bm, d), row),
            pl.BlockSpec((bm, d), row),
            pl.BlockSpec((bm, d), row),
            pl.BlockSpec((bm, d), row),
            pl.BlockSpec((d, d), lambda i: (0, 0), **const),
            pl.BlockSpec((d, dff), lambda i: (0, 0), **const),
            pl.BlockSpec((dff, d), lambda i: (0, 0), **const),
            pl.BlockSpec(vec.shape, lambda i: (0, 0), **const),
            pl.BlockSpec(b1.shape, lambda i: (0, 0), **const),
        ],
        out_specs=pl.BlockSpec((bm, d), row),
        out_shape=jax.ShapeDtypeStruct((m, d), F32),
        compiler_params=_params("arbitrary"),
        name="merge_ffn",
    )(x2, oa, ob, oc, wo, w1, w2, vec, b1)


INPROJ_ROWS = 2048


def _group_layer(x3, z_dtype, attn_fn, conv0, ssm0, mem_k, mem_v, lw, keep=None):
    bsz, t, d = x3.shape
    m = bsz * t
    ch = min(CHUNK, t)
    x2 = x3.reshape(m, d)
    if keep is not None:
        z3, ab3, *tails = _inproj_seq(x3, lw["w_in"], lw["wab"], lw["n_main"], keep)
        z2 = z3.reshape(m, -1)
    else:
        z2, ab = _inproj(x2, lw["w_in"], lw["wab"], lw["n_main"], z_dtype, min(INPROJ_ROWS, m), 2 * d)
        z3, ab3, tails = z2.reshape(bsz, t, -1), ab.reshape(bsz, t, LANES), None
    o_a = attn_fn(z3)
    o_b, ssm = _gdn(z3, ab3, conv0, ssm0, lw["conv_w"], lw["pcol"], lw["norm_g"], ch)
    o_c = _memattn(z3, QC_BLOCK, mem_k, mem_v, min(2048, t))
    y = _ffn(x2, o_a.reshape(m, d), o_b.reshape(m, d), o_c.reshape(m, d),
             lw["wo"], lw["w1"], lw["w2"], lw["vec"], lw["b1"], lw["alpha"], min(512, m))
    return y.reshape(bsz, t, d), z3, ssm, tails


def kernel(x_prompt, x_sample, cache_a_k, cache_a_v, state_b_conv, state_b_ssm, cache_mem_k, cache_mem_v, mem_prompt, w_in, w_b_conv, b_a_log, b_dt_bias, b_norm_g, a_rel_bias, w_mem_kv, w_out, ln1_g, ln1_b, w_ff1, b_ff1, w_ff2, b_ff2, ln2_g, ln2_b):
    depth = w_in.shape[0]
    bp, tp, d = x_prompt.shape
    bs, tn, _ = x_sample.shape
    window = A_LEFT_CHUNKS * CHUNK
    keep_p = min(window, tp)
    lc = cache_a_k.shape[2]
    n_main = w_in.shape[2] - 2 * B_HEADS
    a_width = A_HEADS * A_HEAD_DIM
    b_width3 = 3 * B_HEADS * B_HEAD_DIM
    alpha = (2.0 * depth) ** 0.25
    xp, xs = x_prompt, x_sample
    outs = [[] for _ in range(10)]
    for l in range(depth):
        w_in_b = _scale_cast(
            jnp.swapaxes(w_in, 1, 2), l,
            ((0, a_width, (A_HEAD_DIM ** -0.5) * LOG2E),
             (QC_BLOCK * d, QC_BLOCK * d + C_HEADS * C_HEAD_DIM, (C_HEAD_DIM ** -0.5) * LOG2E)), d)
        vec = jnp.zeros((SUBLANES, d), F32)
        vec = vec.at[0].set(ln1_g[l]).at[1].set(ln1_b[l]).at[2].set(b_ff2[l]).at[3].set(ln2_g[l]).at[4].set(ln2_b[l])
        pcol = jnp.zeros((SUBLANES, LANES), F32)
        pcol = pcol.at[0, 0:B_HEADS].set(b_a_log[l]).at[1, 0:B_HEADS].set(b_dt_bias[l])
        lw = dict(
            w_in=w_in_b, n_main=n_main,
            wab=jnp.pad(w_in_b[n_main:, :], ((0, LANES - 2 * B_HEADS), (0, 0))),
            conv_w=w_b_conv[l],
            pcol=pcol,
            norm_g=b_norm_g[l].reshape(1, B_HEAD_DIM),
            wo=w_out[l].astype(BF16), w1=w_ff1[l].astype(BF16), w2=w_ff2[l].astype(BF16),
            vec=vec, b1=b_ff1[l].reshape(1, -1), alpha=alpha,
        )
        band = window + CHUNK
        bias_tab = _bias_table(a_rel_bias[l], CHUNK, band)
        bias_cols = bias_tab.shape[2]
        col0 = band - bias_cols

        bias_p = bias_tab.reshape(A_HEADS // A_GROUP, A_GROUP * CHUNK, bias_cols)
        mk, mv = _mem_kv(mem_prompt, w_mem_kv[l].astype(BF16))
        conv0 = jnp.zeros((bp, B_CONV - 1, b_width3), F32)
        ssm0 = jnp.zeros((bp, B_HEADS, B_HEAD_DIM, B_HEAD_DIM), F32)
        xp, _, ssm_p, (k_tail, v_tail, conv_tail) = _group_layer(
            xp, BF16, lambda z3: _attn_prompt(z3, bias_p, CHUNK, window), conv0, ssm0, mk, mv, lw, keep=keep_p)
        outs[0].append(k_tail.reshape(bp, keep_p, A_HEADS, A_HEAD_DIM))
        outs[1].append(v_tail.reshape(bp, keep_p, A_HEADS, A_HEAD_DIM))
        outs[2].append(conv_tail[:, SUBLANES - (B_CONV - 1):])
        outs[3].append(ssm_p)
        outs[4].append(mk)
        outs[5].append(mv)

        bias_s = bias_tab[:, 0:tn, 0:lc + tn - col0].reshape(A_HEADS // A_GROUP, A_GROUP * tn, lc + tn - col0)
        kt = cache_a_k[l].transpose(0, 2, 3, 1).reshape(bs, a_width, lc)
        vt = cache_a_v[l].transpose(0, 2, 3, 1).reshape(bs, a_width, lc)
        advanced = {}

        def attn_sample(z3):
            o_a, advanced["k"], advanced["v"] = _attn_sample(z3, kt, vt, bias_s)
            return o_a

        xs, zs3, ssm_s, _ = _group_layer(
            xs, F32, attn_sample, state_b_conv[l], state_b_ssm[l], cache_mem_k[l], cache_mem_v[l], lw)
        outs[6].append(advanced["k"].reshape(bs, A_HEADS, A_HEAD_DIM, lc).transpose(0, 3, 1, 2))
        outs[7].append(advanced["v"].reshape(bs, A_HEADS, A_HEAD_DIM, lc).transpose(0, 3, 1, 2))
        qkvb_s = zs3[:, :, 3 * a_width:3 * a_width + b_width3]
        outs[8].append(jnp.concatenate([state_b_conv[l], qkvb_s], axis=1)[:, tn:])
        outs[9].append(ssm_s)
    return (xp, xs) + tuple(jnp.stack(o) for o in outs)
```

```python
import functools

import jax
import jax.numpy as jnp
from jax import lax
from jax.experimental import pallas as pl
from jax.experimental.pallas import tpu as pltpu

F32 = jnp.float32
BF16 = jnp.bfloat16

CHUNK = 64
A_HEADS = 16
A_HEAD_DIM = 64
A_GROUP = 4
A_LEFT_CHUNKS = 8
A_REL_CLIP = 128
B_HEADS = 8
B_HEAD_DIM = 128
B_CONV = 4
C_HEADS = 4
C_HEAD_DIM = 256
N_BRANCH = 3
LN_EPS = 1e-5
RMS_EPS = 1e-6
L2_EPS = 1e-6
NEG_INF = -1e30

LANES = 128
SUBLANES = 8
VMEM_LIMIT_BYTES = 56 * 1024 * 1024


def _params(*sem):
    return pltpu.CompilerParams(dimension_semantics=sem, vmem_limit_bytes=VMEM_LIMIT_BYTES)


def _dot(a, b):
    return jnp.dot(a, b, preferred_element_type=F32)


def _dot_nt(a, b):
    return lax.dot_general(a, b, (((1,), (1,)), ((), ())), preferred_element_type=F32)


def _dot_tn(a, b):
    return lax.dot_general(a, b, (((0,), (0,)), ((), ())), preferred_element_type=F32)


def _sigmoid(x):
    return 0.5 + 0.5 * jnp.tanh(0.5 * x)


def _silu(x):
    hx = 0.5 * x
    return hx + hx * jnp.tanh(hx)


def _scale_cast_kernel(w_ref, o_ref, *, row_scales):
    rows = lax.broadcasted_iota(jnp.int32, w_ref.shape, 0) + pl.program_id(0) * w_ref.shape[0]
    w = w_ref[...]
    for start, stop, scale in row_scales:
        w = jnp.where((rows >= start) & (rows < stop), w * scale, w)
    o_ref[...] = w.astype(o_ref.dtype)


def _scale_cast(w_t, layer, row_scales, bn):
    _, n, k = w_t.shape
    return pl.pallas_call(
        functools.partial(_scale_cast_kernel, row_scales=row_scales),
        grid=(pl.cdiv(n, bn),),
        in_specs=[pl.BlockSpec((None, bn, k), lambda j: (layer, j, 0))],
        out_specs=pl.BlockSpec((bn, k), lambda j: (j, 0)),
        out_shape=jax.ShapeDtypeStruct((n, k), BF16),
        compiler_params=_params("arbitrary"),
        name="scale_cast",
    )(w_t)


K_BLOCK, V_BLOCK, CONV_BLOCK0, CONV_BLOCKS = 1, 2, 3, 3
QC_BLOCK, GATE_BLOCK0 = 6, 7


def _inproj_kernel(x_ref, w_ref, wab_ref, z_ref, ab_ref, xb_ref):
    @pl.when(pl.program_id(1) == 0)
    def _():
        xb = x_ref[...].astype(BF16)
        xb_ref[...] = xb
        ab_ref[...] = _dot_nt(xb, wab_ref[...])

    z_ref[...] = _dot_nt(xb_ref[...], w_ref[...]).astype(z_ref.dtype)


def _inproj(x2, w, wab, n_main, out_dtype, bm, bn):
    m, d = x2.shape
    return pl.pallas_call(
        _inproj_kernel,
        grid=(m // bm, n_main // bn),
        in_specs=[
            pl.BlockSpec((bm, d), lambda i, j: (i, 0)),
            pl.BlockSpec((bn, d), lambda i, j: (j, 0)),
            pl.BlockSpec((LANES, d), lambda i, j: (0, 0)),
        ],
        out_specs=[
            pl.BlockSpec((bm, bn), lambda i, j: (i, j)),
            pl.BlockSpec((bm, LANES), lambda i, j: (i, 0)),
        ],
        out_shape=[
            jax.ShapeDtypeStruct((m, n_main), out_dtype),
            jax.ShapeDtypeStruct((m, LANES), F32),
        ],
        scratch_shapes=[pltpu.VMEM((bm, d), BF16)],
        compiler_params=_params("arbitrary", "arbitrary"),
        name="inproj",
    )(x2, w, wab)


INPROJ_SUB = 2 * B_HEAD_DIM


def _inproj_seq_kernel(x_ref, w_ref, wab_ref, z_ref, ab_ref, kt_ref, vt_ref, ct_ref, xb_ref):
    j = pl.program_id(1)
    bm, bn = z_ref.shape
    keep = kt_ref.shape[0]

    @pl.when(j == 0)
    def _():
        xb = x_ref[...].astype(BF16)
        xb_ref[...] = xb
        ab_ref[...] = _dot_nt(xb, wab_ref[...])
        kt_ref[...] = jnp.zeros(kt_ref.shape, F32)
        vt_ref[...] = jnp.zeros(vt_ref.shape, F32)

    tails = []
    for n0 in range(0, bn, INPROJ_SUB):
        cols = slice(n0, n0 + INPROJ_SUB)
        acc = _dot_nt(xb_ref[...], w_ref[cols, :])
        z_ref[:, cols] = acc.astype(z_ref.dtype)
        tail = acc[bm - keep:, :]
        kt_ref[:, cols] = jnp.where(j == K_BLOCK, tail, kt_ref[:, cols])
        vt_ref[:, cols] = jnp.where(j == V_BLOCK, tail, vt_ref[:, cols])
        tails.append(acc[bm - SUBLANES:, :])

    @pl.when((j >= CONV_BLOCK0) & (j < CONV_BLOCK0 + CONV_BLOCKS))
    def _():
        ct_ref[...] = jnp.concatenate(tails, axis=1)


def _inproj_seq(x3, w, wab, n_main, keep):
    nseq, t, d = x3.shape
    bm, bn = t, d
    conv_blk = lambda i, j: jnp.clip(j - CONV_BLOCK0, 0, CONV_BLOCKS - 1)
    return pl.pallas_call(
        _inproj_seq_kernel,
        grid=(nseq, n_main // bn),
        in_specs=[
            pl.BlockSpec((None, bm, d), lambda i, j: (i, 0, 0)),
            pl.BlockSpec((bn, d), lambda i, j: (j, 0)),
            pl.BlockSpec((LANES, d), lambda i, j: (0, 0)),
        ],
        out_specs=[
            pl.BlockSpec((None, bm, bn), lambda i, j: (i, 0, j)),
            pl.BlockSpec((None, bm, LANES), lambda i, j: (i, 0, 0)),
            pl.BlockSpec((None, keep, bn), lambda i, j: (i, 0, 0)),
            pl.BlockSpec((None, keep, bn), lambda i, j: (i, 0, 0)),
            pl.BlockSpec((None, SUBLANES, bn), lambda i, j: (i, 0, conv_blk(i, j))),
        ],
        out_shape=[
            jax.ShapeDtypeStruct((nseq, t, n_main), BF16),
            jax.ShapeDtypeStruct((nseq, t, LANES), F32),
            jax.ShapeDtypeStruct((nseq, keep, bn), F32),
            jax.ShapeDtypeStruct((nseq, keep, bn), F32),
            jax.ShapeDtypeStruct((nseq, SUBLANES, CONV_BLOCKS * bn), F32),
        ],
        scratch_shapes=[pltpu.VMEM((bm, d), BF16)],
        compiler_params=_params("arbitrary", "arbitrary"),
        name="inproj_seq",
    )(x3, w, wab)


def _mem_kv_kernel(x_ref, w_ref, k_ref, v_ref):
    acc = _dot(x_ref[...].astype(BF16), w_ref[...])
    mt, heads, dh = k_ref.shape
    k_ref[...] = acc[:, 0:heads * dh].reshape(mt, heads, dh)
    v_ref[...] = acc[:, heads * dh:2 * heads * dh].reshape(mt, heads, dh)


def _mem_kv(mem, w):
    bsz, mt, d = mem.shape
    out = jax.ShapeDtypeStruct((bsz, mt, C_HEADS, C_HEAD_DIM), F32)
    return pl.pallas_call(
        _mem_kv_kernel,
        grid=(bsz,),
        in_specs=[
            pl.BlockSpec((None, mt, d), lambda b: (b, 0, 0)),
            pl.BlockSpec(w.shape, lambda b: (0, 0)),
        ],
        out_specs=[pl.BlockSpec((None, mt, C_HEADS, C_HEAD_DIM), lambda b: (b, 0, 0, 0))] * 2,
        out_shape=[out, out],
        compiler_params=_params("arbitrary"),
        name="mem_kv",
    )(mem, w)


LOG2E = 1.4426950408889634


def _bias_col0(window):
    return (window - A_REL_CLIP + 1) // LANES * LANES


def _bias_table_kernel(rev_ref, o_ref):
    heads, tq, cols = o_ref.shape
    width = rev_ref.shape[1]
    lane = lax.broadcasted_iota(jnp.int32, (heads, width), 1)
    w = jnp.where(lane < cols, (rev_ref[...] - rev_ref[:, 0:1]) * LOG2E, 0.0)
    for h in range(heads):
        rows = jnp.broadcast_to(w[h:h + 1, :], (tq, width))
        o_ref[h] = pltpu.roll(rows, 0, axis=1, stride=1, stride_axis=0)[:, 0:cols]


def _bias_table(rel_bias, tq, band):
    heads = rel_bias.shape[0]
    window = band - tq
    col0 = _bias_col0(window)
    cols = band - col0
    width = 2 * A_REL_CLIP
    assert window - col0 == A_REL_CLIP and window - (band - 1) >= -A_REL_CLIP
    assert cols + tq <= width and width % LANES == 0
    rev = rel_bias[:, ::-1][:, 0:width]
    return pl.pallas_call(
        _bias_table_kernel,
        out_shape=jax.ShapeDtypeStruct((heads, tq, cols), F32),
        compiler_params=pltpu.CompilerParams(vmem_limit_bytes=VMEM_LIMIT_BYTES),
        name="bias_table",
    )(rev)


A_GROUP_WIDTH = A_GROUP * A_HEAD_DIM


def _lane_head(tq):
    return jnp.right_shift(lax.broadcasted_iota(jnp.int32, (tq, A_GROUP_WIDTH), 1), A_HEAD_DIM.bit_length() - 1)


def _attn_scores(qg, kg, bias_g, valid_from):
    lane_head = _lane_head(qg.shape[0])
    qg = qg.astype(BF16)
    zero = jnp.zeros_like(qg)
    qm = jnp.concatenate([jnp.where(lane_head == h, qg, zero) for h in range(A_GROUP)], axis=0)
    s = _dot_nt(qm, kg)
    col0 = s.shape[1] - bias_g.shape[1]
    s = jnp.concatenate([s[:, 0:col0], s[:, col0:] + bias_g], axis=1)
    if valid_from is not None:
        col = lax.broadcasted_iota(jnp.int32, s.shape, 1)
        s = jnp.where(col >= valid_from, s, NEG_INF)
    m = jnp.max(s, axis=-1, keepdims=True)
    p = jnp.exp2(s - m)
    return p.astype(BF16), 1.0 / jnp.sum(p, axis=-1, keepdims=True)


def _attn_values(p, rl, vg):
    tq = p.shape[0] // A_GROUP
    o = _dot(p, vg)
    if rl.shape[1] == 1:
        o = o * rl
    else:
        o = o * jnp.concatenate([rl] * (A_GROUP_WIDTH // LANES), axis=1)
    lane_head = _lane_head(tq)
    out = jnp.zeros((tq, A_GROUP_WIDTH), F32)
    for h in range(A_GROUP):
        out = jnp.where(lane_head == h, o[h * tq:(h + 1) * tq], out)
    return out


A_CHUNKS_PER_STEP = 8


def _attn_prompt_kernel(q_ref, k_ref, v_ref, gate_ref, bias_ref, o_ref, kpad_ref, vpad_ref, *, window, tq):
    s = pl.program_id(1)
    band = window + tq
    gw = A_GROUP_WIDTH
    cps = q_ref.shape[0] // tq

    @pl.when(s == 0)
    def _():
        zeros = jnp.zeros((window, kpad_ref.shape[1]), BF16)
        kpad_ref[0:window, :] = zeros
        vpad_ref[0:window, :] = zeros
        kpad_ref[window:, :] = k_ref[...]
        vpad_ref[window:, :] = v_ref[...]

    for j in range(cps):
        c = s * cps + j
        start = pl.multiple_of(c * tq, tq)
        rows = slice(j * tq, (j + 1) * tq)
        for g in range(A_HEADS // A_GROUP):
            cols = slice(g * gw, (g + 1) * gw)
            p, rl = _attn_scores(q_ref[rows, cols], kpad_ref[pl.ds(start, band), cols], bias_ref[g],
                                 window - c * tq)
            out = _attn_values(p, rl, vpad_ref[pl.ds(start, band), cols])
            o_ref[rows, cols] = (out * _sigmoid(gate_ref[rows, cols].astype(F32))).astype(o_ref.dtype)


def _attn_prompt(z3, bias_g, tq, window):
    bsz, t, _ = z3.shape
    width = A_HEADS * A_HEAD_DIM
    rows = tq * min(A_CHUNKS_PER_STEP, t // tq)
    return pl.pallas_call(
        functools.partial(_attn_prompt_kernel, window=window, tq=tq),
        grid=(bsz, t // rows),
        in_specs=[
            pl.BlockSpec((None, rows, width), lambda b, s: (b, s, 0)),
            pl.BlockSpec((None, t, width), lambda b, s: (b, 0, K_BLOCK)),
            pl.BlockSpec((None, t, width), lambda b, s: (b, 0, V_BLOCK)),
            pl.BlockSpec((None, rows, width), lambda b, s: (b, s, GATE_BLOCK0)),
            pl.BlockSpec(bias_g.shape, lambda b, s: (0, 0, 0)),
        ],
        out_specs=pl.BlockSpec((None, rows, width), lambda b, s: (b, s, 0)),
        out_shape=jax.ShapeDtypeStruct((bsz, t, width), BF16),
        scratch_shapes=[pltpu.VMEM((t + window, width), BF16), pltpu.VMEM((t + window, width), BF16)],
        compiler_params=_params("arbitrary", "arbitrary"),
        name="attn_prompt",
    )(z3, z3, z3, z3, bias_g)


def _attn_sample_kernel(q_ref, kn_ref, vn_ref, gate_ref, kt_ref, vt_ref, bias_ref, o_ref, ko_ref, vo_ref):
    tq = q_ref.shape[0]
    lc = kt_ref.shape[1]
    gw = A_GROUP_WIDTH
    lane_head = _lane_head(tq)
    ko_ref[...] = jnp.concatenate([kt_ref[:, tq:], kn_ref[...].T], axis=1)
    vo_ref[...] = jnp.concatenate([vt_ref[:, tq:], vn_ref[...].T], axis=1)
    for g in range(A_HEADS // A_GROUP):
        cols = slice(g * gw, (g + 1) * gw)
        qg = q_ref[:, cols].astype(BF16)
        zero = jnp.zeros_like(qg)
        qm = jnp.concatenate([jnp.where(lane_head == h, qg, zero) for h in range(A_GROUP)], axis=0)
        s_c = _dot(qm, kt_ref[cols, :].astype(BF16))
        s_n = _dot_nt(qm, kn_ref[:, cols].astype(BF16))
        bias = bias_ref[g]
        col0 = lc + tq - bias.shape[1]
        s_c = jnp.concatenate([s_c[:, 0:col0], s_c[:, col0:] + bias[:, 0:lc - col0]], axis=1)
        s_n = s_n + bias[:, lc - col0:]
        m = jnp.maximum(jnp.max(s_c, axis=-1, keepdims=True), jnp.max(s_n, axis=-1, keepdims=True))
        p_c = jnp.exp2(s_c - m)
        p_n = jnp.exp2(s_n - m)
        rl = 1.0 / (jnp.sum(p_c, axis=-1, keepdims=True) + jnp.sum(p_n, axis=-1, keepdims=True))
        o = (_dot_nt(p_c.astype(BF16), vt_ref[cols, :].astype(BF16))
             + _dot(p_n.astype(BF16), vn_ref[:, cols].astype(BF16))) * rl
        out = jnp.zeros((tq, gw), F32)
        for h in range(A_GROUP):
            out = jnp.where(lane_head == h, o[h * tq:(h + 1) * tq], out)
        o_ref[:, cols] = (out * _sigmoid(gate_ref[:, cols].astype(F32))).astype(o_ref.dtype)


def _attn_sample(z3, kt, vt, bias_g):
    bsz, tq, _ = z3.shape
    width, lc = kt.shape[1:]
    cache = jax.ShapeDtypeStruct(kt.shape, kt.dtype)
    cache_spec = pl.BlockSpec((None, width, lc), lambda b: (b, 0, 0))
    return pl.pallas_call(
        _attn_sample_kernel,
        grid=(bsz,),
        in_specs=[
            pl.BlockSpec((None, tq, width), lambda b: (b, 0, 0)),
            pl.BlockSpec((None, tq, width), lambda b: (b, 0, K_BLOCK)),
            pl.BlockSpec((None, tq, width), lambda b: (b, 0, V_BLOCK)),
            pl.BlockSpec((None, tq, width), lambda b: (b, 0, GATE_BLOCK0)),
            cache_spec, cache_spec,
            pl.BlockSpec(bias_g.shape, lambda b: (0, 0, 0)),
        ],
        out_specs=[pl.BlockSpec((None, tq, width), lambda b: (b, 0, 0)), cache_spec, cache_spec],
        out_shape=[jax.ShapeDtypeStruct((bsz, tq, width), BF16), cache, cache],
        compiler_params=_params("arbitrary"),
        name="attn_sample",
    )(z3, z3, z3, z3, kt, vt, bias_g)


def _gdn_kernel(x_ref, ab_ref, gate_ref, conv0_ref, s0_ref, cw_ref, pcol_ref, ng_ref,
                o_ref, s_ref, xbuf_ref):
    c = pl.program_id(1)
    nb, ch, _ = x_ref.shape
    hd = B_HEAD_DIM
    width = B_HEADS * hd
    keep = B_CONV - 1
    shift_on_mxu = x_ref.dtype == BF16
    top = xbuf_ref.shape[1] - (0 if shift_on_mxu else ch)
    chains = [(bi, h) for bi in range(nb) for h in range(B_HEADS)]
    every = range(len(chains))

    @pl.when(c == 0)
    def _():
        hist = jnp.concatenate([jnp.zeros((nb, top - keep, xbuf_ref.shape[2]), F32),
                                conv0_ref[...].astype(F32)], axis=1)
        xbuf_ref[:, 0:top, :] = hist.astype(xbuf_ref.dtype)
        s_ref[...] = s0_ref[...].astype(F32)

    if shift_on_mxu:
        t_r = lax.broadcasted_iota(jnp.int32, (ch, top + ch), 0)
        u_c = lax.broadcasted_iota(jnp.int32, (ch, top + ch), 1)
        shift01 = jnp.concatenate([jnp.where(u_c == top + t_r - (keep - i), 1.0, 0.0) for i in range(keep)],
                                  axis=0).astype(BF16)
        shifted = [_dot(shift01, jnp.concatenate([xbuf_ref[bi], x_ref[bi]], axis=0)) for bi in range(nb)]
    else:
        xbuf_ref[:, top:top + ch, :] = x_ref[...]

    t_i = lax.broadcasted_iota(jnp.int32, (ch, ch), 0)
    s_i = lax.broadcasted_iota(jnp.int32, (ch, ch), 1)
    tri_incl = (t_i >= s_i)
    tri_strict = (t_i > s_i)
    eye = jnp.where(t_i == s_i, 1.0, 0.0)
    lane = lax.broadcasted_iota(jnp.int32, (ch, LANES), 1)
    lane_row = lax.broadcasted_iota(jnp.int32, (1, ch), 1)
    frame = lax.broadcasted_iota(jnp.int32, (ch, LANES), 0)

    gcum_col, beta_col, gcum_row = [], [], []
    for bi in range(nb):
        ab = ab_ref[bi]
        g = -jnp.exp(pcol_ref[0:1, :]) * jax.nn.softplus(ab + pcol_ref[1:2, :])
        step = 1
        while step < ch:
            g = g + jnp.where(frame >= step, pltpu.roll(g, step, axis=0), 0.0)
            step *= 2
        beta_col.append(_sigmoid(ab))
        gcum_col.append(g)
        gcum_row.append(g.T)

    def pick(x, idx):
        return jnp.sum(jnp.where(lane == idx, x, 0.0), axis=-1, keepdims=True)

    def conv_silu(bi, col0):
        cols = slice(col0, col0 + hd)
        if shift_on_mxu:
            acc = x_ref[bi, :, cols].astype(F32) * cw_ref[keep:keep + 1, cols]
            for i in range(keep):
                acc = acc + shifted[bi][i * ch:(i + 1) * ch, cols] * cw_ref[i:i + 1, cols]
        else:
            acc = xbuf_ref[bi, top:top + ch, cols] * cw_ref[keep:keep + 1, cols]
            for i in range(keep):
                acc = acc + xbuf_ref[bi, top - keep + i:top - keep + i + ch, cols] * cw_ref[i:i + 1, cols]
        return _silu(acc)

    def l2norm(x):
        return x * lax.rsqrt(jnp.sum(x * x, axis=-1, keepdims=True) + L2_EPS)

    q = [l2norm(conv_silu(bi, h * hd)) * (hd ** -0.5) for bi, h in chains]
    k = [l2norm(conv_silu(bi, width + h * hd)) for bi, h in chains]
    v = [conv_silu(bi, 2 * width + h * hd) for bi, h in chains]
    gc = [pick(gcum_col[bi], h) for bi, h in chains]
    bc = [pick(beta_col[bi], B_HEADS + h) for bi, h in chains]
    gr = [gcum_row[bi][h:h + 1, :] for bi, h in chains]
    glast = [jnp.sum(jnp.where(lane_row == ch - 1, gr[i], 0.0), axis=-1, keepdims=True) for i in every]
    dec_incl = [jnp.exp(jnp.where(tri_incl, gc[i] - gr[i], NEG_INF)) for i in every]
    eg = [jnp.exp(gc[i]) for i in every]
    nbc = [-bc[i] for i in every]

    qk_kk = [_dot_nt(jnp.concatenate([q[i], k[i]], axis=0).astype(BF16), k[i].astype(BF16)) for i in every]
    a_qk = [qk_kk[i][0:ch] * dec_incl[i] for i in every]

    tp = [jnp.concatenate([(nbc[i] * qk_kk[i][ch:2 * ch]) * jnp.where(tri_strict, dec_incl[i], 0.0), eye], axis=1)
          for i in every]
    t_half = lax.broadcasted_iota(jnp.int32, (ch, 2 * ch), 1) >= ch
    for _ in range(max(1, (ch - 1).bit_length())):
        tp_b = [tp[i].astype(BF16) for i in every]
        tp = [_dot(tp_b[i][:, 0:ch], tp_b[i]) + jnp.where(t_half, tp[i], 0.0) for i in every]
    rhs = [jnp.concatenate([bc[i] * v[i], (bc[i] * eg[i]) * k[i]], axis=1).astype(BF16) for i in every]
    x_sol = [_dot(tp[i][:, ch:2 * ch].astype(BF16), rhs[i]) for i in every]

    state = [s_ref[bi, h] for bi, h in chains]
    wq = [_dot(jnp.concatenate([x_sol[i][:, hd:2 * hd], q[i] * eg[i]], axis=0).astype(BF16),
               state[i].astype(BF16)) for i in every]
    wv_b = [(x_sol[i][:, 0:hd] - wq[i][0:ch]).astype(BF16) for i in every]
    o = [wq[i][ch:2 * ch] + _dot(a_qk[i].astype(BF16), wv_b[i]) for i in every]
    s_new = [jnp.exp(glast[i]) * state[i]
             + _dot_tn((k[i] * jnp.exp(glast[i] - gc[i])).astype(BF16), wv_b[i]) for i in every]
    for i, (bi, h) in enumerate(chains):
        s_ref[bi, h] = s_new[i]
        on = o[i] * lax.rsqrt(jnp.mean(o[i] * o[i], axis=-1, keepdims=True) + RMS_EPS) * ng_ref[...]
        gate = _sigmoid(gate_ref[bi, :, h * hd:(h + 1) * hd].astype(F32))
        o_ref[bi, :, h * hd:(h + 1) * hd] = (on * gate).astype(o_ref.dtype)

    if shift_on_mxu:
        xbuf_ref[...] = x_ref[:, ch - top:ch, :]
    else:
        xbuf_ref[:, top - keep:top, :] = xbuf_ref[:, top + ch - keep:top + ch, :]


GDN_BATCHES = 2


def _gdn(z3, ab3, conv0, s0, conv_w, pcol, norm_g, ch):
    bsz, t, _ = z3.shape
    nb = GDN_BATCHES
    width3 = 3 * B_HEADS * B_HEAD_DIM
    width = B_HEADS * B_HEAD_DIM
    if z3.dtype == BF16:
        xbuf = pltpu.VMEM((nb, 2 * SUBLANES, width3), BF16)
    else:
        xbuf = pltpu.VMEM((nb, SUBLANES + ch, width3), F32)
    return pl.pallas_call(
        _gdn_kernel,
        grid=(bsz // nb, t // ch),
        in_specs=[
            pl.BlockSpec((nb, ch, width3), lambda b, c: (b, c, 1)),
            pl.BlockSpec((nb, ch, LANES), lambda b, c: (b, c, 0)),
            pl.BlockSpec((nb, ch, width), lambda b, c: (b, c, GATE_BLOCK0 + 1)),
            pl.BlockSpec((nb, B_CONV - 1, width3), lambda b, c: (b, 0, 0)),
            pl.BlockSpec((nb, B_HEADS, B_HEAD_DIM, B_HEAD_DIM), lambda b, c: (b, 0, 0, 0)),
            pl.BlockSpec(conv_w.shape, lambda b, c: (0, 0)),
            pl.BlockSpec(pcol.shape, lambda b, c: (0, 0)),
            pl.BlockSpec(norm_g.shape, lambda b, c: (0, 0)),
        ],
        out_specs=[
            pl.BlockSpec((nb, ch, width), lambda b, c: (b, c, 0)),
            pl.BlockSpec((nb, B_HEADS, B_HEAD_DIM, B_HEAD_DIM), lambda b, c: (b, 0, 0, 0)),
        ],
        out_shape=[
            jax.ShapeDtypeStruct((bsz, t, width), BF16),
            jax.ShapeDtypeStruct((bsz, B_HEADS, B_HEAD_DIM, B_HEAD_DIM), F32),
        ],
        scratch_shapes=[xbuf],
        compiler_params=_params("arbitrary", "arbitrary"),
        name="gdn",
    )(z3, ab3, z3, conv0, s0, conv_w, pcol, norm_g)


MEMATTN_ROWS = 512


def _memattn_kernel(q_ref, gate_ref, mk_ref, mv_ref, o_ref, kb_ref, vb_ref):
    dh = C_HEAD_DIM

    @pl.when(pl.program_id(1) == 0)
    def _():
        kb_ref[...] = mk_ref[...].reshape(kb_ref.shape).astype(BF16)
        vb_ref[...] = mv_ref[...].reshape(vb_ref.shape).astype(BF16)

    tq = q_ref.shape[0]
    rb = min(MEMATTN_ROWS, tq)
    for r0 in range(0, tq, rb):
        for h in range(C_HEADS):
            cols = slice(h * dh, (h + 1) * dh)
            q = q_ref[r0:r0 + rb, cols].astype(BF16)
            s = _dot_nt(q, kb_ref[:, cols])
            m = jnp.max(s, axis=-1, keepdims=True)
            p = jnp.exp2(s - m)
            l = jnp.sum(p, axis=-1, keepdims=True)
            o = _dot(p.astype(BF16), vb_ref[:, cols]) * (1.0 / l)
            gate = _sigmoid(gate_ref[r0:r0 + rb, cols].astype(F32))
            o_ref[r0:r0 + rb, cols] = (o * gate).astype(o_ref.dtype)


def _memattn(z3, q_blk, mk, mv, tq):
    bsz, t, _ = z3.shape
    mem = mk.shape[1]
    width = C_HEADS * C_HEAD_DIM
    kv_spec = pl.BlockSpec((None, mem, C_HEADS, C_HEAD_DIM), lambda b, i: (b, 0, 0, 0))
    return pl.pallas_call(
        _memattn_kernel,
        grid=(bsz, t // tq),
        in_specs=[
            pl.BlockSpec((None, tq, width), lambda b, i: (b, i, q_blk)),
            pl.BlockSpec((None, tq, width), lambda b, i: (b, i, GATE_BLOCK0 + 2)),
            kv_spec,
            kv_spec,
        ],
        out_specs=pl.BlockSpec((None, tq, width), lambda b, i: (b, i, 0)),
        out_shape=jax.ShapeDtypeStruct((bsz, t, width), BF16),
        scratch_shapes=[pltpu.VMEM((mem, width), BF16), pltpu.VMEM((mem, width), BF16)],
        compiler_params=_params("arbitrary", "arbitrary"),
        name="memattn",
    )(z3, z3, mk, mv)


def _layer_norm(x, g, b):
    mu = jnp.mean(x, axis=-1, keepdims=True)
    xc = x - mu
    var = jnp.mean(xc * xc, axis=-1, keepdims=True)
    return xc * lax.rsqrt(var + LN_EPS) * g + b


FFN_PARTS = 2
FFN_HIDDEN_BLOCK = 1024


def _ffn_kernel(x_ref, oa_ref, ob_ref, oc_ref, wo_ref, w1_ref, w2_ref, vec_ref, b1_ref, y_ref, *, alpha, ff_blk):
    bm = x_ref.shape[0]
    rows = [pl.ds(r * (bm // FFN_PARTS), bm // FFN_PARTS) for r in range(FFN_PARTS)]
    parts = range(FFN_PARTS)

    merged = [(oa_ref[r, :].astype(F32) + ob_ref[r, :].astype(F32) + oc_ref[r, :].astype(F32)).astype(BF16)
              for r in rows]
    proj = [_dot(merged[i], wo_ref[...]) for i in parts]
    h = [_layer_norm(alpha * x_ref[rows[i], :] + proj[i], vec_ref[0:1, :], vec_ref[1:2, :]) for i in parts]
    hb = [h[i].astype(BF16) for i in parts]
    acc = [None] * FFN_PARTS
    for k0 in range(0, w1_ref.shape[1], ff_blk):
        f = [_dot(hb[i], w1_ref[:, k0:k0 + ff_blk]) + b1_ref[:, k0:k0 + ff_blk] for i in parts]
        f = [jnp.square(jnp.maximum(f[i], 0.0)).astype(BF16) for i in parts]
        d = [_dot(f[i], w2_ref[k0:k0 + ff_blk, :]) for i in parts]
        acc = [d[i] if acc[i] is None else acc[i] + d[i] for i in parts]
    for i in parts:
        y_ref[rows[i], :] = _layer_norm(alpha * h[i] + acc[i] + vec_ref[2:3, :], vec_ref[3:4, :], vec_ref[4:5, :])


def _ffn(x2, oa, ob, oc, wo, w1, w2, vec, b1, alpha, bm):
    m, d = x2.shape
    dff = w1.shape[1]
    const = dict(pipeline_mode=pl.Buffered(1))
    row = lambda i: (i, 0)
    return pl.pallas_call(
        functools.partial(_ffn_kernel, alpha=alpha, ff_blk=FFN_HIDDEN_BLOCK),
        grid=(m // bm,),
        in_specs=[
            pl.BlockSpec((bm, d), row),
            pl.BlockSpec((bm, d), row),
            pl.BlockSpec((bm, d), row),
            pl.BlockSpec((bm, d), row),
            pl.BlockSpec((d, d), lambda i: (0, 0), **const),
            pl.BlockSpec((d, dff), lambda i: (0, 0), **const),
            pl.BlockSpec((dff, d), lambda i: (0, 0), **const),
            pl.BlockSpec(vec.shape, lambda i: (0, 0), **const),
            pl.BlockSpec(b1.shape, lambda i: (0, 0), **const),
        ],
        out_specs=pl.BlockSpec((bm, d), row),
        out_shape=jax.ShapeDtypeStruct((m, d), F32),
        compiler_params=_params("arbitrary"),
        name="merge_ffn",
    )(x2, oa, ob, oc, wo, w1, w2, vec, b1)


INPROJ_ROWS = 2048


def _group_layer(x3, z_dtype, attn_fn, conv0, ssm0, mem_k, mem_v, lw, keep=None):
    bsz, t, d = x3.shape
    m = bsz * t
    ch = min(CHUNK, t)
    x2 = x3.reshape(m, d)
    if keep is not None:
        z3, ab3, *tails = _inproj_seq(x3, lw["w_in"], lw["wab"], lw["n_main"], keep)
        z2 = z3.reshape(m, -1)
    else:
        z2, ab = _inproj(x2, lw["w_in"], lw["wab"], lw["n_main"], z_dtype, min(INPROJ_ROWS, m), 2 * d)
        z3, ab3, tails = z2.reshape(bsz, t, -1), ab.reshape(bsz, t, LANES), None
    o_a = attn_fn(z3)
    o_b, ssm = _gdn(z3, ab3, conv0, ssm0, lw["conv_w"], lw["pcol"], lw["norm_g"], ch)
    o_c = _memattn(z3, QC_BLOCK, mem_k, mem_v, min(2048, t))
    y = _ffn(x2, o_a.reshape(m, d), o_b.reshape(m, d), o_c.reshape(m, d),
             lw["wo"], lw["w1"], lw["w2"], lw["vec"], lw["b1"], lw["alpha"], min(512, m))
    return y.reshape(bsz, t, d), z3, ssm, tails


def kernel(x_prompt, x_sample, cache_a_k, cache_a_v, state_b_conv, state_b_ssm, cache_mem_k, cache_mem_v, mem_prompt, w_in, w_b_conv, b_a_log, b_dt_bias, b_norm_g, a_rel_bias, w_mem_kv, w_out, ln1_g, ln1_b, w_ff1, b_ff1, w_ff2, b_ff2, ln2_g, ln2_b):
    depth = w_in.shape[0]
    bp, tp, d = x_prompt.shape
    bs, tn, _ = x_sample.shape
    window = A_LEFT_CHUNKS * CHUNK
    keep_p = min(window, tp)
    lc = cache_a_k.shape[2]
    n_main = w_in.shape[2] - 2 * B_HEADS
    a_width = A_HEADS * A_HEAD_DIM
    b_width3 = 3 * B_HEADS * B_HEAD_DIM
    alpha = (2.0 * depth) ** 0.25
    xp, xs = x_prompt, x_sample
    outs = [[] for _ in range(10)]
    for l in range(depth):
        w_in_b = _scale_cast(
            jnp.swapaxes(w_in, 1, 2), l,
            ((0, a_width, (A_HEAD_DIM ** -0.5) * LOG2E),
             (QC_BLOCK * d, QC_BLOCK * d + C_HEADS * C_HEAD_DIM, (C_HEAD_DIM ** -0.5) * LOG2E)), d)
        vec = jnp.zeros((SUBLANES, d), F32)
        vec = vec.at[0].set(ln1_g[l]).at[1].set(ln1_b[l]).at[2].set(b_ff2[l]).at[3].set(ln2_g[l]).at[4].set(ln2_b[l])
        pcol = jnp.zeros((SUBLANES, LANES), F32)
        pcol = pcol.at[0, 0:B_HEADS].set(b_a_log[l]).at[1, 0:B_HEADS].set(b_dt_bias[l])
        lw = dict(
            w_in=w_in_b, n_main=n_main,
            wab=jnp.pad(w_in_b[n_main:, :], ((0, LANES - 2 * B_HEADS), (0, 0))),
            conv_w=w_b_conv[l],
            pcol=pcol,
            norm_g=b_norm_g[l].reshape(1, B_HEAD_DIM),
            wo=w_out[l].astype(BF16), w1=w_ff1[l].astype(BF16), w2=w_ff2[l].astype(BF16),
            vec=vec, b1=b_ff1[l].reshape(1, -1), alpha=alpha,
        )
        band = window + CHUNK
        bias_tab = _bias_table(a_rel_bias[l], CHUNK, band)
        bias_cols = bias_tab.shape[2]
        col0 = band - bias_cols

        bias_p = bias_tab.reshape(A_HEADS // A_GROUP, A_GROUP * CHUNK, bias_cols)
        mk, mv = _mem_kv(mem_prompt, w_mem_kv[l].astype(BF16))
        conv0 = jnp.zeros((bp, B_CONV - 1, b_width3), F32)
        ssm0 = jnp.zeros((bp, B_HEADS, B_HEAD_DIM, B_HEAD_DIM), F32)
        xp, _, ssm_p, (k_tail, v_tail, conv_tail) = _group_layer(
            xp, BF16, lambda z3: _attn_prompt(z3, bias_p, CHUNK, window), conv0, ssm0, mk, mv, lw, keep=keep_p)
        outs[0].append(k_tail.reshape(bp, keep_p, A_HEADS, A_HEAD_DIM))
        outs[1].append(v_tail.reshape(bp, keep_p, A_HEADS, A_HEAD_DIM))
        outs[2].append(conv_tail[:, SUBLANES - (B_CONV - 1):])
        outs[3].append(ssm_p)
        outs[4].append(mk)
        outs[5].append(mv)

        bias_s = bias_tab[:, 0:tn, 0:lc + tn - col0].reshape(A_HEADS // A_GROUP, A_GROUP * tn, lc + tn - col0)
        kt = cache_a_k[l].transpose(0, 2, 3, 1).reshape(bs, a_width, lc)
        vt = cache_a_v[l].transpose(0, 2, 3, 1).reshape(bs, a_width, lc)
        advanced = {}

        def attn_sample(z3):
            o_a, advanced["k"], advanced["v"] = _attn_sample(z3, kt, vt, bias_s)
            return o_a

        xs, zs3, ssm_s, _ = _group_layer(
            xs, F32, attn_sample, state_b_conv[l], state_b_ssm[l], cache_mem_k[l], cache_mem_v[l], lw)
        outs[6].append(advanced["k"].reshape(bs, A_HEADS, A_HEAD_DIM, lc).transpose(0, 3, 1, 2))
        outs[7].append(advanced["v"].reshape(bs, A_HEADS, A_HEAD_DIM, lc).transpose(0, 3, 1, 2))
        qkvb_s = zs3[:, :, 3 * a_width:3 * a_width + b_width3]
        outs[8].append(jnp.concatenate([state_b_conv[l], qkvb_s], axis=1)[:, tn:])
        outs[9].append(ssm_s)
    return (xp, xs) + tuple(jnp.stack(o) for o in outs)
```

```python
import functools

import jax
import jax.numpy as jnp
from jax import lax
from jax.experimental import pallas as pl
from jax.experimental.pallas import tpu as pltpu

F32 = jnp.float32
BF16 = jnp.bfloat16

CHUNK = 64
A_HEADS = 16
A_HEAD_DIM = 64
A_GROUP = 4
A_LEFT_CHUNKS = 8
A_REL_CLIP = 128
B_HEADS = 8
B_HEAD_DIM = 128
B_CONV = 4
C_HEADS = 4
C_HEAD_DIM = 256
N_BRANCH = 3
LN_EPS = 1e-5
RMS_EPS = 1e-6
L2_EPS = 1e-6
NEG_INF = -1e30

LANES = 128
SUBLANES = 8
VMEM_LIMIT_BYTES = 56 * 1024 * 1024


def _params(*sem):
    return pltpu.CompilerParams(dimension_semantics=sem, vmem_limit_bytes=VMEM_LIMIT_BYTES)


def _dot(a, b):
    return jnp.dot(a, b, preferred_element_type=F32)


def _dot_nt(a, b):
    return lax.dot_general(a, b, (((1,), (1,)), ((), ())), preferred_element_type=F32)


def _dot_tn(a, b):
    return lax.dot_general(a, b, (((0,), (0,)), ((), ())), preferred_element_type=F32)


def _sigmoid(x):
    return 0.5 + 0.5 * jnp.tanh(0.5 * x)


def _silu(x):
    hx = 0.5 * x
    return hx + hx * jnp.tanh(hx)


def _scale_cast_kernel(w_ref, o_ref, *, row_scales):
    rows = lax.broadcasted_iota(jnp.int32, w_ref.shape, 0) + pl.program_id(0) * w_ref.shape[0]
    w = w_ref[...]
    for start, stop, scale in row_scales:
        w = jnp.where((rows >= start) & (rows < stop), w * scale, w)
    o_ref[...] = w.astype(o_ref.dtype)


def _scale_cast(w_t, layer, row_scales, bn):
    _, n, k = w_t.shape
    return pl.pallas_call(
        functools.partial(_scale_cast_kernel, row_scales=row_scales),
        grid=(pl.cdiv(n, bn),),
        in_specs=[pl.BlockSpec((None, bn, k), lambda j: (layer, j, 0))],
        out_specs=pl.BlockSpec((bn, k), lambda j: (j, 0)),
        out_shape=jax.ShapeDtypeStruct((n, k), BF16),
        compiler_params=_params("arbitrary"),
        name="scale_cast",
    )(w_t)


K_BLOCK, V_BLOCK, CONV_BLOCK0, CONV_BLOCKS = 1, 2, 3, 3
QC_BLOCK, GATE_BLOCK0 = 6, 7


def _inproj_kernel(x_ref, w_ref, wab_ref, z_ref, ab_ref, xb_ref):
    @pl.when(pl.program_id(1) == 0)
    def _():
        xb = x_ref[...].astype(BF16)
        xb_ref[...] = xb
        ab_ref[...] = _dot_nt(xb, wab_ref[...])

    z_ref[...] = _dot_nt(xb_ref[...], w_ref[...]).astype(z_ref.dtype)


def _inproj(x2, w, wab, n_main, out_dtype, bm, bn):
    m, d = x2.shape
    return pl.pallas_call(
        _inproj_kernel,
        grid=(m // bm, n_main // bn),
        in_specs=[
            pl.BlockSpec((bm, d), lambda i, j: (i, 0)),
            pl.BlockSpec((bn, d), lambda i, j: (j, 0)),
            pl.BlockSpec((LANES, d), lambda i, j: (0, 0)),
        ],
        out_specs=[
            pl.BlockSpec((bm, bn), lambda i, j: (i, j)),
            pl.BlockSpec((bm, LANES), lambda i, j: (i, 0)),
        ],
        out_shape=[
            jax.ShapeDtypeStruct((m, n_main), out_dtype),
            jax.ShapeDtypeStruct((m, LANES), F32),
        ],
        scratch_shapes=[pltpu.VMEM((bm, d), BF16)],
        compiler_params=_params("arbitrary", "arbitrary"),
        name="inproj",
    )(x2, w, wab)


INPROJ_SUB = 2 * B_HEAD_DIM


def _inproj_seq_kernel(x_ref, w_ref, wab_ref, z_ref, ab_ref, kt_ref, vt_ref, ct_ref, xb_ref):
    j = pl.program_id(1)
    bm, bn = z_ref.shape
    keep = kt_ref.shape[0]

    @pl.when(j == 0)
    def _():
        xb = x_ref[...].astype(BF16)
        xb_ref[...] = xb
        ab_ref[...] = _dot_nt(xb, wab_ref[...])
        kt_ref[...] = jnp.zeros(kt_ref.shape, F32)
        vt_ref[...] = jnp.zeros(vt_ref.shape, F32)

    tails = []
    for n0 in range(0, bn, INPROJ_SUB):
        cols = slice(n0, n0 + INPROJ_SUB)
        acc = _dot_nt(xb_ref[...], w_ref[cols, :])
        z_ref[:, cols] = acc.astype(z_ref.dtype)
        tail = acc[bm - keep:, :]
        kt_ref[:, cols] = jnp.where(j == K_BLOCK, tail, kt_ref[:, cols])
        vt_ref[:, cols] = jnp.where(j == V_BLOCK, tail, vt_ref[:, cols])
        tails.append(acc[bm - SUBLANES:, :])

    @pl.when((j >= CONV_BLOCK0) & (j < CONV_BLOCK0 + CONV_BLOCKS))
    def _():
        ct_ref[...] = jnp.concatenate(tails, axis=1)


def _inproj_seq(x3, w, wab, n_main, keep):
    nseq, t, d = x3.shape
    bm, bn = t, d
    conv_blk = lambda i, j: jnp.clip(j - CONV_BLOCK0, 0, CONV_BLOCKS - 1)
    return pl.pallas_call(
        _inproj_seq_kernel,
        grid=(nseq, n_main // bn),
        in_specs=[
            pl.BlockSpec((None, bm, d), lambda i, j: (i, 0, 0)),
            pl.BlockSpec((bn, d), lambda i, j: (j, 0)),
            pl.BlockSpec((LANES, d), lambda i, j: (0, 0)),
        ],
        out_specs=[
            pl.BlockSpec((None, bm, bn), lambda i, j: (i, 0, j)),
            pl.BlockSpec((None, bm, LANES), lambda i, j: (i, 0, 0)),
            pl.BlockSpec((None, keep, bn), lambda i, j: (i, 0, 0)),
            pl.BlockSpec((None, keep, bn), lambda i, j: (i, 0, 0)),
            pl.BlockSpec((None, SUBLANES, bn), lambda i, j: (i, 0, conv_blk(i, j))),
        ],
        out_shape=[
            jax.ShapeDtypeStruct((nseq, t, n_main), BF16),
            jax.ShapeDtypeStruct((nseq, t, LANES), F32),
            jax.ShapeDtypeStruct((nseq, keep, bn), F32),
            jax.ShapeDtypeStruct((nseq, keep, bn), F32),
            jax.ShapeDtypeStruct((nseq, SUBLANES, CONV_BLOCKS * bn), F32),
        ],
        scratch_shapes=[pltpu.VMEM((bm, d), BF16)],
        compiler_params=_params("arbitrary", "arbitrary"),
        name="inproj_seq",
    )(x3, w, wab)


def _mem_kv_kernel(x_ref, w_ref, k_ref, v_ref):
    acc = _dot(x_ref[...].astype(BF16), w_ref[...])
    mt, heads, dh = k_ref.shape
    k_ref[...] = acc[:, 0:heads * dh].reshape(mt, heads, dh)
    v_ref[...] = acc[:, heads * dh:2 * heads * dh].reshape(mt, heads, dh)


def _mem_kv(mem, w):
    bsz, mt, d = mem.shape
    out = jax.ShapeDtypeStruct((bsz, mt, C_HEADS, C_HEAD_DIM), F32)
    return pl.pallas_call(
        _mem_kv_kernel,
        grid=(bsz,),
        in_specs=[
            pl.BlockSpec((None, mt, d), lambda b: (b, 0, 0)),
            pl.BlockSpec(w.shape, lambda b: (0, 0)),
        ],
        out_specs=[pl.BlockSpec((None, mt, C_HEADS, C_HEAD_DIM), lambda b: (b, 0, 0, 0))] * 2,
        out_shape=[out, out],
        compiler_params=_params("arbitrary"),
        name="mem_kv",
    )(mem, w)


LOG2E = 1.4426950408889634


def _bias_col0(window):
    return (window - A_REL_CLIP + 1) // LANES * LANES


def _bias_table_kernel(rev_ref, o_ref):
    heads, tq, cols = o_ref.shape
    width = rev_ref.shape[1]
    lane = lax.broadcasted_iota(jnp.int32, (heads, width), 1)
    w = jnp.where(lane < cols, (rev_ref[...] - rev_ref[:, 0:1]) * LOG2E, 0.0)
    for h in range(heads):
        rows = jnp.broadcast_to(w[h:h + 1, :], (tq, width))
        o_ref[h] = pltpu.roll(rows, 0, axis=1, stride=1, stride_axis=0)[:, 0:cols]


def _bias_table(rel_bias, tq, band):
    heads = rel_bias.shape[0]
    window = band - tq
    col0 = _bias_col0(window)
    cols = band - col0
    width = 2 * A_REL_CLIP
    assert window - col0 == A_REL_CLIP and window - (band - 1) >= -A_REL_CLIP
    assert cols + tq <= width and width % LANES == 0
    rev = rel_bias[:, ::-1][:, 0:width]
    return pl.pallas_call(
        _bias_table_kernel,
        out_shape=jax.ShapeDtypeStruct((heads, tq, cols), F32),
        compiler_params=pltpu.CompilerParams(vmem_limit_bytes=VMEM_LIMIT_BYTES),
        name="bias_table",
    )(rev)


A_GROUP_WIDTH = A_GROUP * A_HEAD_DIM


def _lane_head(tq):
    return jnp.right_shift(lax.broadcasted_iota(jnp.int32, (tq, A_GROUP_WIDTH), 1), A_HEAD_DIM.bit_length() - 1)


def _attn_scores(qg, kg, bias_g, valid_from):
    lane_head = _lane_head(qg.shape[0])
    qg = qg.astype(BF16)
    zero = jnp.zeros_like(qg)
    qm = jnp.concatenate([jnp.where(lane_head == h, qg, zero) for h in range(A_GROUP)], axis=0)
    s = _dot_nt(qm, kg)
    col0 = s.shape[1] - bias_g.shape[1]
    s = jnp.concatenate([s[:, 0:col0], s[:, col0:] + bias_g], axis=1)
    if valid_from is not None:
        col = lax.broadcasted_iota(jnp.int32, s.shape, 1)
        s = jnp.where(col >= valid_from, s, NEG_INF)
    m = jnp.max(s, axis=-1, keepdims=True)
    p = jnp.exp2(s - m)
    return p.astype(BF16), 1.0 / jnp.sum(p, axis=-1, keepdims=True)


def _attn_values(p, rl, vg):
    tq = p.shape[0] // A_GROUP
    o = _dot(p, vg)
    if rl.shape[1] == 1:
        o = o * rl
    else:
        o = o * jnp.concatenate([rl] * (A_GROUP_WIDTH // LANES), axis=1)
    lane_head = _lane_head(tq)
    out = jnp.zeros((tq, A_GROUP_WIDTH), F32)
    for h in range(A_GROUP):
        out = jnp.where(lane_head == h, o[h * tq:(h + 1) * tq], out)
    return out


A_CHUNKS_PER_STEP = 8


def _attn_prompt_kernel(q_ref, k_ref, v_ref, gate_ref, bias_ref, o_ref, kpad_ref, vpad_ref, *, window, tq):
    s = pl.program_id(1)
    band = window + tq
    gw = A_GROUP_WIDTH
    cps = q_ref.shape[0] // tq

    @pl.when(s == 0)
    def _():
        zeros = jnp.zeros((window, kpad_ref.shape[1]), BF16)
        kpad_ref[0:window, :] = zeros
        vpad_ref[0:window, :] = zeros
        kpad_ref[window:, :] = k_ref[...]
        vpad_ref[window:, :] = v_ref[...]

    def chunks(masked):
        for j in range(cps):
            c = s * cps + j
            start = pl.multiple_of(c * tq, tq)
            rows = slice(j * tq, (j + 1) * tq)
            for g in range(A_HEADS // A_GROUP):
                cols = slice(g * gw, (g + 1) * gw)
                p, rl = _attn_scores(q_ref[rows, cols], kpad_ref[pl.ds(start, band), cols], bias_ref[g],
                                     window - c * tq if masked else None)
                out = _attn_values(p, rl, vpad_ref[pl.ds(start, band), cols])
                o_ref[rows, cols] = (out * _sigmoid(gate_ref[rows, cols].astype(F32))).astype(o_ref.dtype)

    @pl.when(s * cps * tq < window)
    def _():
        chunks(masked=True)

    @pl.when(s * cps * tq >= window)
    def _():
        chunks(masked=False)


def _attn_prompt(z3, bias_g, tq, window):
    bsz, t, _ = z3.shape
    width = A_HEADS * A_HEAD_DIM
    rows = tq * min(A_CHUNKS_PER_STEP, t // tq)
    return pl.pallas_call(
        functools.partial(_attn_prompt_kernel, window=window, tq=tq),
        grid=(bsz, t // rows),
        in_specs=[
            pl.BlockSpec((None, rows, width), lambda b, s: (b, s, 0)),
            pl.BlockSpec((None, t, width), lambda b, s: (b, 0, K_BLOCK)),
            pl.BlockSpec((None, t, width), lambda b, s: (b, 0, V_BLOCK)),
            pl.BlockSpec((None, rows, width), lambda b, s: (b, s, GATE_BLOCK0)),
            pl.BlockSpec(bias_g.shape, lambda b, s: (0, 0, 0)),
        ],
        out_specs=pl.BlockSpec((None, rows, width), lambda b, s: (b, s, 0)),
        out_shape=jax.ShapeDtypeStruct((bsz, t, width), BF16),
        scratch_shapes=[pltpu.VMEM((t + window, width), BF16), pltpu.VMEM((t + window, width), BF16)],
        compiler_params=_params("arbitrary", "arbitrary"),
        name="attn_prompt",
    )(z3, z3, z3, z3, bias_g)


def _attn_sample_kernel(q_ref, kn_ref, vn_ref, gate_ref, kt_ref, vt_ref, bias_ref, o_ref, ko_ref, vo_ref):
    tq = q_ref.shape[0]
    lc = kt_ref.shape[1]
    gw = A_GROUP_WIDTH
    lane_head = _lane_head(tq)
    ko_ref[...] = jnp.concatenate([kt_ref[:, tq:], kn_ref[...].T], axis=1)
    vo_ref[...] = jnp.concatenate([vt_ref[:, tq:], vn_ref[...].T], axis=1)
    for g in range(A_HEADS // A_GROUP):
        cols = slice(g * gw, (g + 1) * gw)
        qg = q_ref[:, cols].astype(BF16)
        zero = jnp.zeros_like(qg)
        qm = jnp.concatenate([jnp.where(lane_head == h, qg, zero) for h in range(A_GROUP)], axis=0)
        s_c = _dot(qm, kt_ref[cols, :].astype(BF16))
        s_n = _dot_nt(qm, kn_ref[:, cols].astype(BF16))
        bias = bias_ref[g]
        col0 = lc + tq - bias.shape[1]
        s_c = jnp.concatenate([s_c[:, 0:col0], s_c[:, col0:] + bias[:, 0:lc - col0]], axis=1)
        s_n = s_n + bias[:, lc - col0:]
        m = jnp.maximum(jnp.max(s_c, axis=-1, keepdims=True), jnp.max(s_n, axis=-1, keepdims=True))
        p_c = jnp.exp2(s_c - m)
        p_n = jnp.exp2(s_n - m)
        rl = 1.0 / (jnp.sum(p_c, axis=-1, keepdims=True) + jnp.sum(p_n, axis=-1, keepdims=True))
        o = (_dot_nt(p_c.astype(BF16), vt_ref[cols, :].astype(BF16))
             + _dot(p_n.astype(BF16), vn_ref[:, cols].astype(BF16))) * rl
        out = jnp.zeros((tq, gw), F32)
        for h in range(A_GROUP):
            out = jnp.where(lane_head == h, o[h * tq:(h + 1) * tq], out)
        o_ref[:, cols] = (out * _sigmoid(gate_ref[:, cols].astype(F32))).astype(o_ref.dtype)


def _attn_sample(z3, kt, vt, bias_g):
    bsz, tq, _ = z3.shape
    width, lc = kt.shape[1:]
    cache = jax.ShapeDtypeStruct(kt.shape, kt.dtype)
    cache_spec = pl.BlockSpec((None, width, lc), lambda b: (b, 0, 0))
    return pl.pallas_call(
        _attn_sample_kernel,
        grid=(bsz,),
        in_specs=[
            pl.BlockSpec((None, tq, width), lambda b: (b, 0, 0)),
            pl.BlockSpec((None, tq, width), lambda b: (b, 0, K_BLOCK)),
            pl.BlockSpec((None, tq, width), lambda b: (b, 0, V_BLOCK)),
            pl.BlockSpec((None, tq, width), lambda b: (b, 0, GATE_BLOCK0)),
            cache_spec, cache_spec,
            pl.BlockSpec(bias_g.shape, lambda b: (0, 0, 0)),
        ],
        out_specs=[pl.BlockSpec((None, tq, width), lambda b: (b, 0, 0)), cache_spec, cache_spec],
        out_shape=[jax.ShapeDtypeStruct((bsz, tq, width), BF16), cache, cache],
        compiler_params=_params("arbitrary"),
        name="attn_sample",
    )(z3, z3, z3, z3, kt, vt, bias_g)


def _gdn_kernel(x_ref, ab_ref, gate_ref, conv0_ref, s0_ref, cw_ref, pcol_ref, ng_ref,
                o_ref, s_ref, xbuf_ref):
    c = pl.program_id(1)
    nb, ch, _ = x_ref.shape
    hd = B_HEAD_DIM
    width = B_HEADS * hd
    keep = B_CONV - 1
    shift_on_mxu = x_ref.dtype == BF16
    top = xbuf_ref.shape[1] - (0 if shift_on_mxu else ch)
    chains = [(bi, h) for bi in range(nb) for h in range(B_HEADS)]
    every = range(len(chains))

    @pl.when(c == 0)
    def _():
        hist = jnp.concatenate([jnp.zeros((nb, top - keep, xbuf_ref.shape[2]), F32),
                                conv0_ref[...].astype(F32)], axis=1)
        xbuf_ref[:, 0:top, :] = hist.astype(xbuf_ref.dtype)
        s_ref[...] = s0_ref[...].astype(F32)

    if shift_on_mxu:
        t_r = lax.broadcasted_iota(jnp.int32, (ch, top + ch), 0)
        u_c = lax.broadcasted_iota(jnp.int32, (ch, top + ch), 1)
        shift01 = jnp.concatenate([jnp.where(u_c == top + t_r - (keep - i), 1.0, 0.0) for i in range(keep)],
                                  axis=0).astype(BF16)
        shifted = [_dot(shift01, jnp.concatenate([xbuf_ref[bi], x_ref[bi]], axis=0)) for bi in range(nb)]
    else:
        xbuf_ref[:, top:top + ch, :] = x_ref[...]

    t_i = lax.broadcasted_iota(jnp.int32, (ch, ch), 0)
    s_i = lax.broadcasted_iota(jnp.int32, (ch, ch), 1)
    tri_incl = (t_i >= s_i)
    tri_strict = (t_i > s_i)
    eye = jnp.where(t_i == s_i, 1.0, 0.0)
    lane = lax.broadcasted_iota(jnp.int32, (ch, LANES), 1)
    lane_row = lax.broadcasted_iota(jnp.int32, (1, ch), 1)
    frame = lax.broadcasted_iota(jnp.int32, (ch, LANES), 0)

    gcum_col, beta_col, gcum_row = [], [], []
    for bi in range(nb):
        ab = ab_ref[bi]
        g = -jnp.exp(pcol_ref[0:1, :]) * jax.nn.softplus(ab + pcol_ref[1:2, :])
        step = 1
        while step < ch:
            g = g + jnp.where(frame >= step, pltpu.roll(g, step, axis=0), 0.0)
            step *= 2
        beta_col.append(_sigmoid(ab))
        gcum_col.append(g)
        gcum_row.append(g.T)

    def pick(x, idx):
        return jnp.sum(jnp.where(lane == idx, x, 0.0), axis=-1, keepdims=True)

    def conv_silu(bi, col0):
        cols = slice(col0, col0 + hd)
        if shift_on_mxu:
            acc = x_ref[bi, :, cols].astype(F32) * cw_ref[keep:keep + 1, cols]
            for i in range(keep):
                acc = acc + shifted[bi][i * ch:(i + 1) * ch, cols] * cw_ref[i:i + 1, cols]
        else:
            acc = xbuf_ref[bi, top:top + ch, cols] * cw_ref[keep:keep + 1, cols]
            for i in range(keep):
                acc = acc + xbuf_ref[bi, top - keep + i:top - keep + i + ch, cols] * cw_ref[i:i + 1, cols]
        return _silu(acc)

    def l2norm(x):
        return x * lax.rsqrt(jnp.sum(x * x, axis=-1, keepdims=True) + L2_EPS)

    q = [l2norm(conv_silu(bi, h * hd)) * (hd ** -0.5) for bi, h in chains]
    k = [l2norm(conv_silu(bi, width + h * hd)) for bi, h in chains]
    v = [conv_silu(bi, 2 * width + h * hd) for bi, h in chains]
    gc = [pick(gcum_col[bi], h) for bi, h in chains]
    bc = [pick(beta_col[bi], B_HEADS + h) for bi, h in chains]
    gr = [gcum_row[bi][h:h + 1, :] for bi, h in chains]
    glast = [jnp.sum(jnp.where(lane_row == ch - 1, gr[i], 0.0), axis=-1, keepdims=True) for i in every]
    dec_incl = [jnp.exp(jnp.where(tri_incl, gc[i] - gr[i], NEG_INF)) for i in every]
    eg = [jnp.exp(gc[i]) for i in every]
    nbc = [-bc[i] for i in every]

    qk_kk = [_dot_nt(jnp.concatenate([q[i], k[i]], axis=0).astype(BF16), k[i].astype(BF16)) for i in every]
    a_qk = [qk_kk[i][0:ch] * dec_incl[i] for i in every]

    tp = [jnp.concatenate([(nbc[i] * qk_kk[i][ch:2 * ch]) * jnp.where(tri_strict, dec_incl[i], 0.0), eye], axis=1)
          for i in every]
    t_half = lax.broadcasted_iota(jnp.int32, (ch, 2 * ch), 1) >= ch
    for _ in range(max(1, (ch - 1).bit_length())):
        tp_b = [tp[i].astype(BF16) for i in every]
        tp = [_dot(tp_b[i][:, 0:ch], tp_b[i]) + jnp.where(t_half, tp[i], 0.0) for i in every]
    rhs = [jnp.concatenate([bc[i] * v[i], (bc[i] * eg[i]) * k[i]], axis=1).astype(BF16) for i in every]
    x_sol = [_dot(tp[i][:, ch:2 * ch].astype(BF16), rhs[i]) for i in every]

    state = [s_ref[bi, h] for bi, h in chains]
    wq = [_dot(jnp.concatenate([x_sol[i][:, hd:2 * hd], q[i] * eg[i]], axis=0).astype(BF16),
               state[i].astype(BF16)) for i in every]
    wv_b = [(x_sol[i][:, 0:hd] - wq[i][0:ch]).astype(BF16) for i in every]
    o = [wq[i][ch:2 * ch] + _dot(a_qk[i].astype(BF16), wv_b[i]) for i in every]
    s_new = [jnp.exp(glast[i]) * state[i]
             + _dot_tn((k[i] * jnp.exp(glast[i] - gc[i])).astype(BF16), wv_b[i]) for i in every]
    for i, (bi, h) in enumerate(chains):
        s_ref[bi, h] = s_new[i]
        on = o[i] * lax.rsqrt(jnp.mean(o[i] * o[i], axis=-1, keepdims=True) + RMS_EPS) * ng_ref[...]
        gate = _sigmoid(gate_ref[bi, :, h * hd:(h + 1) * hd].astype(F32))
        o_ref[bi, :, h * hd:(h + 1) * hd] = (on * gate).astype(o_ref.dtype)

    if shift_on_mxu:
        xbuf_ref[...] = x_ref[:, ch - top:ch, :]
    else:
        xbuf_ref[:, top - keep:top, :] = xbuf_ref[:, top + ch - keep:top + ch, :]


GDN_BATCHES = 2


def _gdn(z3, ab3, conv0, s0, conv_w, pcol, norm_g, ch):
    bsz, t, _ = z3.shape
    nb = GDN_BATCHES
    width3 = 3 * B_HEADS * B_HEAD_DIM
    width = B_HEADS * B_HEAD_DIM
    if z3.dtype == BF16:
        xbuf = pltpu.VMEM((nb, 2 * SUBLANES, width3), BF16)
    else:
        xbuf = pltpu.VMEM((nb, SUBLANES + ch, width3), F32)
    return pl.pallas_call(
        _gdn_kernel,
        grid=(bsz // nb, t // ch),
        in_specs=[
            pl.BlockSpec((nb, ch, width3), lambda b, c: (b, c, 1)),
            pl.BlockSpec((nb, ch, LANES), lambda b, c: (b, c, 0)),
            pl.BlockSpec((nb, ch, width), lambda b, c: (b, c, GATE_BLOCK0 + 1)),
            pl.BlockSpec((nb, B_CONV - 1, width3), lambda b, c: (b, 0, 0)),
            pl.BlockSpec((nb, B_HEADS, B_HEAD_DIM, B_HEAD_DIM), lambda b, c: (b, 0, 0, 0)),
            pl.BlockSpec(conv_w.shape, lambda b, c: (0, 0)),
            pl.BlockSpec(pcol.shape, lambda b, c: (0, 0)),
            pl.BlockSpec(norm_g.shape, lambda b, c: (0, 0)),
        ],
        out_specs=[
            pl.BlockSpec((nb, ch, width), lambda b, c: (b, c, 0)),
            pl.BlockSpec((nb, B_HEADS, B_HEAD_DIM, B_HEAD_DIM), lambda b, c: (b, 0, 0, 0)),
        ],
        out_shape=[
            jax.ShapeDtypeStruct((bsz, t, width), BF16),
            jax.ShapeDtypeStruct((bsz, B_HEADS, B_HEAD_DIM, B_HEAD_DIM), F32),
        ],
        scratch_shapes=[xbuf],
        compiler_params=_params("arbitrary", "arbitrary"),
        name="gdn",
    )(z3, ab3, z3, conv0, s0, conv_w, pcol, norm_g)


MEMATTN_ROWS = 512


def _memattn_kernel(q_ref, gate_ref, mk_ref, mv_ref, o_ref, kb_ref, vb_ref):
    dh = C_HEAD_DIM

    @pl.when(pl.program_id(1) == 0)
    def _():
        kb_ref[...] = mk_ref[...].reshape(kb_ref.shape).astype(BF16)
        vb_ref[...] = mv_ref[...].reshape(vb_ref.shape).astype(BF16)

    tq = q_ref.shape[0]
    rb = min(MEMATTN_ROWS, tq)
    for r0 in range(0, tq, rb):
        for h in range(C_HEADS):
            cols = slice(h * dh, (h + 1) * dh)
            q = q_ref[r0:r0 + rb, cols].astype(BF16)
            s = _dot_nt(q, kb_ref[:, cols])
            m = jnp.max(s, axis=-1, keepdims=True)
            p = jnp.exp2(s - m)
            l = jnp.sum(p, axis=-1, keepdims=True)
            o = _dot(p.astype(BF16), vb_ref[:, cols]) * (1.0 / l)
            gate = _sigmoid(gate_ref[r0:r0 + rb, cols].astype(F32))
            o_ref[r0:r0 + rb, cols] = (o * gate).astype(o_ref.dtype)


def _memattn(z3, q_blk, mk, mv, tq):
    bsz, t, _ = z3.shape
    mem = mk.shape[1]
    width = C_HEADS * C_HEAD_DIM
    kv_spec = pl.BlockSpec((None, mem, C_HEADS, C_HEAD_DIM), lambda b, i: (b, 0, 0, 0))
    return pl.pallas_call(
        _memattn_kernel,
        grid=(bsz, t // tq),
        in_specs=[
            pl.BlockSpec((None, tq, width), lambda b, i: (b, i, q_blk)),
            pl.BlockSpec((None, tq, width), lambda b, i: (b, i, GATE_BLOCK0 + 2)),
            kv_spec,
            kv_spec,
        ],
        out_specs=pl.BlockSpec((None, tq, width), lambda b, i: (b, i, 0)),
        out_shape=jax.ShapeDtypeStruct((bsz, t, width), BF16),
        scratch_shapes=[pltpu.VMEM((mem, width), BF16), pltpu.VMEM((mem, width), BF16)],
        compiler_params=_params("arbitrary", "arbitrary"),
        name="memattn",
    )(z3, z3, mk, mv)


def _layer_norm(x, g, b):
    mu = jnp.mean(x, axis=-1, keepdims=True)
    xc = x - mu
    var = jnp.mean(xc * xc, axis=-1, keepdims=True)
    return xc * lax.rsqrt(var + LN_EPS) * g + b


FFN_PARTS = 2
FFN_HIDDEN_BLOCK = 1024


def _ffn_kernel(x_ref, oa_ref, ob_ref, oc_ref, wo_ref, w1_ref, w2_ref, vec_ref, b1_ref, y_ref, *, alpha, ff_blk):
    bm = x_ref.shape[0]
    rows = [pl.ds(r * (bm // FFN_PARTS), bm // FFN_PARTS) for r in range(FFN_PARTS)]
    parts = range(FFN_PARTS)

    merged = [(oa_ref[r, :].astype(F32) + ob_ref[r, :].astype(F32) + oc_ref[r, :].astype(F32)).astype(BF16)
              for r in rows]
    proj = [_dot(merged[i], wo_ref[...]) for i in parts]
    h = [_layer_norm(alpha * x_ref[rows[i], :] + proj[i], vec_ref[0:1, :], vec_ref[1:2, :]) for i in parts]
    hb = [h[i].astype(BF16) for i in parts]
    acc = [None] * FFN_PARTS
    for k0 in range(0, w1_ref.shape[1], ff_blk):
        f = [_dot(hb[i], w1_ref[:, k0:k0 + ff_blk]) + b1_ref[:, k0:k0 + ff_blk] for i in parts]
        f = [jnp.square(jnp.maximum(f[i], 0.0)).astype(BF16) for i in parts]
        d = [_dot(f[i], w2_ref[k0:k0 + ff_blk, :]) for i in parts]
        acc = [d[i] if acc[i] is None else acc[i] + d[i] for i in parts]
    for i in parts:
        y_ref[rows[i], :] = _layer_norm(alpha * h[i] + acc[i] + vec_ref[2:3, :], vec_ref[3:4, :], vec_ref[4:5, :])


def _ffn(x2, oa, ob, oc, wo, w1, w2, vec, b1, alpha, bm):
    m, d = x2.shape
    dff = w1.shape[1]
    const = dict(pipeline_mode=pl.Buffered(1))
    row = lambda i: (i, 0)
    return pl.pallas_call(
        functools.partial(_ffn_kernel, alpha=alpha, ff_blk=FFN_HIDDEN_BLOCK),
        grid=(m // bm,),
        in_specs=[
            pl.BlockSpec((bm, d), row),
            pl.BlockSpec((bm, d), row),
            pl.BlockSpec((bm, d), row),
            pl.BlockSpec((bm, d), row),
            pl.BlockSpec((d, d), lambda i: (0, 0), **const),
            pl.BlockSpec((d, dff), lambda i: (0, 0), **const),
            pl.BlockSpec((dff, d), lambda i: (0, 0), **const),
            pl.BlockSpec(vec.shape, lambda i: (0, 0), **const),
            pl.BlockSpec(b1.shape, lambda i: (0, 0), **const),
        ],
        out_specs=pl.BlockSpec((bm, d), row),
        out_shape=jax.ShapeDtypeStruct((m, d), F32),
        compiler_params=_params("arbitrary"),
        name="merge_ffn",
    )(x2, oa, ob, oc, wo, w1, w2, vec, b1)


INPROJ_ROWS = 2048


def _group_layer(x3, z_dtype, attn_fn, conv0, ssm0, mem_k, mem_v, lw, keep=None):
    bsz, t, d = x3.shape
    m = bsz * t
    ch = min(CHUNK, t)
    x2 = x3.reshape(m, d)
    if keep is not None:
        z3, ab3, *tails = _inproj_seq(x3, lw["w_in"], lw["wab"], lw["n_main"], keep)
        z2 = z3.reshape(m, -1)
    else:
        z2, ab = _inproj(x2, lw["w_in"], lw["wab"], lw["n_main"], z_dtype, min(INPROJ_ROWS, m), 2 * d)
        z3, ab3, tails = z2.reshape(bsz, t, -1), ab.reshape(bsz, t, LANES), None
    o_a = attn_fn(z3)
    o_b, ssm = _gdn(z3, ab3, conv0, ssm0, lw["conv_w"], lw["pcol"], lw["norm_g"], ch)
    o_c = _memattn(z3, QC_BLOCK, mem_k, mem_v, min(2048, t))
    y = _ffn(x2, o_a.reshape(m, d), o_b.reshape(m, d), o_c.reshape(m, d),
             lw["wo"], lw["w1"], lw["w2"], lw["vec"], lw["b1"], lw["alpha"], min(512, m))
    return y.reshape(bsz, t, d), z3, ssm, tails


def kernel(x_prompt, x_sample, cache_a_k, cache_a_v, state_b_conv, state_b_ssm, cache_mem_k, cache_mem_v, mem_prompt, w_in, w_b_conv, b_a_log, b_dt_bias, b_norm_g, a_rel_bias, w_mem_kv, w_out, ln1_g, ln1_b, w_ff1, b_ff1, w_ff2, b_ff2, ln2_g, ln2_b):
    depth = w_in.shape[0]
    bp, tp, d = x_prompt.shape
    bs, tn, _ = x_sample.shape
    window = A_LEFT_CHUNKS * CHUNK
    keep_p = min(window, tp)
    lc = cache_a_k.shape[2]
    n_main = w_in.shape[2] - 2 * B_HEADS
    a_width = A_HEADS * A_HEAD_DIM
    b_width3 = 3 * B_HEADS * B_HEAD_DIM
    alpha = (2.0 * depth) ** 0.25
    xp, xs = x_prompt, x_sample
    outs = [[] for _ in range(10)]
    for l in range(depth):
        w_in_b = _scale_cast(
            jnp.swapaxes(w_in, 1, 2), l,
            ((0, a_width, (A_HEAD_DIM ** -0.5) * LOG2E),
             (QC_BLOCK * d, QC_BLOCK * d + C_HEADS * C_HEAD_DIM, (C_HEAD_DIM ** -0.5) * LOG2E)), d)
        vec = jnp.zeros((SUBLANES, d), F32)
        vec = vec.at[0].set(ln1_g[l]).at[1].set(ln1_b[l]).at[2].set(b_ff2[l]).at[3].set(ln2_g[l]).at[4].set(ln2_b[l])
        pcol = jnp.zeros((SUBLANES, LANES), F32)
        pcol = pcol.at[0, 0:B_HEADS].set(b_a_log[l]).at[1, 0:B_HEADS].set(b_dt_bias[l])
        lw = dict(
            w_in=w_in_b, n_main=n_main,
            wab=jnp.pad(w_in_b[n_main:, :], ((0, LANES - 2 * B_HEADS), (0, 0))),
            conv_w=w_b_conv[l],
            pcol=pcol,
            norm_g=b_norm_g[l].reshape(1, B_HEAD_DIM),
            wo=w_out[l].astype(BF16), w1=w_ff1[l].astype(BF16), w2=w_ff2[l].astype(BF16),
            vec=vec, b1=b_ff1[l].reshape(1, -1), alpha=alpha,
        )
        band = window + CHUNK
        bias_tab = _bias_table(a_rel_bias[l], CHUNK, band)
        bias_cols = bias_tab.shape[2]
        col0 = band - bias_cols

        bias_p = bias_tab.reshape(A_HEADS // A_GROUP, A_GROUP * CHUNK, bias_cols)
        mk, mv = _mem_kv(mem_prompt, w_mem_kv[l].astype(BF16))
        conv0 = jnp.zeros((bp, B_CONV - 1, b_width3), F32)
        ssm0 = jnp.zeros((bp, B_HEADS, B_HEAD_DIM, B_HEAD_DIM), F32)
        xp, _, ssm_p, (k_tail, v_tail, conv_tail) = _group_layer(
            xp, BF16, lambda z3: _attn_prompt(z3, bias_p, CHUNK, window), conv0, ssm0, mk, mv, lw, keep=keep_p)
        outs[0].append(k_tail.reshape(bp, keep_p, A_HEADS, A_HEAD_DIM))
        outs[1].append(v_tail.reshape(bp, keep_p, A_HEADS, A_HEAD_DIM))
        outs[2].append(conv_tail[:, SUBLANES - (B_CONV - 1):])
        outs[3].append(ssm_p)
        outs[4].append(mk)
        outs[5].append(mv)

        bias_s = bias_tab[:, 0:tn, 0:lc + tn - col0].reshape(A_HEADS // A_GROUP, A_GROUP * tn, lc + tn - col0)
        kt = cache_a_k[l].transpose(0, 2, 3, 1).reshape(bs, a_width, lc)
        vt = cache_a_v[l].transpose(0, 2, 3, 1).reshape(bs, a_width, lc)
        advanced = {}

        def attn_sample(z3):
            o_a, advanced["k"], advanced["v"] = _attn_sample(z3, kt, vt, bias_s)
            return o_a

        xs, zs3, ssm_s, _ = _group_layer(
            xs, F32, attn_sample, state_b_conv[l], state_b_ssm[l], cache_mem_k[l], cache_mem_v[l], lw)
        outs[6].append(advanced["k"].reshape(bs, A_HEADS, A_HEAD_DIM, lc).transpose(0, 3, 1, 2))
        outs[7].append(advanced["v"].reshape(bs, A_HEADS, A_HEAD_DIM, lc).transpose(0, 3, 1, 2))
        qkvb_s = zs3[:, :, 3 * a_width:3 * a_width + b_width3]
        outs[8].append(jnp.concatenate([state_b_conv[l], qkvb_s], axis=1)[:, tn:])
        outs[9].append(ssm_s)
    return (xp, xs) + tuple(jnp.stack(o) for o in outs)
```

```python
import functools

import jax
import jax.numpy as jnp
from jax import lax
from jax.experimental import pallas as pl
from jax.experimental.pallas import tpu as pltpu

F32 = jnp.float32
BF16 = jnp.bfloat16

CHUNK = 64
A_HEADS = 16
A_HEAD_DIM = 64
A_GROUP = 4
A_LEFT_CHUNKS = 8
A_REL_CLIP = 128
B_HEADS = 8
B_HEAD_DIM = 128
B_CONV = 4
C_HEADS = 4
C_HEAD_DIM = 256
N_BRANCH = 3
LN_EPS = 1e-5
RMS_EPS = 1e-6
L2_EPS = 1e-6
NEG_INF = -1e30

LANES = 128
SUBLANES = 8
VMEM_LIMIT_BYTES = 56 * 1024 * 1024


def _params(*sem):
    return pltpu.CompilerParams(dimension_semantics=sem, vmem_limit_bytes=VMEM_LIMIT_BYTES)


def _dot(a, b):
    return jnp.dot(a, b, preferred_element_type=F32)


def _dot_nt(a, b):
    return lax.dot_general(a, b, (((1,), (1,)), ((), ())), preferred_element_type=F32)


def _dot_tn(a, b):
    return lax.dot_general(a, b, (((0,), (0,)), ((), ())), preferred_element_type=F32)


def _sigmoid(x):
    return 0.5 + 0.5 * jnp.tanh(0.5 * x)


def _silu(x):
    hx = 0.5 * x
    return hx + hx * jnp.tanh(hx)


def _scale_cast_kernel(w_ref, o_ref, *, row_scales):
    rows = lax.broadcasted_iota(jnp.int32, w_ref.shape, 0) + pl.program_id(0) * w_ref.shape[0]
    w = w_ref[...]
    for start, stop, scale in row_scales:
        w = jnp.where((rows >= start) & (rows < stop), w * scale, w)
    o_ref[...] = w.astype(o_ref.dtype)


def _scale_cast(w_t, layer, row_scales, bn):
    _, n, k = w_t.shape
    return pl.pallas_call(
        functools.partial(_scale_cast_kernel, row_scales=row_scales),
        grid=(pl.cdiv(n, bn),),
        in_specs=[pl.BlockSpec((None, bn, k), lambda j: (layer, j, 0))],
        out_specs=pl.BlockSpec((bn, k), lambda j: (j, 0)),
        out_shape=jax.ShapeDtypeStruct((n, k), BF16),
        compiler_params=_params("arbitrary"),
        name="scale_cast",
    )(w_t)


K_BLOCK, V_BLOCK, CONV_BLOCK0, CONV_BLOCKS = 1, 2, 3, 3
QC_BLOCK, GATE_BLOCK0 = 6, 7


def _inproj_kernel(x_ref, w_ref, wab_ref, z_ref, ab_ref, xb_ref):
    @pl.when(pl.program_id(1) == 0)
    def _():
        xb = x_ref[...].astype(BF16)
        xb_ref[...] = xb
        ab_ref[...] = _dot_nt(xb, wab_ref[...])

    z_ref[...] = _dot_nt(xb_ref[...], w_ref[...]).astype(z_ref.dtype)


def _inproj(x2, w, wab, n_main, out_dtype, bm, bn):
    m, d = x2.shape
    return pl.pallas_call(
        _inproj_kernel,
        grid=(m // bm, n_main // bn),
        in_specs=[
            pl.BlockSpec((bm, d), lambda i, j: (i, 0)),
            pl.BlockSpec((bn, d), lambda i, j: (j, 0)),
            pl.BlockSpec((LANES, d), lambda i, j: (0, 0)),
        ],
        out_specs=[
            pl.BlockSpec((bm, bn), lambda i, j: (i, j)),
            pl.BlockSpec((bm, LANES), lambda i, j: (i, 0)),
        ],
        out_shape=[
            jax.ShapeDtypeStruct((m, n_main), out_dtype),
            jax.ShapeDtypeStruct((m, LANES), F32),
        ],
        scratch_shapes=[pltpu.VMEM((bm, d), BF16)],
        compiler_params=_params("arbitrary", "arbitrary"),
        name="inproj",
    )(x2, w, wab)


INPROJ_SUB = 2 * B_HEAD_DIM


def _inproj_seq_kernel(x_ref, w_ref, wab_ref, z_ref, ab_ref, kt_ref, vt_ref, ct_ref, xb_ref):
    j = pl.program_id(1)
    bm, bn = z_ref.shape
    keep = kt_ref.shape[0]

    @pl.when(j == 0)
    def _():
        xb = x_ref[...].astype(BF16)
        xb_ref[...] = xb
        ab_ref[...] = _dot_nt(xb, wab_ref[...])
        kt_ref[...] = jnp.zeros(kt_ref.shape, F32)
        vt_ref[...] = jnp.zeros(vt_ref.shape, F32)

    tails = []
    for n0 in range(0, bn, INPROJ_SUB):
        cols = slice(n0, n0 + INPROJ_SUB)
        acc = _dot_nt(xb_ref[...], w_ref[cols, :])
        z_ref[:, cols] = acc.astype(z_ref.dtype)
        tail = acc[bm - keep:, :]
        kt_ref[:, cols] = jnp.where(j == K_BLOCK, tail, kt_ref[:, cols])
        vt_ref[:, cols] = jnp.where(j == V_BLOCK, tail, vt_ref[:, cols])
        tails.append(acc[bm - SUBLANES:, :])

    @pl.when((j >= CONV_BLOCK0) & (j < CONV_BLOCK0 + CONV_BLOCKS))
    def _():
        ct_ref[...] = jnp.concatenate(tails, axis=1)


def _inproj_seq(x3, w, wab, n_main, keep):
    nseq, t, d = x3.shape
    bm, bn = t, d
    conv_blk = lambda i, j: jnp.clip(j - CONV_BLOCK0, 0, CONV_BLOCKS - 1)
    return pl.pallas_call(
        _inproj_seq_kernel,
        grid=(nseq, n_main // bn),
        in_specs=[
            pl.BlockSpec((None, bm, d), lambda i, j: (i, 0, 0)),
            pl.BlockSpec((bn, d), lambda i, j: (j, 0)),
            pl.BlockSpec((LANES, d), lambda i, j: (0, 0)),
        ],
        out_specs=[
            pl.BlockSpec((None, bm, bn), lambda i, j: (i, 0, j)),
            pl.BlockSpec((None, bm, LANES), lambda i, j: (i, 0, 0)),
            pl.BlockSpec((None, keep, bn), lambda i, j: (i, 0, 0)),
            pl.BlockSpec((None, keep, bn), lambda i, j: (i, 0, 0)),
            pl.BlockSpec((None, SUBLANES, bn), lambda i, j: (i, 0, conv_blk(i, j))),
        ],
        out_shape=[
            jax.ShapeDtypeStruct((nseq, t, n_main), BF16),
            jax.ShapeDtypeStruct((nseq, t, LANES), F32),
            jax.ShapeDtypeStruct((nseq, keep, bn), F32),
            jax.ShapeDtypeStruct((nseq, keep, bn), F32),
            jax.ShapeDtypeStruct((nseq, SUBLANES, CONV_BLOCKS * bn), F32),
        ],
        scratch_shapes=[pltpu.VMEM((bm, d), BF16)],
        compiler_params=_params("arbitrary", "arbitrary"),
        name="inproj_seq",
    )(x3, w, wab)


def _mem_kv_kernel(x_ref, w_ref, k_ref, v_ref):
    acc = _dot(x_ref[...].astype(BF16), w_ref[...])
    mt, heads, dh = k_ref.shape
    k_ref[...] = acc[:, 0:heads * dh].reshape(mt, heads, dh)
    v_ref[...] = acc[:, heads * dh:2 * heads * dh].reshape(mt, heads, dh)


def _mem_kv(mem, w):
    bsz, mt, d = mem.shape
    out = jax.ShapeDtypeStruct((bsz, mt, C_HEADS, C_HEAD_DIM), F32)
    return pl.pallas_call(
        _mem_kv_kernel,
        grid=(bsz,),
        in_specs=[
            pl.BlockSpec((None, mt, d), lambda b: (b, 0, 0)),
            pl.BlockSpec(w.shape, lambda b: (0, 0)),
        ],
        out_specs=[pl.BlockSpec((None, mt, C_HEADS, C_HEAD_DIM), lambda b: (b, 0, 0, 0))] * 2,
        out_shape=[out, out],
        compiler_params=_params("arbitrary"),
        name="mem_kv",
    )(mem, w)


LOG2E = 1.4426950408889634


def _bias_col0(window):
    return (window - A_REL_CLIP + 1) // LANES * LANES


def _bias_table_kernel(rev_ref, o_ref):
    heads, tq, cols = o_ref.shape
    width = rev_ref.shape[1]
    lane = lax.broadcasted_iota(jnp.int32, (heads, width), 1)
    w = jnp.where(lane < cols, (rev_ref[...] - rev_ref[:, 0:1]) * LOG2E, 0.0)
    for h in range(heads):
        rows = jnp.broadcast_to(w[h:h + 1, :], (tq, width))
        o_ref[h] = pltpu.roll(rows, 0, axis=1, stride=1, stride_axis=0)[:, 0:cols]


def _bias_table(rel_bias, tq, band):
    heads = rel_bias.shape[0]
    window = band - tq
    col0 = _bias_col0(window)
    cols = band - col0
    width = 2 * A_REL_CLIP
    assert window - col0 == A_REL_CLIP and window - (band - 1) >= -A_REL_CLIP
    assert cols + tq <= width and width % LANES == 0
    rev = rel_bias[:, ::-1][:, 0:width]
    return pl.pallas_call(
        _bias_table_kernel,
        out_shape=jax.ShapeDtypeStruct((heads, tq, cols), F32),
        compiler_params=pltpu.CompilerParams(vmem_limit_bytes=VMEM_LIMIT_BYTES),
        name="bias_table",
    )(rev)


A_GROUP_WIDTH = A_GROUP * A_HEAD_DIM


def _lane_head(tq):
    return jnp.right_shift(lax.broadcasted_iota(jnp.int32, (tq, A_GROUP_WIDTH), 1), A_HEAD_DIM.bit_length() - 1)


def _attn_scores(qg, kg, bias_g, valid_from):
    lane_head = _lane_head(qg.shape[0])
    qg = qg.astype(BF16)
    zero = jnp.zeros_like(qg)
    qm = jnp.concatenate([jnp.where(lane_head == h, qg, zero) for h in range(A_GROUP)], axis=0)
    s = _dot_nt(qm, kg)
    col0 = s.shape[1] - bias_g.shape[1]
    s = jnp.concatenate([s[:, 0:col0], s[:, col0:] + bias_g], axis=1)
    if valid_from is not None:
        col = lax.broadcasted_iota(jnp.int32, s.shape, 1)
        s = jnp.where(col >= valid_from, s, NEG_INF)
    m = jnp.max(s, axis=-1, keepdims=True)
    p = jnp.exp2(s - m)
    return p.astype(BF16), 1.0 / jnp.sum(p, axis=-1, keepdims=True)


def _attn_values(p, rl, vg):
    tq = p.shape[0] // A_GROUP
    o = _dot(p, vg)
    if rl.shape[1] == 1:
        o = o * rl
    else:
        o = o * jnp.concatenate([rl] * (A_GROUP_WIDTH // LANES), axis=1)
    lane_head = _lane_head(tq)
    out = jnp.zeros((tq, A_GROUP_WIDTH), F32)
    for h in range(A_GROUP):
        out = jnp.where(lane_head == h, o[h * tq:(h + 1) * tq], out)
    return out


A_CHUNKS_PER_STEP = 8


def _attn_prompt_kernel(q_ref, k_ref, v_ref, gate_ref, bias_ref, o_ref, kpad_ref, vpad_ref, *, window, tq):
    s = pl.program_id(1)
    band = window + tq
    gw = A_GROUP_WIDTH
    cps = q_ref.shape[0] // tq

    @pl.when(s == 0)
    def _():
        zeros = jnp.zeros((window, kpad_ref.shape[1]), BF16)
        kpad_ref[0:window, :] = zeros
        vpad_ref[0:window, :] = zeros
        kpad_ref[window:, :] = k_ref[...]
        vpad_ref[window:, :] = v_ref[...]

    def chunks(masked):
        for j in range(cps):
            c = s * cps + j
            start = pl.multiple_of(c * tq, tq)
            rows = slice(j * tq, (j + 1) * tq)
            for g in range(A_HEADS // A_GROUP):
                cols = slice(g * gw, (g + 1) * gw)
                p, rl = _attn_scores(q_ref[rows, cols], kpad_ref[pl.ds(start, band), cols], bias_ref[g],
                                     window - c * tq if masked else None)
                out = _attn_values(p, rl, vpad_ref[pl.ds(start, band), cols])
                o_ref[rows, cols] = (out * _sigmoid(gate_ref[rows, cols].astype(F32))).astype(o_ref.dtype)

    @pl.when(s * cps * tq < window)
    def _():
        chunks(masked=True)

    @pl.when(s * cps * tq >= window)
    def _():
        chunks(masked=False)


def _attn_prompt(z3, bias_g, tq, window):
    bsz, t, _ = z3.shape
    width = A_HEADS * A_HEAD_DIM
    rows = tq * min(A_CHUNKS_PER_STEP, t // tq)
    return pl.pallas_call(
        functools.partial(_attn_prompt_kernel, window=window, tq=tq),
        grid=(bsz, t // rows),
        in_specs=[
            pl.BlockSpec((None, rows, width), lambda b, s: (b, s, 0)),
            pl.BlockSpec((None, t, width), lambda b, s: (b, 0, K_BLOCK)),
            pl.BlockSpec((None, t, width), lambda b, s: (b, 0, V_BLOCK)),
            pl.BlockSpec((None, rows, width), lambda b, s: (b, s, GATE_BLOCK0)),
            pl.BlockSpec(bias_g.shape, lambda b, s: (0, 0, 0)),
        ],
        out_specs=pl.BlockSpec((None, rows, width), lambda b, s: (b, s, 0)),
        out_shape=jax.ShapeDtypeStruct((bsz, t, width), BF16),
        scratch_shapes=[pltpu.VMEM((t + window, width), BF16), pltpu.VMEM((t + window, width), BF16)],
        compiler_params=_params("arbitrary", "arbitrary"),
        name="attn_prompt",
    )(z3, z3, z3, z3, bias_g)


def _attn_sample_kernel(q_ref, kn_ref, vn_ref, gate_ref, kt_ref, vt_ref, bias_ref, o_ref, ko_ref, vo_ref):
    tq = q_ref.shape[0]
    lc = kt_ref.shape[1]
    gw = A_GROUP_WIDTH
    lane_head = _lane_head(tq)
    ko_ref[...] = jnp.concatenate([kt_ref[:, tq:], kn_ref[...].T], axis=1)
    vo_ref[...] = jnp.concatenate([vt_ref[:, tq:], vn_ref[...].T], axis=1)
    for g in range(A_HEADS // A_GROUP):
        cols = slice(g * gw, (g + 1) * gw)
        qg = q_ref[:, cols].astype(BF16)
        zero = jnp.zeros_like(qg)
        qm = jnp.concatenate([jnp.where(lane_head == h, qg, zero) for h in range(A_GROUP)], axis=0)
        s_c = _dot(qm, kt_ref[cols, :].astype(BF16))
        s_n = _dot_nt(qm, kn_ref[:, cols].astype(BF16))
        bias = bias_ref[g]
        col0 = lc + tq - bias.shape[1]
        s_c = jnp.concatenate([s_c[:, 0:col0], s_c[:, col0:] + bias[:, 0:lc - col0]], axis=1)
        s_n = s_n + bias[:, lc - col0:]
        m = jnp.maximum(jnp.max(s_c, axis=-1, keepdims=True), jnp.max(s_n, axis=-1, keepdims=True))
        p_c = jnp.exp2(s_c - m)
        p_n = jnp.exp2(s_n - m)
        rl = 1.0 / (jnp.sum(p_c, axis=-1, keepdims=True) + jnp.sum(p_n, axis=-1, keepdims=True))
        o = (_dot_nt(p_c.astype(BF16), vt_ref[cols, :].astype(BF16))
             + _dot(p_n.astype(BF16), vn_ref[:, cols].astype(BF16))) * rl
        out = jnp.zeros((tq, gw), F32)
        for h in range(A_GROUP):
            out = jnp.where(lane_head == h, o[h * tq:(h + 1) * tq], out)
        o_ref[:, cols] = (out * _sigmoid(gate_ref[:, cols].astype(F32))).astype(o_ref.dtype)


def _attn_sample(z3, kt, vt, bias_g):
    bsz, tq, _ = z3.shape
    width, lc = kt.shape[1:]
    cache = jax.ShapeDtypeStruct(kt.shape, kt.dtype)
    cache_spec = pl.BlockSpec((None, width, lc), lambda b: (b, 0, 0))
    return pl.pallas_call(
        _attn_sample_kernel,
        grid=(bsz,),
        in_specs=[
            pl.BlockSpec((None, tq, width), lambda b: (b, 0, 0)),
            pl.BlockSpec((None, tq, width), lambda b: (b, 0, K_BLOCK)),
            pl.BlockSpec((None, tq, width), lambda b: (b, 0, V_BLOCK)),
            pl.BlockSpec((None, tq, width), lambda b: (b, 0, GATE_BLOCK0)),
            cache_spec, cache_spec,
            pl.BlockSpec(bias_g.shape, lambda b: (0, 0, 0)),
        ],
        out_specs=[pl.BlockSpec((None, tq, width), lambda b: (b, 0, 0)), cache_spec, cache_spec],
        out_shape=[jax.ShapeDtypeStruct((bsz, tq, width), BF16), cache, cache],
        compiler_params=_params("arbitrary"),
        name="attn_sample",
    )(z3, z3, z3, z3, kt, vt, bias_g)


def _gdn_kernel(x_ref, ab_ref, gate_ref, conv0_ref, s0_ref, cw_ref, pcol_ref, ng_ref,
                o_ref, s_ref, xbuf_ref):
    c = pl.program_id(1)
    nb, ch, _ = x_ref.shape
    hd = B_HEAD_DIM
    width = B_HEADS * hd
    keep = B_CONV - 1
    shift_on_mxu = x_ref.dtype == BF16
    top = xbuf_ref.shape[1] - (0 if shift_on_mxu else ch)
    chains = [(bi, h) for bi in range(nb) for h in range(B_HEADS)]
    every = range(len(chains))

    @pl.when(c == 0)
    def _():
        hist = jnp.concatenate([jnp.zeros((nb, top - keep, xbuf_ref.shape[2]), F32),
                                conv0_ref[...].astype(F32)], axis=1)
        xbuf_ref[:, 0:top, :] = hist.astype(xbuf_ref.dtype)
        s_ref[...] = s0_ref[...].astype(F32)

    if shift_on_mxu:
        t_r = lax.broadcasted_iota(jnp.int32, (ch, top + ch), 0)
        u_c = lax.broadcasted_iota(jnp.int32, (ch, top + ch), 1)
        shift01 = jnp.concatenate([jnp.where(u_c == top + t_r - (keep - i), 1.0, 0.0) for i in range(keep)],
                                  axis=0).astype(BF16)
        shifted = [_dot(shift01, jnp.concatenate([xbuf_ref[bi], x_ref[bi]], axis=0)) for bi in range(nb)]
    else:
        xbuf_ref[:, top:top + ch, :] = x_ref[...]

    t_i = lax.broadcasted_iota(jnp.int32, (ch, ch), 0)
    s_i = lax.broadcasted_iota(jnp.int32, (ch, ch), 1)
    tri_incl = (t_i >= s_i)
    tri_strict = (t_i > s_i)
    eye = jnp.where(t_i == s_i, 1.0, 0.0)
    lane = lax.broadcasted_iota(jnp.int32, (ch, LANES), 1)
    lane_row = lax.broadcasted_iota(jnp.int32, (1, ch), 1)
    frame = lax.broadcasted_iota(jnp.int32, (ch, LANES), 0)

    gcum_col, beta_col, gcum_row = [], [], []
    for bi in range(nb):
        ab = ab_ref[bi]
        g = -jnp.exp(pcol_ref[0:1, :]) * jax.nn.softplus(ab + pcol_ref[1:2, :])
        step = 1
        while step < ch:
            g = g + jnp.where(frame >= step, pltpu.roll(g, step, axis=0), 0.0)
            step *= 2
        beta_col.append(_sigmoid(ab))
        gcum_col.append(g)
        gcum_row.append(g.T)

    def pick(x, idx):
        return jnp.sum(jnp.where(lane == idx, x, 0.0), axis=-1, keepdims=True)

    def conv_silu(bi, col0):
        cols = slice(col0, col0 + hd)
        if shift_on_mxu:
            acc = x_ref[bi, :, cols].astype(F32) * cw_ref[keep:keep + 1, cols]
            for i in range(keep):
                acc = acc + shifted[bi][i * ch:(i + 1) * ch, cols] * cw_ref[i:i + 1, cols]
        else:
            acc = xbuf_ref[bi, top:top + ch, cols] * cw_ref[keep:keep + 1, cols]
            for i in range(keep):
                acc = acc + xbuf_ref[bi, top - keep + i:top - keep + i + ch, cols] * cw_ref[i:i + 1, cols]
        return _silu(acc)

    def l2norm(x):
        return x * lax.rsqrt(jnp.sum(x * x, axis=-1, keepdims=True) + L2_EPS)

    q = [l2norm(conv_silu(bi, h * hd)) * (hd ** -0.5) for bi, h in chains]
    k = [l2norm(conv_silu(bi, width + h * hd)) for bi, h in chains]
    v = [conv_silu(bi, 2 * width + h * hd) for bi, h in chains]
    gc = [pick(gcum_col[bi], h) for bi, h in chains]
    bc = [pick(beta_col[bi], B_HEADS + h) for bi, h in chains]
    gr = [gcum_row[bi][h:h + 1, :] for bi, h in chains]
    glast = [jnp.sum(jnp.where(lane_row == ch - 1, gr[i], 0.0), axis=-1, keepdims=True) for i in every]
    dec_incl = [jnp.exp(jnp.where(tri_incl, gc[i] - gr[i], NEG_INF)) for i in every]
    eg = [jnp.exp(gc[i]) for i in every]
    nbc = [-bc[i] for i in every]

    qk_kk = [_dot_nt(jnp.concatenate([q[i], k[i]], axis=0).astype(BF16), k[i].astype(BF16)) for i in every]
    a_qk = [qk_kk[i][0:ch] * dec_incl[i] for i in every]

    tp = [jnp.concatenate([(nbc[i] * qk_kk[i][ch:2 * ch]) * jnp.where(tri_strict, dec_incl[i], 0.0), eye], axis=1)
          for i in every]
    t_half = lax.broadcasted_iota(jnp.int32, (ch, 2 * ch), 1) >= ch
    for _ in range(max(1, (ch - 1).bit_length())):
        tp_b = [tp[i].astype(BF16) for i in every]
        tp = [_dot(tp_b[i][:, 0:ch], tp_b[i]) + jnp.where(t_half, tp[i], 0.0) for i in every]
    rhs = [jnp.concatenate([bc[i] * v[i], (bc[i] * eg[i]) * k[i]], axis=1).astype(BF16) for i in every]
    x_sol = [_dot(tp[i][:, ch:2 * ch].astype(BF16), rhs[i]) for i in every]

    state = [s_ref[bi, h] for bi, h in chains]
    wq = [_dot(jnp.concatenate([x_sol[i][:, hd:2 * hd], q[i] * eg[i]], axis=0).astype(BF16),
               state[i].astype(BF16)) for i in every]
    wv_b = [(x_sol[i][:, 0:hd] - wq[i][0:ch]).astype(BF16) for i in every]
    o = [wq[i][ch:2 * ch] + _dot(a_qk[i].astype(BF16), wv_b[i]) for i in every]
    s_new = [jnp.exp(glast[i]) * state[i]
             + _dot_tn((k[i] * jnp.exp(glast[i] - gc[i])).astype(BF16), wv_b[i]) for i in every]
    for i, (bi, h) in enumerate(chains):
        s_ref[bi, h] = s_new[i]
        on = o[i] * lax.rsqrt(jnp.mean(o[i] * o[i], axis=-1, keepdims=True) + RMS_EPS) * ng_ref[...]
        gate = _sigmoid(gate_ref[bi, :, h * hd:(h + 1) * hd].astype(F32))
        o_ref[bi, :, h * hd:(h + 1) * hd] = (on * gate).astype(o_ref.dtype)

    if shift_on_mxu:
        xbuf_ref[...] = x_ref[:, ch - top:ch, :]
    else:
        xbuf_ref[:, top - keep:top, :] = xbuf_ref[:, top + ch - keep:top + ch, :]


GDN_BATCHES = 4


def _gdn(z3, ab3, conv0, s0, conv_w, pcol, norm_g, ch):
    bsz, t, _ = z3.shape
    nb = GDN_BATCHES
    width3 = 3 * B_HEADS * B_HEAD_DIM
    width = B_HEADS * B_HEAD_DIM
    if z3.dtype == BF16:
        xbuf = pltpu.VMEM((nb, 2 * SUBLANES, width3), BF16)
    else:
        xbuf = pltpu.VMEM((nb, SUBLANES + ch, width3), F32)
    return pl.pallas_call(
        _gdn_kernel,
        grid=(bsz // nb, t // ch),
        in_specs=[
            pl.BlockSpec((nb, ch, width3), lambda b, c: (b, c, 1)),
            pl.BlockSpec((nb, ch, LANES), lambda b, c: (b, c, 0)),
            pl.BlockSpec((nb, ch, width), lambda b, c: (b, c, GATE_BLOCK0 + 1)),
            pl.BlockSpec((nb, B_CONV - 1, width3), lambda b, c: (b, 0, 0)),
            pl.BlockSpec((nb, B_HEADS, B_HEAD_DIM, B_HEAD_DIM), lambda b, c: (b, 0, 0, 0)),
            pl.BlockSpec(conv_w.shape, lambda b, c: (0, 0)),
            pl.BlockSpec(pcol.shape, lambda b, c: (0, 0)),
            pl.BlockSpec(norm_g.shape, lambda b, c: (0, 0)),
        ],
        out_specs=[
            pl.BlockSpec((nb, ch, width), lambda b, c: (b, c, 0)),
            pl.BlockSpec((nb, B_HEADS, B_HEAD_DIM, B_HEAD_DIM), lambda b, c: (b, 0, 0, 0)),
        ],
        out_shape=[
            jax.ShapeDtypeStruct((bsz, t, width), BF16),
            jax.ShapeDtypeStruct((bsz, B_HEADS, B_HEAD_DIM, B_HEAD_DIM), F32),
        ],
        scratch_shapes=[xbuf],
        compiler_params=_params("arbitrary", "arbitrary"),
        name="gdn",
    )(z3, ab3, z3, conv0, s0, conv_w, pcol, norm_g)


MEMATTN_ROWS = 512


def _memattn_kernel(q_ref, gate_ref, mk_ref, mv_ref, o_ref, kb_ref, vb_ref):
    dh = C_HEAD_DIM

    @pl.when(pl.program_id(1) == 0)
    def _():
        kb_ref[...] = mk_ref[...].reshape(kb_ref.shape).astype(BF16)
        vb_ref[...] = mv_ref[...].reshape(vb_ref.shape).astype(BF16)

    tq = q_ref.shape[0]
    rb = min(MEMATTN_ROWS, tq)
    for r0 in range(0, tq, rb):
        for h in range(C_HEADS):
            cols = slice(h * dh, (h + 1) * dh)
            q = q_ref[r0:r0 + rb, cols].astype(BF16)
            s = _dot_nt(q, kb_ref[:, cols])
            m = jnp.max(s, axis=-1, keepdims=True)
            p = jnp.exp2(s - m)
            l = jnp.sum(p, axis=-1, keepdims=True)
            o = _dot(p.astype(BF16), vb_ref[:, cols]) * (1.0 / l)
            gate = _sigmoid(gate_ref[r0:r0 + rb, cols].astype(F32))
            o_ref[r0:r0 + rb, cols] = (o * gate).astype(o_ref.dtype)


def _memattn(z3, q_blk, mk, mv, tq):
    bsz, t, _ = z3.shape
    mem = mk.shape[1]
    width = C_HEADS * C_HEAD_DIM
    kv_spec = pl.BlockSpec((None, mem, C_HEADS, C_HEAD_DIM), lambda b, i: (b, 0, 0, 0))
    return pl.pallas_call(
        _memattn_kernel,
        grid=(bsz, t // tq),
        in_specs=[
            pl.BlockSpec((None, tq, width), lambda b, i: (b, i, q_blk)),
            pl.BlockSpec((None, tq, width), lambda b, i: (b, i, GATE_BLOCK0 + 2)),
            kv_spec,
            kv_spec,
        ],
        out_specs=pl.BlockSpec((None, tq, width), lambda b, i: (b, i, 0)),
        out_shape=jax.ShapeDtypeStruct((bsz, t, width), BF16),
        scratch_shapes=[pltpu.VMEM((mem, width), BF16), pltpu.VMEM((mem, width), BF16)],
        compiler_params=_params("arbitrary", "arbitrary"),
        name="memattn",
    )(z3, z3, mk, mv)


def _layer_norm(x, g, b):
    mu = jnp.mean(x, axis=-1, keepdims=True)
    xc = x - mu
    var = jnp.mean(xc * xc, axis=-1, keepdims=True)
    return xc * lax.rsqrt(var + LN_EPS) * g + b


FFN_PARTS = 2
FFN_HIDDEN_BLOCK = 1024


def _ffn_kernel(x_ref, oa_ref, ob_ref, oc_ref, wo_ref, w1_ref, w2_ref, vec_ref, b1_ref, y_ref, *, alpha, ff_blk):
    bm = x_ref.shape[0]
    rows = [pl.ds(r * (bm // FFN_PARTS), bm // FFN_PARTS) for r in range(FFN_PARTS)]
    parts = range(FFN_PARTS)

    merged = [(oa_ref[r, :].astype(F32) + ob_ref[r, :].astype(F32) + oc_ref[r, :].astype(F32)).astype(BF16)
              for r in rows]
    proj = [_dot(merged[i], wo_ref[...]) for i in parts]
    h = [_layer_norm(alpha * x_ref[rows[i], :] + proj[i], vec_ref[0:1, :], vec_ref[1:2, :]) for i in parts]
    hb = [h[i].astype(BF16) for i in parts]
    acc = [None] * FFN_PARTS
    for k0 in range(0, w1_ref.shape[1], ff_blk):
        f = [_dot(hb[i], w1_ref[:, k0:k0 + ff_blk]) + b1_ref[:, k0:k0 + ff_blk] for i in parts]
        f = [jnp.square(jnp.maximum(f[i], 0.0)).astype(BF16) for i in parts]
        d = [_dot(f[i], w2_ref[k0:k0 + ff_blk, :]) for i in parts]
        acc = [d[i] if acc[i] is None else acc[i] + d[i] for i in parts]
    for i in parts:
        y_ref[rows[i], :] = _layer_norm(alpha * h[i] + acc[i] + vec_ref[2:3, :], vec_ref[3:4, :], vec_ref[4:5, :])


def _ffn(x2, oa, ob, oc, wo, w1, w2, vec, b1, alpha, bm):
    m, d = x2.shape
    dff = w1.shape[1]
    const = dict(pipeline_mode=pl.Buffered(1))
    row = lambda i: (i, 0)
    return pl.pallas_call(
        functools.partial(_ffn_kernel, alpha=alpha, ff_blk=FFN_HIDDEN_BLOCK),
        grid=(m // bm,),
        in_specs=[
            pl.BlockSpec((bm, d), row),
            pl.BlockSpec((bm, d), row),
            pl.BlockSpec((bm, d), row),
            pl.BlockSpec((bm, d), row),
            pl.BlockSpec((d, d), lambda i: (0, 0), **const),
            pl.BlockSpec((d, dff), lambda i: (0, 0), **const),
            pl.BlockSpec((dff, d), lambda i: (0, 0), **const),
            pl.BlockSpec(vec.shape, lambda i: (0, 0), **const),
            pl.BlockSpec(b1.shape, lambda i: (0, 0), **const),
        ],
        out_specs=pl.BlockSpec((bm, d), row),
        out_shape=jax.ShapeDtypeStruct((m, d), F32),
        compiler_params=_params("arbitrary"),
        name="merge_ffn",
    )(x2, oa, ob, oc, wo, w1, w2, vec, b1)


INPROJ_ROWS = 2048


def _group_layer(x3, z_dtype, attn_fn, conv0, ssm0, mem_k, mem_v, lw, keep=None):
    bsz, t, d = x3.shape
    m = bsz * t
    ch = min(CHUNK, t)
    x2 = x3.reshape(m, d)
    if keep is not None:
        z3, ab3, *tails = _inproj_seq(x3, lw["w_in"], lw["wab"], lw["n_main"], keep)
        z2 = z3.reshape(m, -1)
    else:
        z2, ab = _inproj(x2, lw["w_in"], lw["wab"], lw["n_main"], z_dtype, min(INPROJ_ROWS, m), 2 * d)
        z3, ab3, tails = z2.reshape(bsz, t, -1), ab.reshape(bsz, t, LANES), None
    o_a = attn_fn(z3)
    o_b, ssm = _gdn(z3, ab3, conv0, ssm0, lw["conv_w"], lw["pcol"], lw["norm_g"], ch)
    o_c = _memattn(z3, QC_BLOCK, mem_k, mem_v, min(2048, t))
    y = _ffn(x2, o_a.reshape(m, d), o_b.reshape(m, d), o_c.reshape(m, d),
             lw["wo"], lw["w1"], lw["w2"], lw["vec"], lw["b1"], lw["alpha"], min(512, m))
    return y.reshape(bsz, t, d), z3, ssm, tails


def kernel(x_prompt, x_sample, cache_a_k, cache_a_v, state_b_conv, state_b_ssm, cache_mem_k, cache_mem_v, mem_prompt, w_in, w_b_conv, b_a_log, b_dt_bias, b_norm_g, a_rel_bias, w_mem_kv, w_out, ln1_g, ln1_b, w_ff1, b_ff1, w_ff2, b_ff2, ln2_g, ln2_b):
    depth = w_in.shape[0]
    bp, tp, d = x_prompt.shape
    bs, tn, _ = x_sample.shape
    window = A_LEFT_CHUNKS * CHUNK
    keep_p = min(window, tp)
    lc = cache_a_k.shape[2]
    n_main = w_in.shape[2] - 2 * B_HEADS
    a_width = A_HEADS * A_HEAD_DIM
    b_width3 = 3 * B_HEADS * B_HEAD_DIM
    alpha = (2.0 * depth) ** 0.25
    xp, xs = x_prompt, x_sample
    outs = [[] for _ in range(10)]
    for l in range(depth):
        w_in_b = _scale_cast(
            jnp.swapaxes(w_in, 1, 2), l,
            ((0, a_width, (A_HEAD_DIM ** -0.5) * LOG2E),
             (QC_BLOCK * d, QC_BLOCK * d + C_HEADS * C_HEAD_DIM, (C_HEAD_DIM ** -0.5) * LOG2E)), d)
        vec = jnp.zeros((SUBLANES, d), F32)
        vec = vec.at[0].set(ln1_g[l]).at[1].set(ln1_b[l]).at[2].set(b_ff2[l]).at[3].set(ln2_g[l]).at[4].set(ln2_b[l])
        pcol = jnp.zeros((SUBLANES, LANES), F32)
        pcol = pcol.at[0, 0:B_HEADS].set(b_a_log[l]).at[1, 0:B_HEADS].set(b_dt_bias[l])
        lw = dict(
            w_in=w_in_b, n_main=n_main,
            wab=jnp.pad(w_in_b[n_main:, :], ((0, LANES - 2 * B_HEADS), (0, 0))),
            conv_w=w_b_conv[l],
            pcol=pcol,
            norm_g=b_norm_g[l].reshape(1, B_HEAD_DIM),
            wo=w_out[l].astype(BF16), w1=w_ff1[l].astype(BF16), w2=w_ff2[l].astype(BF16),
            vec=vec, b1=b_ff1[l].reshape(1, -1), alpha=alpha,
        )
        band = window + CHUNK
        bias_tab = _bias_table(a_rel_bias[l], CHUNK, band)
        bias_cols = bias_tab.shape[2]
        col0 = band - bias_cols

        bias_p = bias_tab.reshape(A_HEADS // A_GROUP, A_GROUP * CHUNK, bias_cols)
        mk, mv = _mem_kv(mem_prompt, w_mem_kv[l].astype(BF16))
        conv0 = jnp.zeros((bp, B_CONV - 1, b_width3), F32)
        ssm0 = jnp.zeros((bp, B_HEADS, B_HEAD_DIM, B_HEAD_DIM), F32)
        xp, _, ssm_p, (k_tail, v_tail, conv_tail) = _group_layer(
            xp, BF16, lambda z3: _attn_prompt(z3, bias_p, CHUNK, window), conv0, ssm0, mk, mv, lw, keep=keep_p)
        outs[0].append(k_tail.reshape(bp, keep_p, A_HEADS, A_HEAD_DIM))
        outs[1].append(v_tail.reshape(bp, keep_p, A_HEADS, A_HEAD_DIM))
        outs[2].append(conv_tail[:, SUBLANES - (B_CONV - 1):])
        outs[3].append(ssm_p)
        outs[4].append(mk)
        outs[5].append(mv)

        bias_s = bias_tab[:, 0:tn, 0:lc + tn - col0].reshape(A_HEADS // A_GROUP, A_GROUP * tn, lc + tn - col0)
        kt = cache_a_k[l].transpose(0, 2, 3, 1).reshape(bs, a_width, lc)
        vt = cache_a_v[l].transpose(0, 2, 3, 1).reshape(bs, a_width, lc)
        advanced = {}

        def attn_sample(z3):
            o_a, advanced["k"], advanced["v"] = _attn_sample(z3, kt, vt, bias_s)
            return o_a

        xs, zs3, ssm_s, _ = _group_layer(
            xs, F32, attn_sample, state_b_conv[l], state_b_ssm[l], cache_mem_k[l], cache_mem_v[l], lw)
        outs[6].append(advanced["k"].reshape(bs, A_HEADS, A_HEAD_DIM, lc).transpose(0, 3, 1, 2))
        outs[7].append(advanced["v"].reshape(bs, A_HEADS, A_HEAD_DIM, lc).transpose(0, 3, 1, 2))
        qkvb_s = zs3[:, :, 3 * a_width:3 * a_width + b_width3]
        outs[8].append(jnp.concatenate([state_b_conv[l], qkvb_s], axis=1)[:, tn:])
        outs[9].append(ssm_s)
    return (xp, xs) + tuple(jnp.stack(o) for o in outs)
```
